```python
import jax, jax.numpy as jnp
from jax import lax
import numpy as np

D_MODEL = 1024
BATCH = 16
SEQ = 256
DEPTH = 2
DEC_BATCH = 8
DEC_SEQ = 4096
PAST_LEN = 256

GRID_W = 64
RET_H = 4
RET_DK = 64
RET_DV = 64
RET_W = RET_H * RET_DV
RET_CHUNK = 128
LRU_W = 256
LRU_BLOCKS = 4
LRU_BW = LRU_W // LRU_BLOCKS
CONV_W = 4
LRU_C = 8.0
MLA_H = 8
MLA_DN = 64
MLA_DR = 32
MLA_DV = 64
MLA_W = MLA_H * MLA_DV
Q_RANK = 256
KV_RANK = 128
Q_BLOCK = 128
ROPE_BASE = 10000.0
MIX_W = RET_W + LRU_W + MLA_W
PROJ_SIZES = (RET_H * RET_DK, RET_H * RET_DK, RET_W, RET_W, LRU_W, LRU_W, Q_RANK, KV_RANK, MLA_DR)
PROJ_W = sum(PROJ_SIZES)
N_EXPERTS = 32
TOP_K = 4
D_EXPERT = 1024
SWIGLU_ALPHA = 1.702
SWIGLU_LIMIT = 7.0
MOE_BLOCK = 256
EPS = 1e-6

kernel_name = 'hybrid_diffusion_retention_rglru_mla_moe_step'

F32 = jnp.float32


def rmsnorm(x, g):
    xf = x.astype(F32)
    y = xf * lax.rsqrt(jnp.mean(xf * xf, axis=-1, keepdims=True) + EPS)
    return (y * g.astype(F32)).astype(x.dtype)


def head_layernorm(x, g):
    B, T, H, Dv = x.shape
    xf = x.astype(F32)
    mu = jnp.mean(xf, axis=-1, keepdims=True)
    var = jnp.mean(jnp.square(xf - mu), axis=-1, keepdims=True)
    y = ((xf - mu) * lax.rsqrt(var + EPS)).reshape(B, T, H * Dv)
    return (y * g.astype(F32)).astype(x.dtype)


def axial_rope_tables(T, dim):
    q = dim // 4
    t = jnp.arange(T)
    row = (t // GRID_W).astype(F32)
    col = (t % GRID_W).astype(F32)
    inv = ROPE_BASE ** (-jnp.arange(q, dtype=F32) / q)
    ang_r = row[:, None] * inv
    ang_c = col[:, None] * inv
    return jnp.cos(ang_r), jnp.sin(ang_r), jnp.cos(ang_c), jnp.sin(ang_c)


def _rotate(x, cos, sin):
    x1, x2 = jnp.split(x, 2, axis=-1)
    return jnp.concatenate([x1 * cos - x2 * sin, x2 * cos + x1 * sin], axis=-1)


def rope_2d(x, tables):
    T = x.shape[1]
    shape = (T,) + (1,) * (x.ndim - 3) + (-1,)
    cr, sr, cc, sc = [a.reshape(shape).astype(x.dtype) for a in tables]
    xr, xc = jnp.split(x, 2, axis=-1)
    return jnp.concatenate([_rotate(xr, cr, sr), _rotate(xc, cc, sc)], axis=-1)


def retention_scan(q, k, v, log_gamma, s0):
    B, T, H, DK = q.shape
    DV = v.shape[-1]
    n = T // RET_CHUNK
    dt = q.dtype

    def chunks(a):
        return a.reshape(B, n, RET_CHUNK, H, a.shape[-1]).transpose(1, 0, 3, 2, 4)

    j = jnp.arange(RET_CHUNK, dtype=F32)
    lg = log_gamma.astype(F32)[:, None]
    diff = j[:, None] - j[None, :]
    lower = diff >= 0
    intra = jnp.where(lower, jnp.exp(jnp.where(lower, diff, 0.0) * lg[:, :, None]), 0.0).astype(dt)
    cross = jnp.exp((j + 1.0) * lg).astype(dt)
    into_state = jnp.exp((RET_CHUNK - 1.0 - j) * lg).astype(dt)
    carry_decay = jnp.exp(RET_CHUNK * lg).astype(dt)

    def step(s, qkv):
        qc, kc, vc = qkv
        a = jnp.einsum('bhcd,bhmd->bhcm', qc, kc) * intra
        o = jnp.einsum('bhcm,bhme->bhce', a, vc) + jnp.einsum('bhcd,bhde->bhce', qc, s) * cross[:, :, None]
        s = s * carry_decay[:, :, None] + jnp.einsum('bhmd,bhme->bhde', kc * into_state[:, :, None], vc)
        return s, o

    s_fin, o = lax.scan(step, s0.astype(dt), (chunks(q), chunks(k), chunks(v)))
    return o.transpose(1, 0, 3, 2, 4).reshape(B, T, H, DV), s_fin


def centred_dwconv(x, w, b):
    T = x.shape[1]
    left = CONV_W // 2
    xp = jnp.pad(x, ((0, 0), (left, CONV_W - 1 - left), (0, 0)))
    y = b
    for tap in range(CONV_W):
        y = y + xp[:, tap:tap + T] * w[tap]
    return y


def rg_lru(x, w_gates, b_gates, lam, h0):
    B, T, W = x.shape
    xb = x.reshape(B, T, LRU_BLOCKS, LRU_BW)
    gates = jnp.einsum('btnc,gncd->gbtnd', xb, w_gates).reshape(2, B, T, W) + b_gates[:, None, None, :]
    r = jax.nn.sigmoid(gates[0].astype(F32))
    i = jax.nn.sigmoid(gates[1].astype(F32))
    log_a = LRU_C * r * jax.nn.log_sigmoid(lam.astype(F32))
    a = jnp.exp(log_a)
    b = jnp.sqrt(-jnp.expm1(2.0 * log_a)) * (i * x.astype(F32))

    def combine(u, w):
        a1, b1 = u
        a2, b2 = w
        return a1 * a2, a2 * b1 + b2

    a_cum, b_cum = lax.associative_scan(combine, (a, b), axis=1)
    h = a_cum * h0.astype(F32)[:, None, :] + b_cum
    return h.astype(x.dtype), h[:, -1].astype(x.dtype)


def mla_attention(q_nope, q_rope, k_nope, k_rope, v):
    B, T, H, _ = q_nope.shape
    nb = T // Q_BLOCK
    scale = (MLA_DN + MLA_DR) ** -0.5

    def blocks(a):
        return a.reshape(B, nb, Q_BLOCK, H, a.shape[-1]).swapaxes(0, 1)

    def attend(qs):
        qn, qr = qs
        s = jnp.einsum('bqhd,bkhd->bhqk', qn, k_nope) + jnp.einsum('bqhr,bkr->bhqk', qr, k_rope)
        p = jax.nn.softmax(s.astype(F32) * scale, axis=-1).astype(v.dtype)
        return jnp.einsum('bhqk,bkhd->bqhd', p, v)

    o = lax.map(attend, (blocks(q_nope), blocks(q_rope)))
    return o.swapaxes(0, 1).reshape(B, T, H * MLA_DV)


def token_mixers(h, lp, ctx):
    B, T, _ = h.shape
    z = jnp.einsum('btd,dp->btp', h, lp['w_in'])
    offs = []
    acc = 0
    for s in PROJ_SIZES[:-1]:
        acc += s
        offs.append(acc)
    rq, rk, rv, rg, lx, ly, cq, ckv, kr = jnp.split(z, offs, axis=-1)
    latent = ctx is not None

    rq = rq.reshape(B, T, RET_H, RET_DK)
    rk = rk.reshape(B, T, RET_H, RET_DK) * (RET_DK ** -0.5)
    rv = rv.reshape(B, T, RET_H, RET_DV)
    if latent:
        ctx_ret, ctx_lru, ctx_ckv, ctx_kr = ctx
        tabs = axial_rope_tables(T, RET_DK)
        rq = rope_2d(rq, tabs)
        rk = rope_2d(rk, tabs)
        s0_f, s0_b = ctx_ret[:, 0], ctx_ret[:, 1]
        h0_f, h0_b = ctx_lru[:, 0], ctx_lru[:, 1]
    else:
        s0_f = s0_b = jnp.zeros((B, RET_H, RET_DK, RET_DV), h.dtype)
        h0_f = h0_b = jnp.zeros((B, LRU_W), h.dtype)
    log_gamma = jax.nn.log_sigmoid(lp['ret_decay'].astype(F32))
    o_f, sf = retention_scan(rq, rk, rv, log_gamma[0], s0_f)
    o_b, sb = retention_scan(rq[:, ::-1], rk[:, ::-1], rv[:, ::-1], log_gamma[1], s0_b)
    ret = head_layernorm(o_f + o_b[:, ::-1], lp['ret_norm']) * jax.nn.silu(rg)

    xc = centred_dwconv(lx, lp['conv_w'], lp['conv_b'])
    hf, lf = rg_lru(xc, lp['lru_gw'][0], lp['lru_gb'][0], lp['lru_lam'][0], h0_f)
    hb, lb = rg_lru(xc[:, ::-1], lp['lru_gw'][1], lp['lru_gb'][1], lp['lru_lam'][1], h0_b)
    lru = (hf + hb[:, ::-1]) * jax.nn.gelu(ly)

    q = jnp.einsum('btr,rk->btk', rmsnorm(cq, lp['q_norm']), lp['w_uq']).reshape(B, T, MLA_H, MLA_DN + MLA_DR)
    q_nope, q_rope = q[..., :MLA_DN], q[..., MLA_DN:]
    ckv_n = rmsnorm(ckv, lp['kv_norm'])
    if latent:
        tabs = axial_rope_tables(T, MLA_DR)
        q_rope = rope_2d(q_rope, tabs)
        ckv_all = jnp.concatenate([ctx_ckv.astype(h.dtype), ckv_n], axis=1)
        kr_all = jnp.concatenate([ctx_kr.astype(h.dtype), rope_2d(kr, tabs)], axis=1)
    else:
        ckv_all = ckv_n
        kr_all = kr
    S = ckv_all.shape[1]
    kv = jnp.einsum('bsr,rk->bsk', ckv_all, lp['w_ukv']).reshape(B, S, MLA_H, MLA_DN + MLA_DV)
    mla = mla_attention(q_nope, q_rope, kv[..., :MLA_DN], kr_all, kv[..., MLA_DN:])

    out = jnp.einsum('btm,md->btd', jnp.concatenate([ret, lru, mla], axis=-1), lp['w_out'])
    new_ctx = None if latent else (jnp.stack([sf, sb], axis=1), jnp.stack([lf, lb], axis=1), ckv_n, kr)
    return out, new_ctx


def moe_ffn(h, lp):
    N, D = h.shape
    logits = jnp.dot(h.astype(F32), lp['router_w'].astype(F32)) + lp['router_b'].astype(F32)
    top_val, top_idx = lax.top_k(logits, TOP_K)
    gate = jax.nn.softmax(top_val, axis=-1).astype(h.dtype)
    n_asg = N * TOP_K
    flat_e = top_idx.reshape(-1)
    flat_tok = jnp.repeat(jnp.arange(N, dtype=jnp.int32), TOP_K)
    flat_g = gate.reshape(-1)
    order = jnp.argsort(flat_e)
    e_sorted = flat_e[order]
    counts = jnp.bincount(flat_e, length=N_EXPERTS)
    padded = (counts + MOE_BLOCK - 1) // MOE_BLOCK * MOE_BLOCK
    pad_end = jnp.cumsum(padded)
    pad_start = pad_end - padded
    grp_start = jnp.cumsum(counts) - counts
    slot = pad_start[e_sorted] + jnp.arange(n_asg) - grp_start[e_sorted]
    n_blocks = (n_asg + MOE_BLOCK - 1) // MOE_BLOCK + N_EXPERTS
    n_slots = n_blocks * MOE_BLOCK
    slot_tok = jnp.zeros((n_slots,), jnp.int32).at[slot].set(flat_tok[order])
    slot_gate = jnp.zeros((n_slots,), h.dtype).at[slot].set(flat_g[order])
    block_expert = jnp.minimum(jnp.searchsorted(pad_end, jnp.arange(n_blocks) * MOE_BLOCK, side='right'), N_EXPERTS - 1)
    w_gu, b_gu, w_down, b_down = lp['w_gu'], lp['b_gu'], lp['w_down'], lp['b_down']

    def expert_block(out, args):
        tok, g_slot, e = args
        gu = h[tok] @ w_gu[e] + b_gu[e]
        g, u = jnp.split(gu, 2, axis=-1)
        g = jnp.minimum(g, SWIGLU_LIMIT)
        u = jnp.clip(u, -SWIGLU_LIMIT, SWIGLU_LIMIT)
        y = (g * jax.nn.sigmoid(SWIGLU_ALPHA * g) * (u + 1.0)) @ w_down[e] + b_down[e]
        return out.at[tok].add((y * g_slot[:, None]).astype(out.dtype)), None

    out, _ = lax.scan(expert_block, jnp.zeros_like(h),
                      (slot_tok.reshape(n_blocks, MOE_BLOCK), slot_gate.reshape(n_blocks, MOE_BLOCK), block_expert))
    return out


def layer_forward(x, mod, lp, ctx):
    shift1, scale1, gate1, shift2, scale2, gate2 = jnp.split(mod[:, None, :], 6, axis=-1)
    h = rmsnorm(x, lp['norm'][0]) * (1.0 + scale1) + shift1
    mix, new_ctx = token_mixers(h, lp, ctx)
    x = x + gate1 * rmsnorm(mix, lp['norm'][1])
    B, T, D = x.shape
    h = rmsnorm(x, lp['norm'][2]) * (1.0 + scale2) + shift2
    ff = moe_ffn(h.reshape(B * T, D), lp).reshape(B, T, D)
    x = x + gate2 * rmsnorm(ff, lp['norm'][3])
    return x, new_ctx


def setup_inputs(seed: int = 0) -> dict:
    key = jax.random.key(seed)
    ks = jax.random.split(key, 32)

    def nrm(k, shape, s):
        return jax.random.normal(k, shape, F32) * s

    gam = 1.0 - 2.0 ** (-5.0 - np.arange(RET_H))
    ret_logit = jnp.asarray(np.log(gam / (1.0 - gam)), F32)
    u = jax.random.uniform(ks[18], (DEPTH, 2, LRU_W), F32, minval=0.9, maxval=0.999)
    sig = u ** (1.0 / LRU_C)
    return {
        'x_prompt': nrm(ks[0], (BATCH, SEQ, D_MODEL), 1.0),
        'x_sample': nrm(ks[1], (DEC_BATCH, DEC_SEQ, D_MODEL), 1.0),
        'c': nrm(ks[2], (DEC_BATCH, D_MODEL), 1.0),
        'state_ret': nrm(ks[3], (DEC_BATCH, DEPTH, 2, RET_H, RET_DK, RET_DV), 0.3),
        'state_lru': nrm(ks[4], (DEC_BATCH, DEPTH, 2, LRU_W), 0.5),
        'cache_mla_ckv': nrm(ks[5], (DEC_BATCH, DEPTH, PAST_LEN, KV_RANK), 1.0),
        'cache_mla_krope': nrm(ks[6], (DEC_BATCH, DEPTH, PAST_LEN, MLA_DR), 1.0),
        'c_ctx': nrm(ks[7], (D_MODEL,), 1.0),
        'ada_w': nrm(ks[8], (DEPTH, D_MODEL, 6 * D_MODEL), 0.5 * D_MODEL ** -0.5),
        'ada_b': nrm(ks[9], (DEPTH, 6 * D_MODEL), 0.02),
        'norm_g': 1.0 + nrm(ks[10], (DEPTH, 4, D_MODEL), 0.01),
        'w_in': nrm(ks[11], (DEPTH, D_MODEL, PROJ_W), D_MODEL ** -0.5),
        'ret_decay': ret_logit + nrm(ks[12], (DEPTH, 2, RET_H), 0.1),
        'ret_norm': 1.0 + nrm(ks[13], (DEPTH, RET_W), 0.01),
        'conv_w': nrm(ks[14], (DEPTH, CONV_W, LRU_W), CONV_W ** -0.5),
        'conv_b': nrm(ks[15], (DEPTH, LRU_W), 0.01),
        'lru_gate_w': nrm(ks[16], (DEPTH, 2, 2, LRU_BLOCKS, LRU_BW, LRU_BW), LRU_BW ** -0.5),
        'lru_gate_b': nrm(ks[17], (DEPTH, 2, 2, LRU_W), 0.01),
        'lru_lambda': jnp.log(sig) - jnp.log1p(-sig),
        'mla_q_norm': 1.0 + nrm(ks[19], (DEPTH, Q_RANK), 0.01),
        'mla_kv_norm': 1.0 + nrm(ks[20], (DEPTH, KV_RANK), 0.01),
        'mla_w_uq': nrm(ks[21], (DEPTH, Q_RANK, MLA_H * (MLA_DN + MLA_DR)), Q_RANK ** -0.5),
        'mla_w_ukv': nrm(ks[22], (DEPTH, KV_RANK, MLA_H * (MLA_DN + MLA_DV)), KV_RANK ** -0.5),
        'w_out': nrm(ks[23], (DEPTH, MIX_W, D_MODEL), MIX_W ** -0.5),
        'router_w': nrm(ks[24], (DEPTH, D_MODEL, N_EXPERTS), D_MODEL ** -0.5),
        'router_b': nrm(ks[25], (DEPTH, N_EXPERTS), 0.01),
        'moe_w_gu': nrm(ks[26], (DEPTH, N_EXPERTS, D_MODEL, 2 * D_EXPERT), D_MODEL ** -0.5),
        'moe_b_gu': nrm(ks[27], (DEPTH, N_EXPERTS, 2 * D_EXPERT), 0.01),
        'moe_w_down': nrm(ks[28], (DEPTH, N_EXPERTS, D_EXPERT, D_MODEL), D_EXPERT ** -0.5),
        'moe_b_down': nrm(ks[29], (DEPTH, N_EXPERTS, D_MODEL), 0.01),
    }


def reference(x_prompt, x_sample, c, state_ret, state_lru, cache_mla_ckv, cache_mla_krope, c_ctx,
              ada_w, ada_b, norm_g, w_in, ret_decay, ret_norm, conv_w, conv_b, lru_gate_w, lru_gate_b,
              lru_lambda, mla_q_norm, mla_kv_norm, mla_w_uq, mla_w_ukv, w_out, router_w, router_b,
              moe_w_gu, moe_b_gu, moe_w_down, moe_b_down):
    y_prompt = x_prompt
    y_sample = x_sample
    cond_ctx = jax.nn.silu(c_ctx)[None, :]
    cond_lat = jax.nn.silu(c)
    ret_list, lru_list, ckv_list, kr_list = [], [], [], []
    for l in range(DEPTH):
        lp = {'norm': norm_g[l], 'w_in': w_in[l], 'ret_decay': ret_decay[l], 'ret_norm': ret_norm[l],
              'conv_w': conv_w[l], 'conv_b': conv_b[l], 'lru_gw': lru_gate_w[l], 'lru_gb': lru_gate_b[l],
              'lru_lam': lru_lambda[l], 'q_norm': mla_q_norm[l], 'kv_norm': mla_kv_norm[l],
              'w_uq': mla_w_uq[l], 'w_ukv': mla_w_ukv[l], 'w_out': w_out[l], 'router_w': router_w[l],
              'router_b': router_b[l], 'w_gu': moe_w_gu[l], 'b_gu': moe_b_gu[l], 'w_down': moe_w_down[l],
              'b_down': moe_b_down[l]}
        y_prompt, ctx_new = layer_forward(y_prompt, cond_ctx @ ada_w[l] + ada_b[l], lp, None)
        s_ret, s_lru, k_ckv, k_kr = ctx_new
        ret_list.append(s_ret)
        lru_list.append(s_lru)
        ckv_list.append(k_ckv)
        kr_list.append(k_kr)
        ctx_cached = (state_ret[:, l], state_lru[:, l], cache_mla_ckv[:, l], cache_mla_krope[:, l])
        y_sample, _ = layer_forward(y_sample, cond_lat @ ada_w[l] + ada_b[l], lp, ctx_cached)
    new_state_ret = jnp.stack(ret_list, axis=1)
    new_state_lru = jnp.stack(lru_list, axis=1)
    new_cache_mla_ckv = jnp.stack(ckv_list, axis=1)
    new_cache_mla_krope = jnp.stack(kr_list, axis=1)
    return (y_prompt, y_sample, new_state_ret, new_state_lru, new_cache_mla_ckv, new_cache_mla_krope)
```

```python
import functools
import math

import jax
import jax.numpy as jnp
from jax import lax
from jax.experimental import pallas as pl
from jax.experimental.pallas import tpu as pltpu

F32 = jnp.float32
BF16 = jnp.bfloat16

D_MODEL = 1024
DEPTH = 2
GRID_W = 64
RET_H = 4
RET_DK = 64
RET_W = 256
RET_CHUNK = 128
LRU_W = 256
LRU_BLOCKS = 4
LRU_BW = 64
MLA_H = 8
MLA_DN = 64
MLA_DR = 32
MLA_DV = 64
MLA_W = MLA_H * MLA_DV
Q_RANK = 256
KV_RANK = 128
ROPE_BASE = 10000.0
N_EXPERTS = 32
TOP_K = 4
D_EXPERT = 1024
SWIGLU_ALPHA = 1.702
SWIGLU_LIMIT = 7.0
EPS = 1e-6

LANES = 128
ROW_TILE = 256
MOE_TILE = 256
LRU_ROWS = 256
ATT_TQ = 256
KV_BUILD_ROWS = 512
PROJ_PAD = 2048
VMEM_LIMIT = 56 * 1024 * 1024


def _cparams(sem):
    return pltpu.CompilerParams(dimension_semantics=sem, vmem_limit_bytes=VMEM_LIMIT)


def _rms(x, g):
    return x * lax.rsqrt(jnp.mean(x * x, axis=-1, keepdims=True) + EPS) * g


def _rope(x, c, s, quarter):
    w = x.shape[-1]
    lane = lax.broadcasted_iota(jnp.int32, x.shape, 1)
    first = (lane % (2 * quarter)) < quarter
    partner = jnp.where(first, pltpu.roll(x, w - quarter, 1), pltpu.roll(x, quarter, 1))
    return x * c + partner * s


def _mod_kernel(cond_ref, w_ref, b_ref, o_ref):
    c = cond_ref[...]
    s = (c * jax.nn.sigmoid(c)).astype(BF16)
    o_ref[0] = jnp.dot(s, w_ref[0].astype(BF16), preferred_element_type=F32) + b_ref[0]


def _modulation(cond, ada_w, ada_b):
    nblk = 6
    return pl.pallas_call(
        _mod_kernel,
        grid=(DEPTH, nblk),
        in_specs=[
            pl.BlockSpec((16, D_MODEL), lambda l, j: (0, 0)),
            pl.BlockSpec((1, D_MODEL, D_MODEL), lambda l, j: (l, 0, j)),
            pl.BlockSpec((1, 1, D_MODEL), lambda l, j: (l, 0, j)),
        ],
        out_specs=pl.BlockSpec((1, 16, D_MODEL), lambda l, j: (l, 0, j)),
        out_shape=jax.ShapeDtypeStruct((DEPTH, 16, 6 * D_MODEL), F32),
        compiler_params=_cparams(("arbitrary", "arbitrary")),
        name="adaln_mod",
    )(cond, ada_w, ada_b.reshape(DEPTH, 1, 6 * D_MODEL))


def _inproj_kernel(x_ref, mod_ref, g0_ref, win_ref, rc_ref, rs_ref, mc_ref, ms_ref, qn_ref, kvn_ref, wuq_ref,
                   rq_ref, rk_ref, rv_ref, rg_ref, lx_ref, ly_ref, q_ref, ckv_ref, kr_ref):
    x = x_ref[...]
    mod = mod_ref[0]
    h = _rms(x, g0_ref[...]) * (1.0 + mod[1:2]) + mod[0:1]
    z = jnp.dot(h.astype(BF16), win_ref[...], preferred_element_type=F32)
    rc = rc_ref[...]
    rs = rs_ref[...]
    rq_ref[...] = _rope(z[:, 0:256], rc, rs, 16)
    rk_ref[...] = _rope(z[:, 256:512] * (RET_DK ** -0.5), rc, rs, 16)
    rv_ref[...] = z[:, 512:768]
    rg_ref[...] = z[:, 768:1024]
    lx_ref[...] = z[:, 1024:1280]
    ly_ref[...] = z[:, 1280:1536]
    cqn = _rms(z[:, 1536:1792], qn_ref[...])
    q = jnp.dot(cqn.astype(BF16), wuq_ref[...], preferred_element_type=F32)
    mc = mc_ref[...]
    ms = ms_ref[...]
    for h_i in range(MLA_H):
        sl = slice(h_i * LANES, (h_i + 1) * LANES)
        q_ref[:, sl] = _rope(q[:, sl], mc, ms, 8).astype(BF16)
    ckv_ref[...] = _rms(z[:, 1792:1920], kvn_ref[...])
    kr_ref[...] = _rope(z[:, 1920:2048], mc, ms, 8)


def _in_projection(x, mod, g0, win_p, tabs, qn, kvn, wuq_p, geo):
    n = x.shape[0]
    nt = n // ROW_TILE
    rc, rs, mc, ms = tabs
    row = lambda w: pl.BlockSpec((ROW_TILE, w), lambda i: (i, 0))
    full = lambda a: pl.BlockSpec(a.shape, lambda i: (0,) * a.ndim)
    tab = lambda w: pl.BlockSpec((ROW_TILE, w), lambda i: (geo.rope_block(i), 0))
    outs = [(256, F32)] * 6 + [(MLA_H * LANES, BF16), (LANES, F32), (LANES, F32)]
    return pl.pallas_call(
        _inproj_kernel,
        grid=(nt,),
        in_specs=[row(D_MODEL), pl.BlockSpec((1, 6, D_MODEL), lambda i: (geo.mod_group(i), 0, 0)),
                  full(g0), full(win_p), tab(256), tab(256), tab(LANES), tab(LANES), full(qn), full(kvn), full(wuq_p)],
        out_specs=[row(w) for w, _ in outs],
        out_shape=[jax.ShapeDtypeStruct((n, w), dt) for w, dt in outs],
        compiler_params=_cparams(("arbitrary",)),
        name="in_projection",
    )(x, mod, g0, win_p, rc, rs, mc, ms, qn, kvn, wuq_p)


def _ret_kernel(lg_ref, qf_ref, kf_ref, vf_ref, qb_ref, kb_ref, vb_ref, s0_ref,
                of_ref, ob_ref, sfin_ref, s_scr, intra_scr, cross_scr, into_scr, carry_scr):
    j = pl.program_id(1)
    c = RET_CHUNK
    row = lax.broadcasted_iota(jnp.int32, (c, c), 0)
    lane = lax.broadcasted_iota(jnp.int32, (c, c), 1)
    rowf = row.astype(F32)
    lanef = lane.astype(F32)

    @pl.when(j == 0)
    def _init():
        s_scr[...] = s0_ref[0]
        for d in range(2):
            for h in range(RET_H):
                lg = lg_ref[d, h]
                if d == 0:
                    keep = row >= lane
                    dist = rowf - lanef
                else:
                    keep = lane >= row
                    dist = lanef - rowf
                intra_scr[d, h] = jnp.where(keep, jnp.exp(jnp.where(keep, dist, 0.0) * lg), 0.0)
            for p in range(2):
                lgl = jnp.where(lane < RET_DK, lg_ref[d, 2 * p], lg_ref[d, 2 * p + 1])
                if d == 0:
                    cross_scr[d, p] = jnp.exp((rowf + 1.0) * lgl)
                    into_scr[d, p] = jnp.exp((c - 1.0 - rowf) * lgl)
                else:
                    cross_scr[d, p] = jnp.exp((c - rowf) * lgl)
                    into_scr[d, p] = jnp.exp(rowf * lgl)
                carry_scr[d, p] = jnp.exp(float(c) * lgl)

    same_head = (row < RET_DK) == (lane < RET_DK)

    def one_dir(d, q_ref, k_ref, v_ref, o_ref):
        for p in range(2):
            sl = slice(p * LANES, (p + 1) * LANES)
            q2 = q_ref[:, sl]
            k2 = k_ref[:, sl]
            v2 = v_ref[:, sl]
            q2b = q2.astype(BF16)
            o = jnp.zeros((c, LANES), F32)
            for e in range(2):
                sel = (lane >= RET_DK) if e else (lane < RET_DK)
                ke = jnp.where(sel, k2, 0.0).astype(BF16)
                ve = jnp.where(sel, v2, 0.0).astype(BF16)
                s = lax.dot_general(q2b, ke, (((1,), (1,)), ((), ())), preferred_element_type=F32)
                a = (s * intra_scr[d, 2 * p + e]).astype(BF16)
                o = o + jnp.dot(a, ve, preferred_element_type=F32)
            st = s_scr[d, p]
            o = o + jnp.dot(q2b, st.astype(BF16), preferred_element_type=F32) * cross_scr[d, p]
            kw = (k2 * into_scr[d, p]).astype(BF16)
            upd = lax.dot_general(kw, v2.astype(BF16), (((0,), (0,)), ((), ())), preferred_element_type=F32)
            s_scr[d, p] = st * carry_scr[d, p] + jnp.where(same_head, upd, 0.0)
            o_ref[:, sl] = o

    one_dir(0, qf_ref, kf_ref, vf_ref, of_ref)
    one_dir(1, qb_ref, kb_ref, vb_ref, ob_ref)

    @pl.when(j == pl.num_programs(1) - 1)
    def _fin():
        sfin_ref[0] = s_scr[...]


def _retention(rq, rk, rv, lg, s0, nb, t, row0):
    n = nb * t
    nc = t // RET_CHUNK
    base = row0 // RET_CHUNK
    fwd = lambda off: pl.BlockSpec((RET_CHUNK, RET_W), lambda b, j: (off + b * nc + j, 0))
    bwd = lambda off: pl.BlockSpec((RET_CHUNK, RET_W), lambda b, j: (off + b * nc + nc - 1 - j, 0))
    st = pl.BlockSpec((1, 2, 2, LANES, LANES), lambda b, j: (b, 0, 0, 0, 0))
    return pl.pallas_call(
        _ret_kernel,
        grid=(nb, nc),
        in_specs=[pl.BlockSpec(memory_space=pltpu.SMEM), fwd(base), fwd(base), fwd(base), bwd(base), bwd(base),
                  bwd(base), st],
        out_specs=[fwd(0), bwd(0), st],
        out_shape=[jax.ShapeDtypeStruct((n, RET_W), F32), jax.ShapeDtypeStruct((n, RET_W), F32),
                   jax.ShapeDtypeStruct((nb, 2, 2, LANES, LANES), F32)],
        scratch_shapes=[pltpu.VMEM((2, 2, LANES, LANES), F32), pltpu.VMEM((2, RET_H, RET_CHUNK, RET_CHUNK), F32),
                        pltpu.VMEM((2, 2, RET_CHUNK, LANES), F32), pltpu.VMEM((2, 2, RET_CHUNK, LANES), F32),
                        pltpu.VMEM((2, 2, RET_CHUNK, LANES), F32)],
        compiler_params=_cparams(("arbitrary", "arbitrary")),
        name="retention",
    )(lg, rq, rk, rv, rq, rk, rv, s0)


def _state_to_pairs(s):
    b = s.shape[0]
    s = s.reshape(b, 2, 2, 2, RET_DK, RET_DK)
    z = jnp.zeros_like(s[:, :, :, 0])
    top = jnp.concatenate([s[:, :, :, 0], z], axis=-1)
    bot = jnp.concatenate([z, s[:, :, :, 1]], axis=-1)
    return jnp.concatenate([top, bot], axis=-2)


def _pairs_to_state(s):
    b = s.shape[0]
    a = s[..., :RET_DK, :RET_DK]
    c = s[..., RET_DK:, RET_DK:]
    return jnp.stack([a, c], axis=3).reshape(b, 2, RET_H, RET_DK, RET_DK)


def _scan_rows(a, b, reverse):
    r = a.shape[0]
    rows = lax.broadcasted_iota(jnp.int32, a.shape, 0)
    s = 1
    while s < r:
        if reverse:
            a_s = pltpu.roll(a, r - s, 0)
            b_s = pltpu.roll(b, r - s, 0)
            m = rows < r - s
        else:
            a_s = pltpu.roll(a, s, 0)
            b_s = pltpu.roll(b, s, 0)
            m = rows >= s
        b = jnp.where(m, a * b_s + b, b)
        a = jnp.where(m, a * a_s, a)
        s *= 2
    return a, b


def _lru_kernel(c8_ref, cw_ref, cb_ref, wg_ref, bg_ref, lx_ref, ly_ref, h0_ref, o_ref, hfin_ref, xc_scr, hf_scr,
                *, t, r):
    nch = t // r
    cw = cw_ref[0]
    cb = cb_ref[0]
    wg = wg_ref[0]
    bg = bg_ref[0]
    c8 = c8_ref[0]

    def conv_body(c, carry):
        r0 = pl.multiple_of(c * r, r)
        cur = lx_ref[pl.ds(r0, r), :]
        prev = lx_ref[pl.ds(pl.multiple_of(jnp.maximum(r0 - 8, 0), 8), 8), :]
        nxt = lx_ref[pl.ds(pl.multiple_of(jnp.minimum(r0 + r, t - 8), 8), 8), :]
        prev = jnp.where(c > 0, prev, 0.0)
        nxt = jnp.where(c < nch - 1, nxt, 0.0)
        ext = jnp.concatenate([prev, cur, nxt], axis=0)
        acc = jnp.broadcast_to(cb, (r, LANES))
        for tap in range(4):
            sh = (2 - tap) % (r + 16)
            xs = ext if sh == 0 else pltpu.roll(ext, sh, 0)
            acc = acc + xs[8:8 + r] * cw[tap:tap + 1]
        xc_scr[pl.ds(r0, r), :] = acc
        return carry

    lax.fori_loop(0, nch, conv_body, 0)

    def gates(xc, d):
        g = jnp.dot(xc.astype(BF16), wg[:, d * 256:(d + 1) * 256], preferred_element_type=F32) + bg[:, d * 256:(d + 1) * 256]
        rg = jax.nn.sigmoid(g[:, :LANES])
        ig = jax.nn.sigmoid(g[:, LANES:])
        log_a = c8[d:d + 1] * rg
        a = jnp.exp(log_a)
        b = jnp.sqrt(jnp.tanh(-log_a) * (a * a + 1.0)) * (ig * xc)
        return a, b

    def fwd_body(c, h):
        r0 = pl.multiple_of(c * r, r)
        a, b = gates(xc_scr[pl.ds(r0, r), :], 0)
        a, b = _scan_rows(a, b, False)
        hc = a * h + b
        hf_scr[pl.ds(r0, r), :] = hc
        return hc[r - 1:r]

    h_f = lax.fori_loop(0, nch, fwd_body, h0_ref[0, 0:1, :])

    def bwd_body(i, h):
        r0 = pl.multiple_of((nch - 1 - i) * r, r)
        a, b = gates(xc_scr[pl.ds(r0, r), :], 1)
        a, b = _scan_rows(a, b, True)
        hc = a * h + b
        o_ref[pl.ds(r0, r), :] = ((hf_scr[pl.ds(r0, r), :] + hc) * jax.nn.gelu(ly_ref[pl.ds(r0, r), :])).astype(BF16)
        return hc[0:1]

    h_b = lax.fori_loop(0, nch, bwd_body, h0_ref[0, 1:2, :])
    hfin_ref[0, 0:1, :] = h_f
    hfin_ref[0, 1:2, :] = h_b


def _lru(lx, ly, h0, c8, cw, cb, wg, bg, nb, t, row0):
    base = row0 // t
    r = min(LRU_ROWS, t)
    seq = lambda off: pl.BlockSpec((t, LANES), lambda b, hh: (off + b, hh))
    par = lambda a: pl.BlockSpec((1,) + a.shape[1:], lambda b, hh: (hh,) + (0,) * (a.ndim - 1))
    st = pl.BlockSpec((1, 2, LANES), lambda b, hh: (b, 0, hh))
    return pl.pallas_call(
        functools.partial(_lru_kernel, t=t, r=r),
        grid=(nb, 2),
        in_specs=[par(c8), par(cw), par(cb), par(wg), par(bg), seq(base), seq(base), st],
        out_specs=[seq(0), st],
        out_shape=[jax.ShapeDtypeStruct((nb * t, LRU_W), BF16), jax.ShapeDtypeStruct((nb, 2, LRU_W), F32)],
        scratch_shapes=[pltpu.VMEM((t, LANES), F32), pltpu.VMEM((t, LANES), F32)],
        compiler_params=_cparams(("arbitrary", "arbitrary")),
        name="rg_lru",
    )(c8, cw, cb, wg, bg, lx, ly, h0)


def _mla_kernel(*refs, n_ctx, t, tq):
    if n_ctx:
        q_ref, ckv_ref, kr_ref, cckv_ref, ckr_ref, wk_ref, wv_ref = refs[:7]
    else:
        q_ref, ckv_ref, kr_ref, wk_ref, wv_ref = refs[:5]
    o_ref, kt_scr, v_scr, q_scr, s_scr, p_scr, o_scr = refs[-7:]
    qi = pl.program_id(1)

    def put(ckv, kr, c0):
        nrow = ckv.shape[0]
        ckv_b = ckv.astype(BF16)
        cat = jnp.concatenate([ckv_b, kr.astype(BF16)], axis=1)
        kt = lax.dot_general(wk_ref[...], cat, (((1,), (1,)), ((), ())), preferred_element_type=F32)
        kt_scr[:, c0:c0 + nrow] = kt.astype(BF16)
        v = jnp.dot(ckv_b, wv_ref[...], preferred_element_type=F32)
        for pp in range(MLA_H // 2):
            v_scr[pp, c0:c0 + nrow, :] = v[:, pp * LANES:(pp + 1) * LANES].astype(BF16)

    @pl.when(qi == 0)
    def _build():
        if n_ctx:
            put(cckv_ref[0], ckr_ref[0], 0)
        step = min(KV_BUILD_ROWS, t)
        for c in range(t // step):
            put(ckv_ref[c * step:(c + 1) * step, :], kr_ref[c * step:(c + 1) * step, :], n_ctx + c * step)

    for h in range(MLA_H):
        q_scr[h] = q_ref[:, h * LANES:(h + 1) * LANES]
    cexp = (MLA_DN + MLA_DR) ** -0.5 * math.log2(math.e)
    lane = lax.broadcasted_iota(jnp.int32, (tq, LANES), 1)

    def pair_body(pp, carry):
        outs = []
        for e in range(2):
            h = 2 * pp + e
            kt = kt_scr[pl.ds(pl.multiple_of(h * LANES, LANES), LANES), :]
            s_scr[...] = jnp.dot(q_scr[h], kt, preferred_element_type=F32)
            s = s_scr[...]
            m = jnp.max(s, axis=1, keepdims=True)
            p = jnp.exp2((s - m) * cexp)
            l = jnp.sum(p, axis=1, keepdims=True)
            p_scr[...] = p.astype(BF16)
            outs.append(jnp.dot(p_scr[...], v_scr[pp], preferred_element_type=F32) / l)
        o_scr[pp] = jnp.where(lane < MLA_DV, outs[0], outs[1])
        return carry

    lax.fori_loop(0, MLA_H // 2, pair_body, 0)
    for pp in range(MLA_H // 2):
        o_ref[:, pp * LANES:(pp + 1) * LANES] = o_scr[pp].astype(BF16)


def _mla(q, ckvn, kr, cache, wk_aug_t, wv, nb, t, row0):
    tq = min(ATT_TQ, t)
    nq = t // tq
    n_ctx = 0 if cache is None else cache[0].shape[1]
    s_len = n_ctx + t
    qspec = pl.BlockSpec((tq, MLA_H * LANES), lambda b, i: (row0 // tq + b * nq + i, 0))
    seq = pl.BlockSpec((t, LANES), lambda b, i: (row0 // t + b, 0))
    full = lambda a: pl.BlockSpec(a.shape, lambda b, i: (0,) * a.ndim)
    ins = [q, ckvn, kr]
    specs = [qspec, seq, seq]
    if n_ctx:
        cspec = pl.BlockSpec((1, n_ctx, LANES), lambda b, i: (b, 0, 0))
        ins += [cache[0], cache[1]]
        specs += [cspec, cspec]
    ins += [wk_aug_t, wv]
    specs += [full(wk_aug_t), full(wv)]
    return pl.pallas_call(
        functools.partial(_mla_kernel, n_ctx=n_ctx, t=t, tq=tq),
        grid=(nb, nq),
        in_specs=specs,
        out_specs=pl.BlockSpec((tq, MLA_W), lambda b, i: (b * nq + i, 0)),
        out_shape=jax.ShapeDtypeStruct((nb * t, MLA_W), BF16),
        scratch_shapes=[pltpu.VMEM((MLA_H * LANES, s_len), BF16), pltpu.VMEM((MLA_H // 2, s_len, LANES), BF16),
                        pltpu.VMEM((MLA_H, tq, LANES), BF16), pltpu.VMEM((tq, s_len), F32),
                        pltpu.VMEM((tq, s_len), BF16), pltpu.VMEM((MLA_H // 2, tq, LANES), F32)],
        compiler_params=_cparams(("arbitrary", "arbitrary")),
        name="mla_attention",
    )(*ins)


def _outproj_kernel(ofc_ref, obc_ref, lruc_ref, mlac_ref, ofl_ref, obl_ref, lrul_ref, mlal_ref, rg_ref, x_ref, mod_ref,
                    retn_ref, g1_ref, g2_ref, wout_ref, rw_ref, rb_ref, x1_ref, h2_ref, lg_ref, *, ctx_tiles):
    is_ctx = pl.program_id(0) < ctx_tiles
    pick = lambda c_ref, l_ref: jnp.where(is_ctx, c_ref[...], l_ref[...])
    o = pick(ofc_ref, ofl_ref) + pick(obc_ref, obl_ref)
    hid = lax.broadcasted_iota(jnp.int32, o.shape, 1) // RET_DK

    def head_sum(v):
        tot = jnp.zeros_like(v)
        for hh in range(RET_H):
            msk = hid == hh
            tot = jnp.where(msk, jnp.sum(jnp.where(msk, v, 0.0), axis=1, keepdims=True), tot)
        return tot

    mu = head_sum(o) * (1.0 / RET_DK)
    dl = o - mu
    var = head_sum(dl * dl) * (1.0 / RET_DK)
    rg = rg_ref[...]
    ret = dl * lax.rsqrt(var + EPS) * retn_ref[...] * (rg * jax.nn.sigmoid(rg))
    mix = jnp.concatenate([ret.astype(BF16), pick(lruc_ref, lrul_ref), pick(mlac_ref, mlal_ref)], axis=1)
    mo = jnp.dot(mix, wout_ref[...], preferred_element_type=F32)
    mod = mod_ref[0]
    x1 = x_ref[...] + mod[2:3] * _rms(mo, g1_ref[...])
    x1_ref[...] = x1
    h2 = _rms(x1, g2_ref[...]) * (1.0 + mod[4:5]) + mod[3:4]
    h2_ref[...] = h2
    lg_ref[...] = jnp.dot(h2.astype(BF16), rw_ref[...], preferred_element_type=F32) + rb_ref[...]


def _out_projection(ctx_mix, lat_mix, rg, x, mod, retn, g1, g2, wout, rw, rb, geo):
    n = x.shape[0]
    ct = geo.ctx_tiles
    row = lambda w: pl.BlockSpec((ROW_TILE, w), lambda i: (i, 0))
    crow = lambda w: pl.BlockSpec((ROW_TILE, w), lambda i: (jnp.minimum(i, ct - 1), 0))
    lrow = lambda w: pl.BlockSpec((ROW_TILE, w), lambda i: (jnp.maximum(i - ct, 0), 0))
    full = lambda a: pl.BlockSpec(a.shape, lambda i: (0,) * a.ndim)
    widths = (256, 256, 256, MLA_W)
    return pl.pallas_call(
        functools.partial(_outproj_kernel, ctx_tiles=ct),
        grid=(n // ROW_TILE,),
        in_specs=[crow(w) for w in widths] + [lrow(w) for w in widths] + [
                  row(256), row(D_MODEL),
                  pl.BlockSpec((1, 6, D_MODEL), lambda i: (geo.mod_group(i), 0, 0)),
                  full(retn), full(g1), full(g2), full(wout), full(rw), full(rb)],
        out_specs=[row(D_MODEL), row(D_MODEL), row(LANES)],
        out_shape=[jax.ShapeDtypeStruct((n, D_MODEL), F32), jax.ShapeDtypeStruct((n, D_MODEL), F32),
                   jax.ShapeDtypeStruct((n, LANES), F32)],
        compiler_params=_cparams(("arbitrary",)),
        name="out_projection",
    )(*ctx_mix, *lat_mix, rg, x, mod, retn, g1, g2, wout, rw, rb)


def _row_copy(src_hbm, src_row, dst, dst_row, sem):
    return pltpu.make_async_copy(src_hbm.at[pl.ds(src_row, 1), :], dst.at[pl.ds(dst_row, 1), :], sem)


def _moe_kernel(te_ref, nu_ref, tok_ref, tokn_ref, h_hbm, wgu_ref, bgu_ref, wd_ref, bd_ref, y_ref, x_scr, sem):
    del te_ref
    t = pl.program_id(0)
    n_used = nu_ref[0]
    slot = lax.rem(t, 2)

    def gather(idx_ref, buf):
        def body(r, carry):
            _row_copy(h_hbm, idx_ref[0, 0, r], x_scr.at[buf], r, sem.at[buf]).start()
            return carry
        lax.fori_loop(0, MOE_TILE, body, 0, unroll=8)

    @pl.when(jnp.logical_and(t == 0, n_used > 0))
    def _first():
        gather(tok_ref, 0)

    @pl.when(t + 1 < n_used)
    def _ahead():
        gather(tokn_ref, 1 - slot)

    @pl.when(t < n_used)
    def _compute():
        pltpu.make_async_copy(h_hbm.at[pl.ds(0, MOE_TILE), :], x_scr.at[slot], sem.at[slot]).wait()
        x = x_scr[slot].astype(BF16)
        gu = jnp.dot(x, wgu_ref[0], preferred_element_type=F32) + bgu_ref[0]
        g = jnp.minimum(gu[:, :D_EXPERT], SWIGLU_LIMIT)
        u = jnp.clip(gu[:, D_EXPERT:], -SWIGLU_LIMIT, SWIGLU_LIMIT)
        act = g * jax.nn.sigmoid(SWIGLU_ALPHA * g) * (u + 1.0)
        y_ref[...] = jnp.dot(act.astype(BF16), wd_ref[0], preferred_element_type=F32) + bd_ref[0]

    @pl.when(t >= n_used)
    def _idle():
        y_ref[...] = jnp.zeros_like(y_ref)


def _moe(h2, tile_expert, n_used, slot_tok, wgu, bgu, wd, bd):
    n_tiles = tile_expert.shape[0]
    tok3 = slot_tok.reshape(n_tiles, 1, MOE_TILE)
    grid_spec = pltpu.PrefetchScalarGridSpec(
        num_scalar_prefetch=2,
        grid=(n_tiles,),
        in_specs=[
            pl.BlockSpec((1, 1, MOE_TILE), lambda t, te, nu: (t, 0, 0), memory_space=pltpu.SMEM),
            pl.BlockSpec((1, 1, MOE_TILE), lambda t, te, nu: (jnp.minimum(t + 1, n_tiles - 1), 0, 0),
                         memory_space=pltpu.SMEM),
            pl.BlockSpec(memory_space=pl.ANY),
            pl.BlockSpec((1, D_MODEL, 2 * D_EXPERT), lambda t, te, nu: (te[t], 0, 0)),
            pl.BlockSpec((1, 1, 2 * D_EXPERT), lambda t, te, nu: (te[t], 0, 0)),
            pl.BlockSpec((1, D_EXPERT, D_MODEL), lambda t, te, nu: (te[t], 0, 0)),
            pl.BlockSpec((1, 1, D_MODEL), lambda t, te, nu: (te[t], 0, 0)),
        ],
        out_specs=pl.BlockSpec((MOE_TILE, D_MODEL), lambda t, te, nu: (t, 0)),
        scratch_shapes=[pltpu.VMEM((2, MOE_TILE, D_MODEL), F32), pltpu.SemaphoreType.DMA((2,))],
    )
    return pl.pallas_call(
        _moe_kernel,
        grid_spec=grid_spec,
        out_shape=jax.ShapeDtypeStruct((n_tiles * MOE_TILE, D_MODEL), F32),
        compiler_params=_cparams(("arbitrary",)),
        name="moe_experts",
    )(tile_expert, n_used, tok3, tok3, h2, wgu, bgu.reshape(N_EXPERTS, 1, -1), wd, bd.reshape(N_EXPERTS, 1, -1))


def _combine_kernel(inv_ref, y_hbm, gate_ref, x1_ref, mod_ref, g3_ref, o_ref, buf, sem):
    for k in range(TOP_K):
        def body(r, carry, k=k):
            _row_copy(y_hbm, inv_ref[0, 0, k * ROW_TILE + r], buf.at[k], r, sem).start()
            return carry
        lax.fori_loop(0, ROW_TILE, body, 0, unroll=8)
    for k in range(TOP_K):
        pltpu.make_async_copy(y_hbm.at[pl.ds(0, ROW_TILE), :], buf.at[k], sem).wait()
    gate = gate_ref[...]
    ff = gate[:, 0:1] * buf[0]
    for k in range(1, TOP_K):
        ff = ff + gate[:, k:k + 1] * buf[k]
    o_ref[...] = x1_ref[...] + mod_ref[0][5:6] * _rms(ff, g3_ref[...])


def _combine(y_sorted, inv_slot, gate, x1, mod, g3, geo):
    n = x1.shape[0]
    nt = n // ROW_TILE
    inv3 = inv_slot.reshape(nt, ROW_TILE, TOP_K).transpose(0, 2, 1).reshape(nt, 1, TOP_K * ROW_TILE)
    return pl.pallas_call(
        _combine_kernel,
        grid=(nt,),
        in_specs=[
            pl.BlockSpec((1, 1, TOP_K * ROW_TILE), lambda i: (i, 0, 0), memory_space=pltpu.SMEM),
            pl.BlockSpec(memory_space=pl.ANY),
            pl.BlockSpec((ROW_TILE, TOP_K), lambda i: (i, 0)),
            pl.BlockSpec((ROW_TILE, D_MODEL), lambda i: (i, 0)),
            pl.BlockSpec((1, 6, D_MODEL), lambda i: (geo.mod_group(i), 0, 0)),
            pl.BlockSpec((1, D_MODEL), lambda i: (0, 0)),
        ],
        out_specs=pl.BlockSpec((ROW_TILE, D_MODEL), lambda i: (i, 0)),
        out_shape=jax.ShapeDtypeStruct((n, D_MODEL), F32),
        scratch_shapes=[pltpu.VMEM((TOP_K, ROW_TILE, D_MODEL), F32), pltpu.SemaphoreType.DMA(())],
        compiler_params=_cparams(("arbitrary",)),
        name="moe_combine",
    )(inv3, y_sorted, gate, x1, mod, g3)


def _route(logits):
    n = logits.shape[0]
    top_val, top_idx = lax.top_k(logits, TOP_K)
    gate = jax.nn.softmax(top_val, axis=-1)
    n_asg = n * TOP_K
    flat_e = top_idx.reshape(-1)
    order = jnp.argsort(flat_e)
    e_sorted = flat_e[order]
    counts = jnp.bincount(flat_e, length=N_EXPERTS)
    padded = (counts + MOE_TILE - 1) // MOE_TILE * MOE_TILE
    pad_end = jnp.cumsum(padded)
    pad_start = pad_end - padded
    grp_start = jnp.cumsum(counts) - counts
    slot_sorted = pad_start[e_sorted] + jnp.arange(n_asg) - grp_start[e_sorted]
    n_tiles = n_asg // MOE_TILE + N_EXPERTS
    slot_tok = jnp.zeros((n_tiles * MOE_TILE,), jnp.int32).at[slot_sorted].set((order // TOP_K).astype(jnp.int32))
    inv_slot = jnp.zeros((n_asg,), jnp.int32).at[order].set(slot_sorted.astype(jnp.int32))
    tile_expert = jnp.minimum(jnp.searchsorted(pad_end, jnp.arange(n_tiles) * MOE_TILE, side='right'),
                              N_EXPERTS - 1).astype(jnp.int32)
    n_used = (pad_end[-1] // MOE_TILE).astype(jnp.int32).reshape(1)
    return gate, slot_tok, inv_slot.reshape(n, TOP_K), tile_expert, n_used


class _Geometry:
    def __init__(self, nb_ctx, t_ctx, nb_lat, t_lat):
        self.nb_ctx, self.t_ctx, self.nb_lat, self.t_lat = nb_ctx, t_ctx, nb_lat, t_lat
        self.n_ctx = nb_ctx * t_ctx
        self.n = self.n_ctx + nb_lat * t_lat
        self.ctx_tiles = self.n_ctx // ROW_TILE
        self.lat_tiles = t_lat // ROW_TILE

    def mod_group(self, i):
        return jnp.where(i < self.ctx_tiles, 0, 1 + (i - self.ctx_tiles) // self.lat_tiles)

    def rope_block(self, i):
        return jnp.where(i < self.ctx_tiles, self.lat_tiles, (i - self.ctx_tiles) % self.lat_tiles)


def _rope_tables(t):
    pos = jnp.arange(t)
    row = (pos // GRID_W).astype(F32)
    col = (pos % GRID_W).astype(F32)

    def cs(dim):
        q = dim // 4
        inv = ROPE_BASE ** (-jnp.arange(q, dtype=F32) / q)
        ar = row[:, None] * inv
        ac = col[:, None] * inv
        c = jnp.concatenate([jnp.cos(ar), jnp.cos(ar), jnp.cos(ac), jnp.cos(ac)], axis=-1)
        s = jnp.concatenate([-jnp.sin(ar), jnp.sin(ar), -jnp.sin(ac), jnp.sin(ac)], axis=-1)
        return c, s

    def with_identity(c, s):
        return (jnp.concatenate([c, jnp.ones((ROW_TILE, c.shape[1]), F32)], axis=0),
                jnp.concatenate([s, jnp.zeros((ROW_TILE, s.shape[1]), F32)], axis=0))

    c64, s64 = cs(RET_DK)
    rc, rs = with_identity(jnp.tile(c64, (1, RET_H)), jnp.tile(s64, (1, RET_H)))
    c32, s32 = cs(MLA_DR)
    pad = lambda a, v: jnp.concatenate([jnp.full((t, MLA_DN), v, F32), a, jnp.full((t, LANES - MLA_DN - MLA_DR), v, F32)], axis=-1)
    mc, ms = with_identity(pad(c32, 1.0), pad(s32, 0.0))
    return rc, rs, mc, ms


def _layer_params(l, norm_g, w_in, ret_decay, ret_norm, conv_w, conv_b, lru_gate_w, lru_gate_b, lru_lambda,
                  mla_q_norm, mla_kv_norm, mla_w_uq, mla_w_ukv, w_out, router_w, router_b,
                  moe_w_gu, moe_b_gu, moe_w_down, moe_b_down):
    p = {}
    p['g'] = [norm_g[l, i].reshape(1, D_MODEL) for i in range(4)]
    kr0 = 1920
    p['win'] = jnp.concatenate([w_in[l][:, :kr0], jnp.zeros((D_MODEL, MLA_DN), F32), w_in[l][:, kr0:],
                                jnp.zeros((D_MODEL, LANES - MLA_DN - MLA_DR), F32)], axis=1).astype(BF16)
    p['lg'] = jax.nn.log_sigmoid(ret_decay[l].astype(F32))
    p['retn'] = ret_norm[l].reshape(1, RET_W)
    p['qn'] = mla_q_norm[l].reshape(1, Q_RANK)
    p['kvn'] = mla_kv_norm[l].reshape(1, KV_RANK)
    wuq = mla_w_uq[l].reshape(Q_RANK, MLA_H, MLA_DN + MLA_DR)
    p['wuq'] = jnp.pad(wuq, ((0, 0), (0, 0), (0, LANES - MLA_DN - MLA_DR))).reshape(Q_RANK, MLA_H * LANES).astype(BF16)
    wukv = mla_w_ukv[l].reshape(KV_RANK, MLA_H, MLA_DN + MLA_DV)
    wk = jnp.pad(wukv[:, :, :MLA_DN], ((0, 0), (0, 0), (0, LANES - MLA_DN))).reshape(KV_RANK, MLA_H * LANES)
    place = jnp.zeros((LANES, LANES), F32).at[jnp.arange(MLA_DN, MLA_DN + MLA_DR), jnp.arange(MLA_DN, MLA_DN + MLA_DR)].set(1.0)
    p['wk_aug_t'] = jnp.concatenate([wk, jnp.tile(place, (1, MLA_H))], axis=0).T.astype(BF16)
    p['wv'] = wukv[:, :, MLA_DN:].reshape(KV_RANK, MLA_W).astype(BF16)
    p['wout'] = w_out[l].astype(BF16)
    p['rw'] = jnp.pad(router_w[l], ((0, 0), (0, LANES - N_EXPERTS))).astype(BF16)
    p['rb'] = jnp.pad(router_b[l], (0, LANES - N_EXPERTS)).reshape(1, LANES)
    p['cw'] = conv_w[l].reshape(4, 2, LANES).transpose(1, 0, 2)
    p['cb'] = conv_b[l].reshape(2, 1, LANES)
    gw = lru_gate_w[l]
    halves = []
    for hh in range(2):
        cols = []
        for d in range(2):
            for g in range(2):
                blk = jnp.zeros((LANES, LANES), F32)
                for j in range(2):
                    blk = blk.at[j * LRU_BW:(j + 1) * LRU_BW, j * LRU_BW:(j + 1) * LRU_BW].set(gw[d, g, 2 * hh + j])
                cols.append(blk)
        halves.append(jnp.concatenate(cols, axis=1))
    p['wg'] = jnp.stack(halves).astype(BF16)
    gb = lru_gate_b[l].reshape(2, 2, 2, LANES)
    p['bg'] = gb.transpose(2, 0, 1, 3).reshape(2, 1, 4 * LANES)
    p['c8'] = (8.0 * jax.nn.log_sigmoid(lru_lambda[l].astype(F32))).reshape(2, 2, LANES).transpose(1, 0, 2)
    p['wgu'] = moe_w_gu[l].astype(BF16)
    p['bgu'] = moe_b_gu[l]
    p['wd'] = moe_w_down[l].astype(BF16)
    p['bd'] = moe_b_down[l]
    return p


def _forward(x_prompt, x_sample, c, state_ret, state_lru, cache_mla_ckv, cache_mla_krope, c_ctx, ada_w, ada_b,
             *weights):
    nb_ctx, t_ctx, _ = x_prompt.shape
    nb_lat, t_lat, _ = x_sample.shape
    geo = _Geometry(nb_ctx, t_ctx, nb_lat, t_lat)
    n_c = geo.n_ctx
    x = jnp.concatenate([x_prompt.reshape(n_c, D_MODEL), x_sample.reshape(-1, D_MODEL)], axis=0)
    cond = jnp.concatenate([c_ctx[None, :], c, jnp.zeros((16 - 1 - nb_lat, D_MODEL), F32)], axis=0)
    mod_all = _modulation(cond, ada_w, ada_b)[:, :1 + nb_lat].reshape(DEPTH, 1 + nb_lat, 6, D_MODEL)
    tabs = _rope_tables(t_lat)
    krope_pad = jnp.pad(cache_mla_krope, ((0, 0), (0, 0), (0, 0), (MLA_DN, LANES - MLA_DN - MLA_DR)))
    ret_out, lru_out, ckv_out, kr_out = [], [], [], []
    for l in range(DEPTH):
        p = _layer_params(l, *weights)
        mod = mod_all[l]
        rq, rk, rv, rg, lx, ly, q, ckvn, kr = _in_projection(x, mod, p['g'][0], p['win'], tabs, p['qn'], p['kvn'],
                                                             p['wuq'], geo)
        zero_s = jnp.zeros((nb_ctx, 2, 2, LANES, LANES), F32)
        ofc, obc, s_ctx = _retention(rq, rk, rv, p['lg'], zero_s, nb_ctx, t_ctx, 0)
        ofl, obl, _ = _retention(rq, rk, rv, p['lg'], _state_to_pairs(state_ret[:, l]), nb_lat, t_lat, n_c)
        lru_c, h_ctx = _lru(lx, ly, jnp.zeros((nb_ctx, 2, LRU_W), F32), p['c8'], p['cw'], p['cb'], p['wg'], p['bg'],
                            nb_ctx, t_ctx, 0)
        lru_l, _ = _lru(lx, ly, state_lru[:, l], p['c8'], p['cw'], p['cb'], p['wg'], p['bg'], nb_lat, t_lat, n_c)
        mla_c = _mla(q, ckvn, kr, None, p['wk_aug_t'], p['wv'], nb_ctx, t_ctx, 0)
        mla_l = _mla(q, ckvn, kr, (cache_mla_ckv[:, l], krope_pad[:, l]), p['wk_aug_t'], p['wv'], nb_lat, t_lat, n_c)
        x1, h2, logits = _out_projection((ofc, obc, lru_c, mla_c), (ofl, obl, lru_l, mla_l), rg, x, mod, p['retn'],
                                         p['g'][1], p['g'][2], p['wout'], p['rw'], p['rb'], geo)
        gate, slot_tok, inv_slot, tile_expert, n_used = _route(logits[:, :N_EXPERTS])
        y_sorted = _moe(h2, tile_expert, n_used, slot_tok, p['wgu'], p['bgu'], p['wd'], p['bd'])
        x = _combine(y_sorted, inv_slot, gate, x1, mod, p['g'][3], geo)
        ret_out.append(_pairs_to_state(s_ctx))
        lru_out.append(h_ctx)
        ckv_out.append(ckvn[:n_c].reshape(nb_ctx, t_ctx, KV_RANK))
        kr_out.append(kr[:n_c, MLA_DN:MLA_DN + MLA_DR].reshape(nb_ctx, t_ctx, MLA_DR))
    y_prompt = x[:n_c].reshape(nb_ctx, t_ctx, D_MODEL)
    y_sample = x[n_c:].reshape(nb_lat, t_lat, D_MODEL)
    return (y_prompt, y_sample, jnp.stack(ret_out, axis=1), jnp.stack(lru_out, axis=1),
            jnp.stack(ckv_out, axis=1), jnp.stack(kr_out, axis=1))


def kernel(x_prompt, x_sample, c, state_ret, state_lru, cache_mla_ckv, cache_mla_krope, c_ctx, ada_w, ada_b, norm_g, w_in, ret_decay, ret_norm, conv_w, conv_b, lru_gate_w, lru_gate_b, lru_lambda, mla_q_norm, mla_kv_norm, mla_w_uq, mla_w_ukv, w_out, router_w, router_b, moe_w_gu, moe_b_gu, moe_w_down, moe_b_down):
    return _forward(x_prompt, x_sample, c, state_ret, state_lru, cache_mla_ckv, cache_mla_krope, c_ctx, ada_w, ada_b,
                    norm_g, w_in, ret_decay, ret_norm, conv_w, conv_b, lru_gate_w, lru_gate_b, lru_lambda,
                    mla_q_norm, mla_kv_norm, mla_w_uq, mla_w_ukv, w_out, router_w, router_b,
                    moe_w_gu, moe_b_gu, moe_w_down, moe_b_down)
```

```python
import functools
import math

import jax
import jax.numpy as jnp
from jax import lax
from jax.experimental import pallas as pl
from jax.experimental.pallas import tpu as pltpu

F32 = jnp.float32
BF16 = jnp.bfloat16

D_MODEL = 1024
DEPTH = 2
GRID_W = 64
RET_H = 4
RET_DK = 64
RET_W = 256
RET_CHUNK = 128
LRU_W = 256
LRU_BLOCKS = 4
LRU_BW = 64
MLA_H = 8
MLA_DN = 64
MLA_DR = 32
MLA_DV = 64
MLA_W = MLA_H * MLA_DV
MLA_VT_ROWS = 80
Q_RANK = 256
KV_RANK = 128
ROPE_BASE = 10000.0
N_EXPERTS = 32
TOP_K = 4
D_EXPERT = 1024
SWIGLU_ALPHA = 1.702
SWIGLU_LIMIT = 7.0
EPS = 1e-6

LANES = 128
ROW_TILE = 256
MOE_TILE = 256
LRU_ROWS = 256
ATT_TQ = 256
ATT_KC = 256
ATT_AHEAD = 4
KV_BUILD_ROWS = 512
PROJ_PAD = 2048
VMEM_LIMIT = 56 * 1024 * 1024


def _cparams(sem):
    return pltpu.CompilerParams(dimension_semantics=sem, vmem_limit_bytes=VMEM_LIMIT)


def _rms(x, g):
    return x * lax.rsqrt(jnp.mean(x * x, axis=-1, keepdims=True) + EPS) * g


def _rope(x, c, s, quarter):
    w = x.shape[-1]
    lane = lax.broadcasted_iota(jnp.int32, x.shape, 1)
    first = (lane % (2 * quarter)) < quarter
    partner = jnp.where(first, pltpu.roll(x, w - quarter, 1), pltpu.roll(x, quarter, 1))
    return x * c + partner * s


def _mod_kernel(cond_ref, w_ref, b_ref, o_ref):
    c = cond_ref[...]
    s = (c * jax.nn.sigmoid(c)).astype(BF16)
    o_ref[0] = jnp.dot(s, w_ref[0].astype(BF16), preferred_element_type=F32) + b_ref[0]


def _modulation(cond, ada_w, ada_b):
    nblk = 6
    return pl.pallas_call(
        _mod_kernel,
        grid=(DEPTH, nblk),
        in_specs=[
            pl.BlockSpec((16, D_MODEL), lambda l, j: (0, 0)),
            pl.BlockSpec((1, D_MODEL, D_MODEL), lambda l, j: (l, 0, j)),
            pl.BlockSpec((1, 1, D_MODEL), lambda l, j: (l, 0, j)),
        ],
        out_specs=pl.BlockSpec((1, 16, D_MODEL), lambda l, j: (l, 0, j)),
        out_shape=jax.ShapeDtypeStruct((DEPTH, 16, 6 * D_MODEL), F32),
        compiler_params=_cparams(("arbitrary", "arbitrary")),
        name="adaln_mod",
    )(cond, ada_w, ada_b.reshape(DEPTH, 1, 6 * D_MODEL))


def _inproj_kernel(x_ref, mod_ref, g0_ref, win_ref, rc_ref, rs_ref, mc_ref, ms_ref, qn_ref, kvn_ref, wuq_ref,
                   rq_ref, rk_ref, rv_ref, rg_ref, lx_ref, ly_ref, q_ref, ckv_ref, kr_ref):
    x = x_ref[...]
    mod = mod_ref[0]
    h = _rms(x, g0_ref[...]) * (1.0 + mod[1:2]) + mod[0:1]
    z = jnp.dot(h.astype(BF16), win_ref[...], preferred_element_type=F32)
    rc = rc_ref[...]
    rs = rs_ref[...]
    rq_ref[...] = _rope(z[:, 0:256], rc, rs, 16)
    rk_ref[...] = _rope(z[:, 256:512] * (RET_DK ** -0.5), rc, rs, 16)
    rv_ref[...] = z[:, 512:768]
    rg_ref[...] = z[:, 768:1024]
    lx_ref[...] = z[:, 1024:1280]
    ly_ref[...] = z[:, 1280:1536]
    cqn = _rms(z[:, 1536:1792], qn_ref[...])
    q = jnp.dot(cqn.astype(BF16), wuq_ref[...], preferred_element_type=F32)
    mc = mc_ref[...]
    ms = ms_ref[...]
    for h_i in range(MLA_H):
        sl = slice(h_i * LANES, (h_i + 1) * LANES)
        q_ref[:, sl] = _rope(q[:, sl], mc, ms, 8).astype(BF16)
    ckv_ref[...] = _rms(z[:, 1792:1920], kvn_ref[...])
    kr_ref[...] = _rope(z[:, 1920:2048], mc, ms, 8)


def _in_projection(x, mod, g0, win_p, tabs, qn, kvn, wuq_p, geo):
    n = x.shape[0]
    nt = n // ROW_TILE
    rc, rs, mc, ms = tabs
    row = lambda w: pl.BlockSpec((ROW_TILE, w), lambda i: (i, 0))
    full = lambda a: pl.BlockSpec(a.shape, lambda i: (0,) * a.ndim)
    tab = lambda w: pl.BlockSpec((ROW_TILE, w), lambda i: (geo.rope_block(i), 0))
    outs = [(256, F32)] * 6 + [(MLA_H * LANES, BF16), (LANES, F32), (LANES, F32)]
    return pl.pallas_call(
        _inproj_kernel,
        grid=(nt,),
        in_specs=[row(D_MODEL), pl.BlockSpec((1, 6, D_MODEL), lambda i: (geo.mod_group(i), 0, 0)),
                  full(g0), full(win_p), tab(256), tab(256), tab(LANES), tab(LANES), full(qn), full(kvn), full(wuq_p)],
        out_specs=[row(w) for w, _ in outs],
        out_shape=[jax.ShapeDtypeStruct((n, w), dt) for w, dt in outs],
        compiler_params=_cparams(("arbitrary",)),
        name="in_projection",
    )(x, mod, g0, win_p, rc, rs, mc, ms, qn, kvn, wuq_p)


def _ret_kernel(lg_ref, qf_ref, kf_ref, vf_ref, qb_ref, kb_ref, vb_ref, s0_ref,
                of_ref, ob_ref, sfin_ref, s_scr, intra_scr, cross_scr, into_scr, carry_scr):
    j = pl.program_id(1)
    c = RET_CHUNK
    row = lax.broadcasted_iota(jnp.int32, (c, c), 0)
    lane = lax.broadcasted_iota(jnp.int32, (c, c), 1)
    rowf = row.astype(F32)
    lanef = lane.astype(F32)

    @pl.when(j == 0)
    def _init():
        s_scr[...] = s0_ref[0]
        for d in range(2):
            for h in range(RET_H):
                lg = lg_ref[d, h]
                if d == 0:
                    keep = row >= lane
                    dist = rowf - lanef
                else:
                    keep = lane >= row
                    dist = lanef - rowf
                intra_scr[d, h] = jnp.where(keep, jnp.exp(jnp.where(keep, dist, 0.0) * lg), 0.0)
            for p in range(2):
                lgl = jnp.where(lane < RET_DK, lg_ref[d, 2 * p], lg_ref[d, 2 * p + 1])
                if d == 0:
                    cross_scr[d, p] = jnp.exp((rowf + 1.0) * lgl)
                    into_scr[d, p] = jnp.exp((c - 1.0 - rowf) * lgl)
                else:
                    cross_scr[d, p] = jnp.exp((c - rowf) * lgl)
                    into_scr[d, p] = jnp.exp(rowf * lgl)
                carry_scr[d, p] = jnp.exp(float(c) * lgl)

    same_head = (row < RET_DK) == (lane < RET_DK)

    def one_dir(d, q_ref, k_ref, v_ref, o_ref):
        for p in range(2):
            sl = slice(p * LANES, (p + 1) * LANES)
            q2 = q_ref[:, sl]
            k2 = k_ref[:, sl]
            v2 = v_ref[:, sl]
            q2b = q2.astype(BF16)
            o = jnp.zeros((c, LANES), F32)
            for e in range(2):
                sel = (lane >= RET_DK) if e else (lane < RET_DK)
                ke = jnp.where(sel, k2, 0.0).astype(BF16)
                ve = jnp.where(sel, v2, 0.0).astype(BF16)
                s = lax.dot_general(q2b, ke, (((1,), (1,)), ((), ())), preferred_element_type=F32)
                a = (s * intra_scr[d, 2 * p + e]).astype(BF16)
                o = o + jnp.dot(a, ve, preferred_element_type=F32)
            st = s_scr[d, p]
            o = o + jnp.dot(q2b, st.astype(BF16), preferred_element_type=F32) * cross_scr[d, p]
            kw = (k2 * into_scr[d, p]).astype(BF16)
            upd = lax.dot_general(kw, v2.astype(BF16), (((0,), (0,)), ((), ())), preferred_element_type=F32)
            s_scr[d, p] = st * carry_scr[d, p] + jnp.where(same_head, upd, 0.0)
            o_ref[:, sl] = o

    one_dir(0, qf_ref, kf_ref, vf_ref, of_ref)
    one_dir(1, qb_ref, kb_ref, vb_ref, ob_ref)

    @pl.when(j == pl.num_programs(1) - 1)
    def _fin():
        sfin_ref[0] = s_scr[...]


def _retention(rq, rk, rv, lg, s0, nb, t, row0):
    n = nb * t
    nc = t // RET_CHUNK
    base = row0 // RET_CHUNK
    fwd = lambda off: pl.BlockSpec((RET_CHUNK, RET_W), lambda b, j: (off + b * nc + j, 0))
    bwd = lambda off: pl.BlockSpec((RET_CHUNK, RET_W), lambda b, j: (off + b * nc + nc - 1 - j, 0))
    st = pl.BlockSpec((1, 2, 2, LANES, LANES), lambda b, j: (b, 0, 0, 0, 0))
    return pl.pallas_call(
        _ret_kernel,
        grid=(nb, nc),
        in_specs=[pl.BlockSpec(memory_space=pltpu.SMEM), fwd(base), fwd(base), fwd(base), bwd(base), bwd(base),
                  bwd(base), st],
        out_specs=[fwd(0), bwd(0), st],
        out_shape=[jax.ShapeDtypeStruct((n, RET_W), F32), jax.ShapeDtypeStruct((n, RET_W), F32),
                   jax.ShapeDtypeStruct((nb, 2, 2, LANES, LANES), F32)],
        scratch_shapes=[pltpu.VMEM((2, 2, LANES, LANES), F32), pltpu.VMEM((2, RET_H, RET_CHUNK, RET_CHUNK), F32),
                        pltpu.VMEM((2, 2, RET_CHUNK, LANES), F32), pltpu.VMEM((2, 2, RET_CHUNK, LANES), F32),
                        pltpu.VMEM((2, 2, RET_CHUNK, LANES), F32)],
        compiler_params=_cparams(("arbitrary", "arbitrary")),
        name="retention",
    )(lg, rq, rk, rv, rq, rk, rv, s0)


def _state_to_pairs(s):
    b = s.shape[0]
    s = s.reshape(b, 2, 2, 2, RET_DK, RET_DK)
    z = jnp.zeros_like(s[:, :, :, 0])
    top = jnp.concatenate([s[:, :, :, 0], z], axis=-1)
    bot = jnp.concatenate([z, s[:, :, :, 1]], axis=-1)
    return jnp.concatenate([top, bot], axis=-2)


def _pairs_to_state(s):
    b = s.shape[0]
    a = s[..., :RET_DK, :RET_DK]
    c = s[..., RET_DK:, RET_DK:]
    return jnp.stack([a, c], axis=3).reshape(b, 2, RET_H, RET_DK, RET_DK)


def _scan_rows(a, b, reverse):
    r = a.shape[0]
    rows = lax.broadcasted_iota(jnp.int32, a.shape, 0)
    s = 1
    while s < r:
        if reverse:
            a_s = pltpu.roll(a, r - s, 0)
            b_s = pltpu.roll(b, r - s, 0)
            m = rows < r - s
        else:
            a_s = pltpu.roll(a, s, 0)
            b_s = pltpu.roll(b, s, 0)
            m = rows >= s
        b = jnp.where(m, a * b_s + b, b)
        a = jnp.where(m, a * a_s, a)
        s *= 2
    return a, b


def _lru_kernel(c8_ref, cw_ref, cb_ref, wg_ref, bg_ref, lx_ref, ly_ref, h0_ref, o_ref, hfin_ref, xc_scr, hf_scr,
                *, t, r):
    nch = t // r
    cw = cw_ref[0]
    cb = cb_ref[0]
    wg = wg_ref[0]
    bg = bg_ref[0]
    c8 = c8_ref[0]

    def conv_body(c, carry):
        r0 = pl.multiple_of(c * r, r)
        cur = lx_ref[pl.ds(r0, r), :]
        prev = lx_ref[pl.ds(pl.multiple_of(jnp.maximum(r0 - 8, 0), 8), 8), :]
        nxt = lx_ref[pl.ds(pl.multiple_of(jnp.minimum(r0 + r, t - 8), 8), 8), :]
        prev = jnp.where(c > 0, prev, 0.0)
        nxt = jnp.where(c < nch - 1, nxt, 0.0)
        ext = jnp.concatenate([prev, cur, nxt], axis=0)
        acc = jnp.broadcast_to(cb, (r, LANES))
        for tap in range(4):
            sh = (2 - tap) % (r + 16)
            xs = ext if sh == 0 else pltpu.roll(ext, sh, 0)
            acc = acc + xs[8:8 + r] * cw[tap:tap + 1]
        xc_scr[pl.ds(r0, r), :] = acc
        return carry

    lax.fori_loop(0, nch, conv_body, 0)

    def gates(xc, d):
        g = jnp.dot(xc.astype(BF16), wg[:, d * 256:(d + 1) * 256], preferred_element_type=F32) + bg[:, d * 256:(d + 1) * 256]
        rg = jax.nn.sigmoid(g[:, :LANES])
        ig = jax.nn.sigmoid(g[:, LANES:])
        log_a = c8[d:d + 1] * rg
        a = jnp.exp(log_a)
        b = jnp.sqrt(jnp.tanh(-log_a) * (a * a + 1.0)) * (ig * xc)
        return a, b

    def fwd_body(c, h):
        r0 = pl.multiple_of(c * r, r)
        a, b = gates(xc_scr[pl.ds(r0, r), :], 0)
        a, b = _scan_rows(a, b, False)
        hc = a * h + b
        hf_scr[pl.ds(r0, r), :] = hc
        return hc[r - 1:r]

    h_f = lax.fori_loop(0, nch, fwd_body, h0_ref[0, 0:1, :])

    def bwd_body(i, h):
        r0 = pl.multiple_of((nch - 1 - i) * r, r)
        a, b = gates(xc_scr[pl.ds(r0, r), :], 1)
        a, b = _scan_rows(a, b, True)
        hc = a * h + b
        o_ref[pl.ds(r0, r), :] = ((hf_scr[pl.ds(r0, r), :] + hc) * jax.nn.gelu(ly_ref[pl.ds(r0, r), :])).astype(BF16)
        return hc[0:1]

    h_b = lax.fori_loop(0, nch, bwd_body, h0_ref[0, 1:2, :])
    hfin_ref[0, 0:1, :] = h_f
    hfin_ref[0, 1:2, :] = h_b


def _lru(lx, ly, h0, c8, cw, cb, wg, bg, nb, t, row0):
    base = row0 // t
    r = min(LRU_ROWS, t)
    seq = lambda off: pl.BlockSpec((t, LANES), lambda b, hh: (off + b, hh))
    par = lambda a: pl.BlockSpec((1,) + a.shape[1:], lambda b, hh: (hh,) + (0,) * (a.ndim - 1))
    st = pl.BlockSpec((1, 2, LANES), lambda b, hh: (b, 0, hh))
    return pl.pallas_call(
        functools.partial(_lru_kernel, t=t, r=r),
        grid=(nb, 2),
        in_specs=[par(c8), par(cw), par(cb), par(wg), par(bg), seq(base), seq(base), st],
        out_specs=[seq(0), st],
        out_shape=[jax.ShapeDtypeStruct((nb * t, LRU_W), BF16), jax.ShapeDtypeStruct((nb, 2, LRU_W), F32)],
        scratch_shapes=[pltpu.VMEM((t, LANES), F32), pltpu.VMEM((t, LANES), F32)],
        compiler_params=_cparams(("arbitrary", "arbitrary")),
        name="rg_lru",
    )(c8, cw, cb, wg, bg, lx, ly, h0)


def _mla_kernel(*refs, n_ctx, t, tq):
    if n_ctx:
        q_ref, ckv_ref, kr_ref, cckv_ref, ckr_ref, wk_ref, wv_ref = refs[:7]
    else:
        q_ref, ckv_ref, kr_ref, wk_ref, wv_ref = refs[:5]
    o_ref, k_scr, vt_scr, qt_scr, ot_scr = refs[-5:]
    qi = pl.program_id(1)
    cexp = (MLA_DN + MLA_DR) ** -0.5 * math.log2(math.e)
    ones_row = (lax.broadcasted_iota(jnp.int32, (MLA_H * MLA_VT_ROWS, 1), 0) % MLA_VT_ROWS == MLA_DV).astype(F32)

    def put(ckv, kr, c0):
        nrow = ckv.shape[0]
        ckv_b = ckv.astype(BF16)
        kn = jnp.dot(ckv_b, wk_ref[...], preferred_element_type=F32)
        vt = lax.dot_general(wv_ref[...], ckv_b, (((1,), (1,)), ((), ())), preferred_element_type=F32) + ones_row
        for h in range(MLA_H):
            k_scr[h, c0:c0 + nrow, :] = ((kn[:, h * LANES:(h + 1) * LANES] + kr) * cexp).astype(BF16)
            vt_scr[h, :, c0:c0 + nrow] = vt[h * MLA_VT_ROWS:(h + 1) * MLA_VT_ROWS].astype(BF16)

    @pl.when(qi == 0)
    def _build():
        if n_ctx:
            put(cckv_ref[0], ckr_ref[0], 0)
        step = min(KV_BUILD_ROWS, t)
        for c in range(t // step):
            put(ckv_ref[c * step:(c + 1) * step, :], kr_ref[c * step:(c + 1) * step, :], n_ctx + c * step)

    qt_scr[...] = q_ref[...].astype(F32).T.astype(BF16)
    s_len = n_ctx + t

    def head_body(h, carry):
        qt = qt_scr[pl.ds(pl.multiple_of(h * LANES, LANES), LANES), :]
        m = jnp.full((1, tq), -1e30, F32)
        o = jnp.zeros((MLA_VT_ROWS, tq), F32)
        nchunk = s_len // ATT_KC
        score = lambda c: jnp.dot(k_scr[h, c * ATT_KC:(c + 1) * ATT_KC, :], qt, preferred_element_type=F32)
        pending = [score(c) for c in range(min(ATT_AHEAD, nchunk))]
        held = None
        for c in range(nchunk + 1):
            if c + ATT_AHEAD < nchunk:
                pending.append(score(c + ATT_AHEAD))
            if c < nchunk:
                s = pending[c]
                m_new = jnp.maximum(m, jnp.max(s, axis=0, keepdims=True))
                p = jnp.exp2(s - m_new).astype(BF16)
                alpha = jnp.exp2(m - m_new)
                m = m_new
            if held is not None:
                p_h, alpha_h, ks = held
                o = o * alpha_h + jnp.dot(vt_scr[h, :, ks], p_h, preferred_element_type=F32)
            if c < nchunk:
                held = (p, alpha, slice(c * ATT_KC, (c + 1) * ATT_KC))
        ot_scr[h] = o[:MLA_DV] / o[MLA_DV:MLA_DV + 1]
        return carry

    lax.fori_loop(0, MLA_H, head_body, 0)
    for pp in range(MLA_H // 2):
        pair = jnp.concatenate([ot_scr[2 * pp], ot_scr[2 * pp + 1]], axis=0)
        o_ref[:, pp * LANES:(pp + 1) * LANES] = pair.T.astype(BF16)


def _mla(q, ckvn, kr, cache, wk, wv_t, nb, t, row0):
    tq = min(ATT_TQ, t)
    nq = t // tq
    n_ctx = 0 if cache is None else cache[0].shape[1]
    s_len = n_ctx + t
    qspec = pl.BlockSpec((tq, MLA_H * LANES), lambda b, i: (row0 // tq + b * nq + i, 0))
    seq = pl.BlockSpec((t, LANES), lambda b, i: (row0 // t + b, 0))
    full = lambda a: pl.BlockSpec(a.shape, lambda b, i: (0,) * a.ndim)
    ins = [q, ckvn, kr]
    specs = [qspec, seq, seq]
    if n_ctx:
        cspec = pl.BlockSpec((1, n_ctx, LANES), lambda b, i: (b, 0, 0))
        ins += [cache[0], cache[1]]
        specs += [cspec, cspec]
    ins += [wk, wv_t]
    specs += [full(wk), full(wv_t)]
    return pl.pallas_call(
        functools.partial(_mla_kernel, n_ctx=n_ctx, t=t, tq=tq),
        grid=(nb, nq),
        in_specs=specs,
        out_specs=pl.BlockSpec((tq, MLA_W), lambda b, i: (b * nq + i, 0)),
        out_shape=jax.ShapeDtypeStruct((nb * t, MLA_W), BF16),
        scratch_shapes=[pltpu.VMEM((MLA_H, s_len, LANES), BF16), pltpu.VMEM((MLA_H, MLA_VT_ROWS, s_len), BF16),
                        pltpu.VMEM((MLA_H * LANES, tq), BF16), pltpu.VMEM((MLA_H, MLA_DV, tq), F32)],
        compiler_params=_cparams(("arbitrary", "arbitrary")),
        name="mla_attention",
    )(*ins)


def _outproj_kernel(ofc_ref, obc_ref, lruc_ref, mlac_ref, ofl_ref, obl_ref, lrul_ref, mlal_ref, rg_ref, x_ref, mod_ref,
                    retn_ref, g1_ref, g2_ref, wout_ref, rw_ref, rb_ref, x1_ref, h2_ref, route_ref, cnt_ref, cnt_scr,
                    *, ctx_tiles):
    is_ctx = pl.program_id(0) < ctx_tiles
    pick = lambda c_ref, l_ref: jnp.where(is_ctx, c_ref[...], l_ref[...])
    o = pick(ofc_ref, ofl_ref) + pick(obc_ref, obl_ref)
    hid = lax.broadcasted_iota(jnp.int32, o.shape, 1) // RET_DK

    def head_sum(v):
        tot = jnp.zeros_like(v)
        for hh in range(RET_H):
            msk = hid == hh
            tot = jnp.where(msk, jnp.sum(jnp.where(msk, v, 0.0), axis=1, keepdims=True), tot)
        return tot

    mu = head_sum(o) * (1.0 / RET_DK)
    dl = o - mu
    var = head_sum(dl * dl) * (1.0 / RET_DK)
    rg = rg_ref[...]
    ret = dl * lax.rsqrt(var + EPS) * retn_ref[...] * (rg * jax.nn.sigmoid(rg))
    mix = jnp.concatenate([ret.astype(BF16), pick(lruc_ref, lrul_ref), pick(mlac_ref, mlal_ref)], axis=1)
    mo = jnp.dot(mix, wout_ref[...], preferred_element_type=F32)
    mod = mod_ref[0]
    x1 = x_ref[...] + mod[2:3] * _rms(mo, g1_ref[...])
    x1_ref[...] = x1
    h2 = _rms(x1, g2_ref[...]) * (1.0 + mod[4:5]) + mod[3:4]
    h2_ref[...] = h2

    tm = h2.shape[0]
    lane = lax.broadcasted_iota(jnp.int32, (tm, LANES), 1)
    lanef = lane.astype(F32)
    lg = jnp.dot(h2.astype(BF16), rw_ref[...], preferred_element_type=F32) + rb_ref[...]
    lg = jnp.where(lane < N_EXPERTS, lg, -jnp.inf)
    tops, idxs, hots = [], [], []
    for _ in range(TOP_K):
        m = jnp.max(lg, axis=1, keepdims=True)
        idx = jnp.min(jnp.where(lg == m, lanef, float(LANES)), axis=1, keepdims=True)
        hot = lanef == idx
        lg = jnp.where(hot, -jnp.inf, lg)
        tops.append(m)
        idxs.append(idx)
        hots.append(hot)
    exps = [jnp.exp(t - tops[0]) for t in tops]
    den = exps[0] + exps[1] + exps[2] + exps[3]
    member = jnp.zeros((tm, LANES), F32)
    for hot in hots:
        member = member + hot.astype(F32)

    @pl.when(pl.program_id(0) == 0)
    def _zero_counts():
        cnt_scr[...] = jnp.zeros_like(cnt_scr)

    earlier = (lax.broadcasted_iota(jnp.int32, (tm, tm), 0) > lax.broadcasted_iota(jnp.int32, (tm, tm), 1))
    before = jnp.dot(earlier.astype(BF16), member.astype(BF16), preferred_element_type=F32) + cnt_scr[0:1, :]
    route = jnp.zeros((tm, LANES), F32)
    for k in range(TOP_K):
        rank = jnp.sum(jnp.where(hots[k], before, 0.0), axis=1, keepdims=True)
        route = jnp.where(lane == k, idxs[k], route)
        route = jnp.where(lane == TOP_K + k, exps[k] / den, route)
        route = jnp.where(lane == 2 * TOP_K + k, rank, route)
    route_ref[...] = route
    cnt_scr[...] = cnt_scr[...] + jnp.sum(member, axis=0, keepdims=True)
    cnt_ref[...] = cnt_scr[...]


def _out_projection(ctx_mix, lat_mix, rg, x, mod, retn, g1, g2, wout, rw, rb, geo):
    n = x.shape[0]
    ct = geo.ctx_tiles
    row = lambda w: pl.BlockSpec((ROW_TILE, w), lambda i: (i, 0))
    crow = lambda w: pl.BlockSpec((ROW_TILE, w), lambda i: (jnp.minimum(i, ct - 1), 0))
    lrow = lambda w: pl.BlockSpec((ROW_TILE, w), lambda i: (jnp.maximum(i - ct, 0), 0))
    full = lambda a: pl.BlockSpec(a.shape, lambda i: (0,) * a.ndim)
    widths = (256, 256, 256, MLA_W)
    return pl.pallas_call(
        functools.partial(_outproj_kernel, ctx_tiles=ct),
        grid=(n // ROW_TILE,),
        in_specs=[crow(w) for w in widths] + [lrow(w) for w in widths] + [
                  row(256), row(D_MODEL),
                  pl.BlockSpec((1, 6, D_MODEL), lambda i: (geo.mod_group(i), 0, 0)),
                  full(retn), full(g1), full(g2), full(wout), full(rw), full(rb)],
        out_specs=[row(D_MODEL), row(D_MODEL), row(LANES), pl.BlockSpec((8, LANES), lambda i: (0, 0))],
        out_shape=[jax.ShapeDtypeStruct((n, D_MODEL), F32), jax.ShapeDtypeStruct((n, D_MODEL), F32),
                   jax.ShapeDtypeStruct((n, LANES), F32), jax.ShapeDtypeStruct((8, LANES), F32)],
        scratch_shapes=[pltpu.VMEM((8, LANES), F32)],
        compiler_params=_cparams(("arbitrary",)),
        name="out_projection",
    )(*ctx_mix, *lat_mix, rg, x, mod, retn, g1, g2, wout, rw, rb)


def _row_copy(src_hbm, src_row, dst, dst_row, sem):
    return pltpu.make_async_copy(src_hbm.at[pl.ds(src_row, 1), :], dst.at[pl.ds(dst_row, 1), :], sem)


def _dispatch_kernel(pe_ref, slot_ref, h_ref, xs_hbm, zero_scr, sem):
    i = pl.program_id(0)
    n_slots = xs_hbm.shape[0]

    @pl.when(i == 0)
    def _zero_pads():
        zero_scr[...] = jnp.zeros_like(zero_scr)

        def fill(e):
            end = pe_ref[e]
            start = 0 if e == 0 else pe_ref[e - 1]
            dst = xs_hbm.at[pl.ds(pl.multiple_of(jnp.maximum(end - MOE_TILE, 0), MOE_TILE), MOE_TILE), :]
            return end > start, pltpu.make_async_copy(zero_scr, dst, sem)

        def fill_tail(j):
            row = pe_ref[N_EXPERTS - 1] + j * MOE_TILE
            dst = xs_hbm.at[pl.ds(pl.multiple_of(jnp.minimum(row, n_slots - MOE_TILE), MOE_TILE), MOE_TILE), :]
            return row < n_slots, pltpu.make_async_copy(zero_scr, dst, sem)

        for e in range(N_EXPERTS):
            for todo, cp in (fill(e), fill_tail(e)):
                pl.when(todo)(cp.start)
        for e in range(N_EXPERTS):
            for todo, cp in (fill(e), fill_tail(e)):
                pl.when(todo)(cp.wait)

    def body(r, carry):
        for k in range(TOP_K):
            dst = xs_hbm.at[pl.ds(slot_ref[0, 0, k * ROW_TILE + r], 1), :]
            pltpu.make_async_copy(h_ref.at[pl.ds(r, 1), :], dst, sem).start(priority=k % 2)
        return carry

    lax.fori_loop(0, ROW_TILE, body, 0, unroll=4)
    for k in range(TOP_K):
        pltpu.make_async_copy(h_ref, xs_hbm.at[pl.ds(0, ROW_TILE), :], sem).wait()


def _dispatch(h2, slots3, pad_end, n_slots):
    n = h2.shape[0]
    grid_spec = pltpu.PrefetchScalarGridSpec(
        num_scalar_prefetch=1,
        grid=(n // ROW_TILE,),
        in_specs=[
            pl.BlockSpec((1, 1, TOP_K * ROW_TILE), lambda i, pe: (i, 0, 0), memory_space=pltpu.SMEM),
            pl.BlockSpec((ROW_TILE, D_MODEL), lambda i, pe: (i, 0)),
        ],
        out_specs=pl.BlockSpec(memory_space=pl.ANY),
        scratch_shapes=[pltpu.VMEM((MOE_TILE, D_MODEL), F32), pltpu.SemaphoreType.DMA(())],
    )
    return pl.pallas_call(
        _dispatch_kernel,
        grid_spec=grid_spec,
        out_shape=jax.ShapeDtypeStruct((n_slots, D_MODEL), F32),
        compiler_params=_cparams(("arbitrary",)),
        name="moe_dispatch",
    )(pad_end, slots3, h2)


def _moe_kernel(te_ref, nu_ref, x_ref, wgu_ref, bgu_ref, wd_ref, bd_ref, y_ref):
    del te_ref
    t = pl.program_id(0)
    n_used = nu_ref[0]

    @pl.when(t < n_used)
    def _compute():
        x = x_ref[...].astype(BF16)
        gu = jnp.dot(x, wgu_ref[0], preferred_element_type=F32) + bgu_ref[0]
        g = jnp.minimum(gu[:, :D_EXPERT], SWIGLU_LIMIT)
        u = jnp.clip(gu[:, D_EXPERT:], -SWIGLU_LIMIT, SWIGLU_LIMIT)
        act = g * jax.nn.sigmoid(SWIGLU_ALPHA * g) * (u + 1.0)
        y_ref[...] = jnp.dot(act.astype(BF16), wd_ref[0], preferred_element_type=F32) + bd_ref[0]

    @pl.when(t >= n_used)
    def _idle():
        y_ref[...] = jnp.zeros_like(y_ref)


def _moe(xs, tile_expert, n_used, wgu, bgu, wd, bd):
    n_tiles = tile_expert.shape[0]
    grid_spec = pltpu.PrefetchScalarGridSpec(
        num_scalar_prefetch=2,
        grid=(n_tiles,),
        in_specs=[
            pl.BlockSpec((MOE_TILE, D_MODEL), lambda t, te, nu: (jnp.minimum(t, jnp.maximum(nu[0] - 1, 0)), 0)),
            pl.BlockSpec((1, D_MODEL, 2 * D_EXPERT), lambda t, te, nu: (te[t], 0, 0)),
            pl.BlockSpec((1, 1, 2 * D_EXPERT), lambda t, te, nu: (te[t], 0, 0)),
            pl.BlockSpec((1, D_EXPERT, D_MODEL), lambda t, te, nu: (te[t], 0, 0)),
            pl.BlockSpec((1, 1, D_MODEL), lambda t, te, nu: (te[t], 0, 0)),
        ],
        out_specs=pl.BlockSpec((MOE_TILE, D_MODEL), lambda t, te, nu: (t, 0)),
    )
    return pl.pallas_call(
        _moe_kernel,
        grid_spec=grid_spec,
        out_shape=jax.ShapeDtypeStruct((n_tiles * MOE_TILE, D_MODEL), F32),
        compiler_params=_cparams(("arbitrary",)),
        name="moe_experts",
    )(tile_expert, n_used, xs, wgu, bgu.reshape(N_EXPERTS, 1, -1), wd, bd.reshape(N_EXPERTS, 1, -1))


def _combine_kernel(inv_ref, y_hbm, gate_ref, x1_ref, mod_ref, g3_ref, o_ref, buf, sem):
    for k in range(TOP_K):
        def body(r, carry, k=k):
            _row_copy(y_hbm, inv_ref[0, 0, k * ROW_TILE + r], buf.at[k], r, sem).start(priority=k % 2)
            return carry
        lax.fori_loop(0, ROW_TILE, body, 0, unroll=8)
    for k in range(TOP_K):
        pltpu.make_async_copy(y_hbm.at[pl.ds(0, ROW_TILE), :], buf.at[k], sem).wait()
    gate = gate_ref[...]
    ff = gate[:, 0:1] * buf[0]
    for k in range(1, TOP_K):
        ff = ff + gate[:, k:k + 1] * buf[k]
    o_ref[...] = x1_ref[...] + mod_ref[0][5:6] * _rms(ff, g3_ref[...])


def _combine(y_sorted, inv3, gate, x1, mod, g3, geo):
    n = x1.shape[0]
    nt = n // ROW_TILE
    return pl.pallas_call(
        _combine_kernel,
        grid=(nt,),
        in_specs=[
            pl.BlockSpec((1, 1, TOP_K * ROW_TILE), lambda i: (i, 0, 0), memory_space=pltpu.SMEM),
            pl.BlockSpec(memory_space=pl.ANY),
            pl.BlockSpec((ROW_TILE, TOP_K), lambda i: (i, 0)),
            pl.BlockSpec((ROW_TILE, D_MODEL), lambda i: (i, 0)),
            pl.BlockSpec((1, 6, D_MODEL), lambda i: (geo.mod_group(i), 0, 0)),
            pl.BlockSpec((1, D_MODEL), lambda i: (0, 0)),
        ],
        out_specs=pl.BlockSpec((ROW_TILE, D_MODEL), lambda i: (i, 0)),
        out_shape=jax.ShapeDtypeStruct((n, D_MODEL), F32),
        scratch_shapes=[pltpu.VMEM((TOP_K, ROW_TILE, D_MODEL), F32), pltpu.SemaphoreType.DMA(())],
        compiler_params=_cparams(("arbitrary",)),
        name="moe_combine",
    )(inv3, y_sorted, gate, x1, mod, g3)


def _plan(route, counts):
    n = route.shape[0]
    nt = n // ROW_TILE
    expert = route[:, 0:TOP_K].astype(jnp.int32)
    gate = route[:, TOP_K:2 * TOP_K]
    rank = route[:, 2 * TOP_K:3 * TOP_K].astype(jnp.int32)
    cnt = counts[0, :N_EXPERTS].astype(jnp.int32)
    padded = (cnt + MOE_TILE - 1) // MOE_TILE * MOE_TILE
    pad_end = jnp.cumsum(padded).astype(jnp.int32)
    pad_start = pad_end - padded
    start_of = jnp.sum(jnp.where(expert[:, :, None] == jnp.arange(N_EXPERTS), pad_start, 0), axis=-1)
    slots = start_of + rank
    slots3 = slots.reshape(nt, ROW_TILE, TOP_K).transpose(0, 2, 1).reshape(nt, 1, TOP_K * ROW_TILE)
    n_tiles = n * TOP_K // MOE_TILE + N_EXPERTS
    tile_start = jnp.arange(n_tiles, dtype=jnp.int32) * MOE_TILE
    tile_expert = jnp.minimum(jnp.sum((tile_start[:, None] >= pad_end[None, :]).astype(jnp.int32), axis=1),
                              N_EXPERTS - 1)
    n_used = (pad_end[-1] // MOE_TILE).reshape(1)
    return gate, slots3, pad_end, tile_expert, n_used, n_tiles * MOE_TILE


class _Geometry:
    def __init__(self, nb_ctx, t_ctx, nb_lat, t_lat):
        self.nb_ctx, self.t_ctx, self.nb_lat, self.t_lat = nb_ctx, t_ctx, nb_lat, t_lat
        self.n_ctx = nb_ctx * t_ctx
        self.n = self.n_ctx + nb_lat * t_lat
        self.ctx_tiles = self.n_ctx // ROW_TILE
        self.lat_tiles = t_lat // ROW_TILE

    def mod_group(self, i):
        return jnp.where(i < self.ctx_tiles, 0, 1 + (i - self.ctx_tiles) // self.lat_tiles)

    def rope_block(self, i):
        return jnp.where(i < self.ctx_tiles, self.lat_tiles, (i - self.ctx_tiles) % self.lat_tiles)


def _rope_tables(t):
    pos = jnp.arange(t)
    row = (pos // GRID_W).astype(F32)
    col = (pos % GRID_W).astype(F32)

    def cs(dim):
        q = dim // 4
        inv = ROPE_BASE ** (-jnp.arange(q, dtype=F32) / q)
        ar = row[:, None] * inv
        ac = col[:, None] * inv
        c = jnp.concatenate([jnp.cos(ar), jnp.cos(ar), jnp.cos(ac), jnp.cos(ac)], axis=-1)
        s = jnp.concatenate([-jnp.sin(ar), jnp.sin(ar), -jnp.sin(ac), jnp.sin(ac)], axis=-1)
        return c, s

    def with_identity(c, s):
        return (jnp.concatenate([c, jnp.ones((ROW_TILE, c.shape[1]), F32)], axis=0),
                jnp.concatenate([s, jnp.zeros((ROW_TILE, s.shape[1]), F32)], axis=0))

    c64, s64 = cs(RET_DK)
    rc, rs = with_identity(jnp.tile(c64, (1, RET_H)), jnp.tile(s64, (1, RET_H)))
    c32, s32 = cs(MLA_DR)
    pad = lambda a, v: jnp.concatenate([jnp.full((t, MLA_DN), v, F32), a, jnp.full((t, LANES - MLA_DN - MLA_DR), v, F32)], axis=-1)
    mc, ms = with_identity(pad(c32, 1.0), pad(s32, 0.0))
    return rc, rs, mc, ms


def _layer_params(l, norm_g, w_in, ret_decay, ret_norm, conv_w, conv_b, lru_gate_w, lru_gate_b, lru_lambda,
                  mla_q_norm, mla_kv_norm, mla_w_uq, mla_w_ukv, w_out, router_w, router_b,
                  moe_w_gu, moe_b_gu, moe_w_down, moe_b_down):
    p = {}
    p['g'] = [norm_g[l, i].reshape(1, D_MODEL) for i in range(4)]
    kr0 = 1920
    p['win'] = jnp.concatenate([w_in[l][:, :kr0], jnp.zeros((D_MODEL, MLA_DN), F32), w_in[l][:, kr0:],
                                jnp.zeros((D_MODEL, LANES - MLA_DN - MLA_DR), F32)], axis=1).astype(BF16)
    p['lg'] = jax.nn.log_sigmoid(ret_decay[l].astype(F32))
    p['retn'] = ret_norm[l].reshape(1, RET_W)
    p['qn'] = mla_q_norm[l].reshape(1, Q_RANK)
    p['kvn'] = mla_kv_norm[l].reshape(1, KV_RANK)
    wuq = mla_w_uq[l].reshape(Q_RANK, MLA_H, MLA_DN + MLA_DR)
    p['wuq'] = jnp.pad(wuq, ((0, 0), (0, 0), (0, LANES - MLA_DN - MLA_DR))).reshape(Q_RANK, MLA_H * LANES).astype(BF16)
    wukv = mla_w_ukv[l].reshape(KV_RANK, MLA_H, MLA_DN + MLA_DV)
    p['wk'] = jnp.pad(wukv[:, :, :MLA_DN], ((0, 0), (0, 0), (0, LANES - MLA_DN))).reshape(KV_RANK, MLA_H * LANES).astype(BF16)
    wv_t = jnp.pad(wukv[:, :, MLA_DN:].transpose(1, 2, 0), ((0, 0), (0, MLA_VT_ROWS - MLA_DV), (0, 0)))
    p['wv_t'] = wv_t.reshape(MLA_H * MLA_VT_ROWS, KV_RANK).astype(BF16)
    p['wout'] = w_out[l].astype(BF16)
    p['rw'] = jnp.pad(router_w[l], ((0, 0), (0, LANES - N_EXPERTS))).astype(BF16)
    p['rb'] = jnp.pad(router_b[l], (0, LANES - N_EXPERTS)).reshape(1, LANES)
    p['cw'] = conv_w[l].reshape(4, 2, LANES).transpose(1, 0, 2)
    p['cb'] = conv_b[l].reshape(2, 1, LANES)
    gw = lru_gate_w[l]
    halves = []
    for hh in range(2):
        cols = []
        for d in range(2):
            for g in range(2):
                blk = jnp.zeros((LANES, LANES), F32)
                for j in range(2):
                    blk = blk.at[j * LRU_BW:(j + 1) * LRU_BW, j * LRU_BW:(j + 1) * LRU_BW].set(gw[d, g, 2 * hh + j])
                cols.append(blk)
        halves.append(jnp.concatenate(cols, axis=1))
    p['wg'] = jnp.stack(halves).astype(BF16)
    gb = lru_gate_b[l].reshape(2, 2, 2, LANES)
    p['bg'] = gb.transpose(2, 0, 1, 3).reshape(2, 1, 4 * LANES)
    p['c8'] = (8.0 * jax.nn.log_sigmoid(lru_lambda[l].astype(F32))).reshape(2, 2, LANES).transpose(1, 0, 2)
    p['wgu'] = moe_w_gu[l].astype(BF16)
    p['bgu'] = moe_b_gu[l]
    p['wd'] = moe_w_down[l].astype(BF16)
    p['bd'] = moe_b_down[l]
    return p


def _forward(x_prompt, x_sample, c, state_ret, state_lru, cache_mla_ckv, cache_mla_krope, c_ctx, ada_w, ada_b,
             *weights):
    nb_ctx, t_ctx, _ = x_prompt.shape
    nb_lat, t_lat, _ = x_sample.shape
    geo = _Geometry(nb_ctx, t_ctx, nb_lat, t_lat)
    n_c = geo.n_ctx
    x = jnp.concatenate([x_prompt.reshape(n_c, D_MODEL), x_sample.reshape(-1, D_MODEL)], axis=0)
    cond = jnp.concatenate([c_ctx[None, :], c, jnp.zeros((16 - 1 - nb_lat, D_MODEL), F32)], axis=0)
    mod_all = _modulation(cond, ada_w, ada_b)[:, :1 + nb_lat].reshape(DEPTH, 1 + nb_lat, 6, D_MODEL)
    tabs = _rope_tables(t_lat)
    krope_pad = jnp.pad(cache_mla_krope, ((0, 0), (0, 0), (0, 0), (MLA_DN, LANES - MLA_DN - MLA_DR)))
    ret_out, lru_out, ckv_out, kr_out = [], [], [], []
    for l in range(DEPTH):
        p = _layer_params(l, *weights)
        mod = mod_all[l]
        rq, rk, rv, rg, lx, ly, q, ckvn, kr = _in_projection(x, mod, p['g'][0], p['win'], tabs, p['qn'], p['kvn'],
                                                             p['wuq'], geo)
        zero_s = jnp.zeros((nb_ctx, 2, 2, LANES, LANES), F32)
        ofc, obc, s_ctx = _retention(rq, rk, rv, p['lg'], zero_s, nb_ctx, t_ctx, 0)
        ofl, obl, _ = _retention(rq, rk, rv, p['lg'], _state_to_pairs(state_ret[:, l]), nb_lat, t_lat, n_c)
        lru_c, h_ctx = _lru(lx, ly, jnp.zeros((nb_ctx, 2, LRU_W), F32), p['c8'], p['cw'], p['cb'], p['wg'], p['bg'],
                            nb_ctx, t_ctx, 0)
        lru_l, _ = _lru(lx, ly, state_lru[:, l], p['c8'], p['cw'], p['cb'], p['wg'], p['bg'], nb_lat, t_lat, n_c)
        mla_c = _mla(q, ckvn, kr, None, p['wk'], p['wv_t'], nb_ctx, t_ctx, 0)
        mla_l = _mla(q, ckvn, kr, (cache_mla_ckv[:, l], krope_pad[:, l]), p['wk'], p['wv_t'], nb_lat, t_lat, n_c)
        x1, h2, route, counts = _out_projection((ofc, obc, lru_c, mla_c), (ofl, obl, lru_l, mla_l), rg, x, mod,
                                                p['retn'], p['g'][1], p['g'][2], p['wout'], p['rw'], p['rb'], geo)
        gate, slots3, pad_end, tile_expert, n_used, n_slots = _plan(route, counts)
        xs = _dispatch(h2, slots3, pad_end, n_slots)
        y_sorted = _moe(xs, tile_expert, n_used, p['wgu'], p['bgu'], p['wd'], p['bd'])
        x = _combine(y_sorted, slots3, gate, x1, mod, p['g'][3], geo)
        ret_out.append(_pairs_to_state(s_ctx))
        lru_out.append(h_ctx)
        ckv_out.append(ckvn[:n_c].reshape(nb_ctx, t_ctx, KV_RANK))
        kr_out.append(kr[:n_c, MLA_DN:MLA_DN + MLA_DR].reshape(nb_ctx, t_ctx, MLA_DR))
    y_prompt = x[:n_c].reshape(nb_ctx, t_ctx, D_MODEL)
    y_sample = x[n_c:].reshape(nb_lat, t_lat, D_MODEL)
    return (y_prompt, y_sample, jnp.stack(ret_out, axis=1), jnp.stack(lru_out, axis=1),
            jnp.stack(ckv_out, axis=1), jnp.stack(kr_out, axis=1))


def kernel(x_prompt, x_sample, c, state_ret, state_lru, cache_mla_ckv, cache_mla_krope, c_ctx, ada_w, ada_b, norm_g, w_in, ret_decay, ret_norm, conv_w, conv_b, lru_gate_w, lru_gate_b, lru_lambda, mla_q_norm, mla_kv_norm, mla_w_uq, mla_w_ukv, w_out, router_w, router_b, moe_w_gu, moe_b_gu, moe_w_down, moe_b_down):
    return _forward(x_prompt, x_sample, c, state_ret, state_lru, cache_mla_ckv, cache_mla_krope, c_ctx, ada_w, ada_b,
                    norm_g, w_in, ret_decay, ret_norm, conv_w, conv_b, lru_gate_w, lru_gate_b, lru_lambda,
                    mla_q_norm, mla_kv_norm, mla_w_uq, mla_w_ukv, w_out, router_w, router_b,
                    moe_w_gu, moe_b_gu, moe_w_down, moe_b_down)
```

```python
import functools
import math

import jax
import jax.numpy as jnp
from jax import lax
from jax.experimental import pallas as pl
from jax.experimental.pallas import tpu as pltpu

F32 = jnp.float32
BF16 = jnp.bfloat16

D_MODEL = 1024
DEPTH = 2
GRID_W = 64
RET_H = 4
RET_DK = 64
RET_W = 256
RET_CHUNK = 128
LRU_W = 256
LRU_BLOCKS = 4
LRU_BW = 64
MLA_H = 8
MLA_DN = 64
MLA_DR = 32
MLA_DV = 64
MLA_W = MLA_H * MLA_DV
MLA_VT_ROWS = 80
Q_RANK = 256
KV_RANK = 128
ROPE_BASE = 10000.0
N_EXPERTS = 32
TOP_K = 4
D_EXPERT = 1024
SWIGLU_ALPHA = 1.702
SWIGLU_LIMIT = 7.0
EPS = 1e-6

LANES = 128
ROW_TILE = 256
MOE_TILE = 256
LRU_ROWS = 256
ATT_TQ = 256
ATT_KC = 256
ATT_AHEAD = 4
KV_BUILD_ROWS = 512
PROJ_PAD = 2048
VMEM_LIMIT = 56 * 1024 * 1024


def _cparams(sem):
    return pltpu.CompilerParams(dimension_semantics=sem, vmem_limit_bytes=VMEM_LIMIT)


def _rms(x, g):
    return x * lax.rsqrt(jnp.mean(x * x, axis=-1, keepdims=True) + EPS) * g


def _rope(x, c, s, quarter):
    w = x.shape[-1]
    lane = lax.broadcasted_iota(jnp.int32, x.shape, 1)
    first = (lane % (2 * quarter)) < quarter
    partner = jnp.where(first, pltpu.roll(x, w - quarter, 1), pltpu.roll(x, quarter, 1))
    return x * c + partner * s


def _mod_kernel(cond_ref, w_ref, b_ref, o_ref):
    c = cond_ref[...]
    s = (c * jax.nn.sigmoid(c)).astype(BF16)
    o_ref[0] = jnp.dot(s, w_ref[0].astype(BF16), preferred_element_type=F32) + b_ref[0]


def _modulation(cond, ada_w, ada_b):
    nblk = 6
    return pl.pallas_call(
        _mod_kernel,
        grid=(DEPTH, nblk),
        in_specs=[
            pl.BlockSpec((16, D_MODEL), lambda l, j: (0, 0)),
            pl.BlockSpec((1, D_MODEL, D_MODEL), lambda l, j: (l, 0, j)),
            pl.BlockSpec((1, 1, D_MODEL), lambda l, j: (l, 0, j)),
        ],
        out_specs=pl.BlockSpec((1, 16, D_MODEL), lambda l, j: (l, 0, j)),
        out_shape=jax.ShapeDtypeStruct((DEPTH, 16, 6 * D_MODEL), F32),
        compiler_params=_cparams(("arbitrary", "arbitrary")),
        name="adaln_mod",
    )(cond, ada_w, ada_b.reshape(DEPTH, 1, 6 * D_MODEL))


def _inproj_kernel(xc_ref, xl_ref, mod_ref, g0_ref, win_ref, rc_ref, rs_ref, mc_ref, ms_ref, qn_ref, kvn_ref, wuq_ref,
                   rq_ref, rk_ref, rv_ref, rg_ref, lx_ref, ly_ref, q_ref, ckv_ref, kr_ref, *, ctx_tiles):
    x = jnp.where(pl.program_id(0) < ctx_tiles, xc_ref[...], xl_ref[...])
    mod = mod_ref[0]
    h = _rms(x, g0_ref[...]) * (1.0 + mod[1:2]) + mod[0:1]
    z = jnp.dot(h.astype(BF16), win_ref[...], preferred_element_type=F32)
    rc = rc_ref[...]
    rs = rs_ref[...]
    rq_ref[...] = _rope(z[:, 0:256], rc, rs, 16)
    rk_ref[...] = _rope(z[:, 256:512] * (RET_DK ** -0.5), rc, rs, 16)
    rv_ref[...] = z[:, 512:768]
    rg_ref[...] = z[:, 768:1024]
    lx_ref[...] = z[:, 1024:1280]
    ly_ref[...] = z[:, 1280:1536]
    cqn = _rms(z[:, 1536:1792], qn_ref[...])
    q = jnp.dot(cqn.astype(BF16), wuq_ref[...], preferred_element_type=F32)
    mc = mc_ref[...]
    ms = ms_ref[...]
    for h_i in range(MLA_H):
        sl = slice(h_i * LANES, (h_i + 1) * LANES)
        q_ref[:, sl] = _rope(q[:, sl], mc, ms, 8).astype(BF16)
    ckv_ref[...] = _rms(z[:, 1792:1920], kvn_ref[...])
    kr_ref[...] = _rope(z[:, 1920:2048], mc, ms, 8)


def _in_projection(x_ctx, x_lat, mod, g0, win_p, tabs, qn, kvn, wuq_p, geo):
    n = geo.n
    nt = n // ROW_TILE
    rc, rs, mc, ms = tabs
    row = lambda w: pl.BlockSpec((ROW_TILE, w), lambda i: (i, 0))
    full = lambda a: pl.BlockSpec(a.shape, lambda i: (0,) * a.ndim)
    tab = lambda w: pl.BlockSpec((ROW_TILE, w), lambda i: (geo.rope_block(i), 0))
    outs = [(256, F32)] * 6 + [(MLA_H * LANES, BF16), (LANES, F32), (LANES, F32)]
    return pl.pallas_call(
        functools.partial(_inproj_kernel, ctx_tiles=geo.ctx_tiles),
        grid=(nt,),
        in_specs=[geo.ctx_rows(D_MODEL), geo.lat_rows(D_MODEL),
                  pl.BlockSpec((1, 6, D_MODEL), lambda i: (geo.mod_group(i), 0, 0)),
                  full(g0), full(win_p), tab(256), tab(256), tab(LANES), tab(LANES), full(qn), full(kvn), full(wuq_p)],
        out_specs=[row(w) for w, _ in outs],
        out_shape=[jax.ShapeDtypeStruct((n, w), dt) for w, dt in outs],
        compiler_params=_cparams(("arbitrary",)),
        name="in_projection",
    )(x_ctx, x_lat, mod, g0, win_p, rc, rs, mc, ms, qn, kvn, wuq_p)


def _ret_kernel(lg_ref, qf_ref, kf_ref, vf_ref, qb_ref, kb_ref, vb_ref, s0_ref,
                of_ref, ob_ref, sfin_ref, s_scr, intra_scr, cross_scr, into_scr, carry_scr):
    j = pl.program_id(1)
    c = RET_CHUNK
    row = lax.broadcasted_iota(jnp.int32, (c, c), 0)
    lane = lax.broadcasted_iota(jnp.int32, (c, c), 1)
    rowf = row.astype(F32)
    lanef = lane.astype(F32)

    @pl.when(j == 0)
    def _init():
        s_scr[...] = s0_ref[0]
        for d in range(2):
            for h in range(RET_H):
                lg = lg_ref[d, h]
                if d == 0:
                    keep = row >= lane
                    dist = rowf - lanef
                else:
                    keep = lane >= row
                    dist = lanef - rowf
                intra_scr[d, h] = jnp.where(keep, jnp.exp(jnp.where(keep, dist, 0.0) * lg), 0.0)
            for p in range(2):
                lgl = jnp.where(lane < RET_DK, lg_ref[d, 2 * p], lg_ref[d, 2 * p + 1])
                if d == 0:
                    cross_scr[d, p] = jnp.exp((rowf + 1.0) * lgl)
                    into_scr[d, p] = jnp.exp((c - 1.0 - rowf) * lgl)
                else:
                    cross_scr[d, p] = jnp.exp((c - rowf) * lgl)
                    into_scr[d, p] = jnp.exp(rowf * lgl)
                carry_scr[d, p] = jnp.exp(float(c) * lgl)

    same_head = (row < RET_DK) == (lane < RET_DK)

    chains = [(d, p, refs) for d, refs in ((0, (qf_ref, kf_ref, vf_ref, of_ref)), (1, (qb_ref, kb_ref, vb_ref, ob_ref)))
              for p in range(2)]
    staged = []
    for d, p, (q_ref, k_ref, v_ref, _) in chains:
        sl = slice(p * LANES, (p + 1) * LANES)
        q2b = q_ref[:, sl].astype(BF16)
        k2 = k_ref[:, sl]
        v2 = v_ref[:, sl]
        scores, vals = [], []
        for e in range(2):
            sel = (lane >= RET_DK) if e else (lane < RET_DK)
            ke = jnp.where(sel, k2, 0.0).astype(BF16)
            vals.append(jnp.where(sel, v2, 0.0).astype(BF16))
            scores.append(lax.dot_general(q2b, ke, (((1,), (1,)), ((), ())), preferred_element_type=F32))
        st = s_scr[d, p]
        from_state = jnp.dot(q2b, st.astype(BF16), preferred_element_type=F32)
        kw = (k2 * into_scr[d, p]).astype(BF16)
        upd = lax.dot_general(kw, v2.astype(BF16), (((0,), (0,)), ((), ())), preferred_element_type=F32)
        staged.append((scores, vals, st, from_state, upd))
    for (d, p, (_, _, _, o_ref)), (scores, vals, st, from_state, upd) in zip(chains, staged):
        o = from_state * cross_scr[d, p]
        for e in range(2):
            a = (scores[e] * intra_scr[d, 2 * p + e]).astype(BF16)
            o = o + jnp.dot(a, vals[e], preferred_element_type=F32)
        s_scr[d, p] = st * carry_scr[d, p] + jnp.where(same_head, upd, 0.0)
        o_ref[:, p * LANES:(p + 1) * LANES] = o

    @pl.when(j == pl.num_programs(1) - 1)
    def _fin():
        sfin_ref[0] = s_scr[...]


def _retention(rq, rk, rv, lg, s0, nb, t, row0):
    n = nb * t
    nc = t // RET_CHUNK
    base = row0 // RET_CHUNK
    fwd = lambda off: pl.BlockSpec((RET_CHUNK, RET_W), lambda b, j: (off + b * nc + j, 0))
    bwd = lambda off: pl.BlockSpec((RET_CHUNK, RET_W), lambda b, j: (off + b * nc + nc - 1 - j, 0))
    st = pl.BlockSpec((1, 2, 2, LANES, LANES), lambda b, j: (b, 0, 0, 0, 0))
    return pl.pallas_call(
        _ret_kernel,
        grid=(nb, nc),
        in_specs=[pl.BlockSpec(memory_space=pltpu.SMEM), fwd(base), fwd(base), fwd(base), bwd(base), bwd(base),
                  bwd(base), st],
        out_specs=[fwd(0), bwd(0), st],
        out_shape=[jax.ShapeDtypeStruct((n, RET_W), F32), jax.ShapeDtypeStruct((n, RET_W), F32),
                   jax.ShapeDtypeStruct((nb, 2, 2, LANES, LANES), F32)],
        scratch_shapes=[pltpu.VMEM((2, 2, LANES, LANES), F32), pltpu.VMEM((2, RET_H, RET_CHUNK, RET_CHUNK), F32),
                        pltpu.VMEM((2, 2, RET_CHUNK, LANES), F32), pltpu.VMEM((2, 2, RET_CHUNK, LANES), F32),
                        pltpu.VMEM((2, 2, RET_CHUNK, LANES), F32)],
        compiler_params=_cparams(("arbitrary", "arbitrary")),
        name="retention",
    )(lg, rq, rk, rv, rq, rk, rv, s0)


def _state_to_pairs(s):
    b = s.shape[0]
    s = s.reshape(b, 2, 2, 2, RET_DK, RET_DK)
    z = jnp.zeros_like(s[:, :, :, 0])
    top = jnp.concatenate([s[:, :, :, 0], z], axis=-1)
    bot = jnp.concatenate([z, s[:, :, :, 1]], axis=-1)
    return jnp.concatenate([top, bot], axis=-2)


def _pairs_to_state(s):
    b = s.shape[0]
    a = s[..., :RET_DK, :RET_DK]
    c = s[..., RET_DK:, RET_DK:]
    return jnp.stack([a, c], axis=3).reshape(b, 2, RET_H, RET_DK, RET_DK)


def _scan_rows(a, b, reverse):
    r = a.shape[0]
    rows = lax.broadcasted_iota(jnp.int32, a.shape, 0)
    s = 1
    while s < r:
        if reverse:
            a_s = pltpu.roll(a, r - s, 0)
            b_s = pltpu.roll(b, r - s, 0)
            m = rows < r - s
        else:
            a_s = pltpu.roll(a, s, 0)
            b_s = pltpu.roll(b, s, 0)
            m = rows >= s
        b = jnp.where(m, a * b_s + b, b)
        a = jnp.where(m, a * a_s, a)
        s *= 2
    return a, b


def _lru_kernel(c8_ref, cw_ref, cb_ref, wg_ref, bg_ref, lx_ref, ly_ref, h0_ref, o_ref, hfin_ref, xc_scr, hf_scr,
                *, t, r):
    nch = t // r
    cw = cw_ref[0]
    cb = cb_ref[0]
    wg = wg_ref[0]
    bg = bg_ref[0]
    c8 = c8_ref[0]

    def conv_body(c, carry):
        r0 = pl.multiple_of(c * r, r)
        cur = lx_ref[pl.ds(r0, r), :]
        prev = lx_ref[pl.ds(pl.multiple_of(jnp.maximum(r0 - 8, 0), 8), 8), :]
        nxt = lx_ref[pl.ds(pl.multiple_of(jnp.minimum(r0 + r, t - 8), 8), 8), :]
        prev = jnp.where(c > 0, prev, 0.0)
        nxt = jnp.where(c < nch - 1, nxt, 0.0)
        ext = jnp.concatenate([prev, cur, nxt], axis=0)
        acc = jnp.broadcast_to(cb, (r, LANES))
        for tap in range(4):
            sh = (2 - tap) % (r + 16)
            xs = ext if sh == 0 else pltpu.roll(ext, sh, 0)
            acc = acc + xs[8:8 + r] * cw[tap:tap + 1]
        xc_scr[pl.ds(r0, r), :] = acc
        return carry

    lax.fori_loop(0, nch, conv_body, 0)

    def gates(xc, d):
        g = jnp.dot(xc.astype(BF16), wg[:, d * 256:(d + 1) * 256], preferred_element_type=F32) + bg[:, d * 256:(d + 1) * 256]
        rg = jax.nn.sigmoid(g[:, :LANES])
        ig = jax.nn.sigmoid(g[:, LANES:])
        log_a = c8[d:d + 1] * rg
        a = jnp.exp(log_a)
        b = jnp.sqrt(jnp.tanh(-log_a) * (a * a + 1.0)) * (ig * xc)
        return a, b

    def fwd_body(c, h):
        r0 = pl.multiple_of(c * r, r)
        a, b = gates(xc_scr[pl.ds(r0, r), :], 0)
        a, b = _scan_rows(a, b, False)
        hc = a * h + b
        hf_scr[pl.ds(r0, r), :] = hc
        return hc[r - 1:r]

    h_f = lax.fori_loop(0, nch, fwd_body, h0_ref[0, 0:1, :])

    def bwd_body(i, h):
        r0 = pl.multiple_of((nch - 1 - i) * r, r)
        a, b = gates(xc_scr[pl.ds(r0, r), :], 1)
        a, b = _scan_rows(a, b, True)
        hc = a * h + b
        o_ref[pl.ds(r0, r), :] = ((hf_scr[pl.ds(r0, r), :] + hc) * jax.nn.gelu(ly_ref[pl.ds(r0, r), :])).astype(BF16)
        return hc[0:1]

    h_b = lax.fori_loop(0, nch, bwd_body, h0_ref[0, 1:2, :])
    hfin_ref[0, 0:1, :] = h_f
    hfin_ref[0, 1:2, :] = h_b


def _lru(lx, ly, h0, c8, cw, cb, wg, bg, nb, t, row0):
    base = row0 // t
    r = min(LRU_ROWS, t)
    seq = lambda off: pl.BlockSpec((t, LANES), lambda b, hh: (off + b, hh))
    par = lambda a: pl.BlockSpec((1,) + a.shape[1:], lambda b, hh: (hh,) + (0,) * (a.ndim - 1))
    st = pl.BlockSpec((1, 2, LANES), lambda b, hh: (b, 0, hh))
    return pl.pallas_call(
        functools.partial(_lru_kernel, t=t, r=r),
        grid=(nb, 2),
        in_specs=[par(c8), par(cw), par(cb), par(wg), par(bg), seq(base), seq(base), st],
        out_specs=[seq(0), st],
        out_shape=[jax.ShapeDtypeStruct((nb * t, LRU_W), BF16), jax.ShapeDtypeStruct((nb, 2, LRU_W), F32)],
        scratch_shapes=[pltpu.VMEM((t, LANES), F32), pltpu.VMEM((t, LANES), F32)],
        compiler_params=_cparams(("arbitrary", "arbitrary")),
        name="rg_lru",
    )(c8, cw, cb, wg, bg, lx, ly, h0)


def _mla_kernel(*refs, n_ctx, t, tq):
    if n_ctx:
        q_ref, ckv_ref, kr_ref, cckv_ref, ckr_ref, wk_ref, wv_ref = refs[:7]
    else:
        q_ref, ckv_ref, kr_ref, wk_ref, wv_ref = refs[:5]
    o_ref, k_scr, vt_scr, qt_scr, ot_scr = refs[-5:]
    qi = pl.program_id(1)
    cexp = (MLA_DN + MLA_DR) ** -0.5 * math.log2(math.e)
    ones_row = (lax.broadcasted_iota(jnp.int32, (MLA_H * MLA_VT_ROWS, 1), 0) % MLA_VT_ROWS == MLA_DV).astype(F32)

    def put(ckv, kr, c0):
        nrow = ckv.shape[0]
        ckv_b = ckv.astype(BF16)
        kn = jnp.dot(ckv_b, wk_ref[...], preferred_element_type=F32)
        vt = lax.dot_general(wv_ref[...], ckv_b, (((1,), (1,)), ((), ())), preferred_element_type=F32) + ones_row
        for h in range(MLA_H):
            k_scr[h, c0:c0 + nrow, :] = ((kn[:, h * LANES:(h + 1) * LANES] + kr) * cexp).astype(BF16)
            vt_scr[h, :, c0:c0 + nrow] = vt[h * MLA_VT_ROWS:(h + 1) * MLA_VT_ROWS].astype(BF16)

    @pl.when(qi == 0)
    def _build():
        if n_ctx:
            put(cckv_ref[0], ckr_ref[0], 0)
        step = min(KV_BUILD_ROWS, t)
        for c in range(t // step):
            put(ckv_ref[c * step:(c + 1) * step, :], kr_ref[c * step:(c + 1) * step, :], n_ctx + c * step)

    qt_scr[...] = q_ref[...].astype(F32).T.astype(BF16)
    s_len = n_ctx + t

    nchunk = s_len // ATT_KC
    steps = [(h, c) for h in range(MLA_H) for c in range(nchunk)]

    def score(i):
        h, c = steps[i]
        return jnp.dot(k_scr[h, c * ATT_KC:(c + 1) * ATT_KC, :], qt_scr[h * LANES:(h + 1) * LANES, :],
                       preferred_element_type=F32)

    pending = {i: score(i) for i in range(min(ATT_AHEAD, len(steps)))}
    held = None
    m = o = None
    for i in range(len(steps) + 1):
        if i + ATT_AHEAD < len(steps):
            pending[i + ATT_AHEAD] = score(i + ATT_AHEAD)
        if i < len(steps):
            h, c = steps[i]
            s = pending.pop(i)
            m_old = jnp.full((1, tq), -1e30, F32) if c == 0 else m
            m = jnp.maximum(m_old, jnp.max(s, axis=0, keepdims=True))
            p = jnp.exp2(s - m).astype(BF16)
            alpha = jnp.exp2(m_old - m)
        if held is not None:
            hh, cc, p_h, alpha_h = held
            pv = jnp.dot(vt_scr[hh, :, cc * ATT_KC:(cc + 1) * ATT_KC], p_h, preferred_element_type=F32)
            o = pv if cc == 0 else o * alpha_h + pv
            if cc == nchunk - 1:
                ot_scr[hh] = o[:MLA_DV] / o[MLA_DV:MLA_DV + 1]
        held = (h, c, p, alpha) if i < len(steps) else None
    for pp in range(MLA_H // 2):
        pair = jnp.concatenate([ot_scr[2 * pp], ot_scr[2 * pp + 1]], axis=0)
        o_ref[:, pp * LANES:(pp + 1) * LANES] = pair.T.astype(BF16)


def _mla(q, ckvn, kr, cache, wk, wv_t, nb, t, row0):
    tq = min(ATT_TQ, t)
    nq = t // tq
    n_ctx = 0 if cache is None else cache[0].shape[1]
    s_len = n_ctx + t
    qspec = pl.BlockSpec((tq, MLA_H * LANES), lambda b, i: (row0 // tq + b * nq + i, 0))
    seq = pl.BlockSpec((t, LANES), lambda b, i: (row0 // t + b, 0))
    full = lambda a: pl.BlockSpec(a.shape, lambda b, i: (0,) * a.ndim)
    ins = [q, ckvn, kr]
    specs = [qspec, seq, seq]
    if n_ctx:
        cspec = pl.BlockSpec((1, n_ctx, LANES), lambda b, i: (b, 0, 0))
        ins += [cache[0], cache[1]]
        specs += [cspec, cspec]
    ins += [wk, wv_t]
    specs += [full(wk), full(wv_t)]
    return pl.pallas_call(
        functools.partial(_mla_kernel, n_ctx=n_ctx, t=t, tq=tq),
        grid=(nb, nq),
        in_specs=specs,
        out_specs=pl.BlockSpec((tq, MLA_W), lambda b, i: (b * nq + i, 0)),
        out_shape=jax.ShapeDtypeStruct((nb * t, MLA_W), BF16),
        scratch_shapes=[pltpu.VMEM((MLA_H, s_len, LANES), BF16), pltpu.VMEM((MLA_H, MLA_VT_ROWS, s_len), BF16),
                        pltpu.VMEM((MLA_H * LANES, tq), BF16), pltpu.VMEM((MLA_H, MLA_DV, tq), F32)],
        compiler_params=_cparams(("arbitrary", "arbitrary")),
        name="mla_attention",
    )(*ins)


def _outproj_kernel(ofc_ref, obc_ref, lruc_ref, mlac_ref, xc_ref, ofl_ref, obl_ref, lrul_ref, mlal_ref, xl_ref, rg_ref,
                    mod_ref, retn_ref, g1_ref, g2_ref, wout_ref, rw_ref, rb_ref, x1_ref, h2_ref, route_ref, cnt_ref,
                    cnt_scr, *, ctx_tiles):
    is_ctx = pl.program_id(0) < ctx_tiles
    pick = lambda c_ref, l_ref: jnp.where(is_ctx, c_ref[...], l_ref[...])
    o = pick(ofc_ref, ofl_ref) + pick(obc_ref, obl_ref)
    hid = lax.broadcasted_iota(jnp.int32, o.shape, 1) // RET_DK

    def head_sum(v):
        tot = jnp.zeros_like(v)
        for hh in range(RET_H):
            msk = hid == hh
            tot = jnp.where(msk, jnp.sum(jnp.where(msk, v, 0.0), axis=1, keepdims=True), tot)
        return tot

    mu = head_sum(o) * (1.0 / RET_DK)
    dl = o - mu
    var = head_sum(dl * dl) * (1.0 / RET_DK)
    rg = rg_ref[...]
    ret = dl * lax.rsqrt(var + EPS) * retn_ref[...] * (rg * jax.nn.sigmoid(rg))
    mix = jnp.concatenate([ret.astype(BF16), pick(lruc_ref, lrul_ref), pick(mlac_ref, mlal_ref)], axis=1)
    mo = jnp.dot(mix, wout_ref[...], preferred_element_type=F32)
    mod = mod_ref[0]
    x1 = pick(xc_ref, xl_ref) + mod[2:3] * _rms(mo, g1_ref[...])
    x1_ref[...] = x1
    h2 = _rms(x1, g2_ref[...]) * (1.0 + mod[4:5]) + mod[3:4]
    h2_ref[...] = h2

    tm = h2.shape[0]
    lane = lax.broadcasted_iota(jnp.int32, (tm, LANES), 1)
    lanef = lane.astype(F32)
    lg = jnp.dot(h2.astype(BF16), rw_ref[...], preferred_element_type=F32) + rb_ref[...]
    lg = jnp.where(lane < N_EXPERTS, lg, -jnp.inf)
    tops, idxs, hots = [], [], []
    for _ in range(TOP_K):
        m = jnp.max(lg, axis=1, keepdims=True)
        idx = jnp.min(jnp.where(lg == m, lanef, float(LANES)), axis=1, keepdims=True)
        hot = lanef == idx
        lg = jnp.where(hot, -jnp.inf, lg)
        tops.append(m)
        idxs.append(idx)
        hots.append(hot)
    exps = [jnp.exp(t - tops[0]) for t in tops]
    den = exps[0] + exps[1] + exps[2] + exps[3]
    member = jnp.zeros((tm, LANES), F32)
    for hot in hots:
        member = member + hot.astype(F32)

    @pl.when(pl.program_id(0) == 0)
    def _zero_counts():
        cnt_scr[...] = jnp.zeros_like(cnt_scr)

    earlier = (lax.broadcasted_iota(jnp.int32, (tm, tm), 0) > lax.broadcasted_iota(jnp.int32, (tm, tm), 1))
    before = jnp.dot(earlier.astype(BF16), member.astype(BF16), preferred_element_type=F32) + cnt_scr[0:1, :]
    route = jnp.zeros((tm, LANES), F32)
    for k in range(TOP_K):
        rank = jnp.sum(jnp.where(hots[k], before, 0.0), axis=1, keepdims=True)
        route = jnp.where(lane == k, idxs[k], route)
        route = jnp.where(lane == TOP_K + k, exps[k] / den, route)
        route = jnp.where(lane == 2 * TOP_K + k, rank, route)
    route_ref[...] = route
    cnt_scr[...] = cnt_scr[...] + jnp.sum(member, axis=0, keepdims=True)
    cnt_ref[...] = cnt_scr[...]


def _out_projection(ctx_mix, lat_mix, rg, mod, retn, g1, g2, wout, rw, rb, geo):
    n = geo.n
    row = lambda w: pl.BlockSpec((ROW_TILE, w), lambda i: (i, 0))
    full = lambda a: pl.BlockSpec(a.shape, lambda i: (0,) * a.ndim)
    widths = (256, 256, 256, MLA_W, D_MODEL)
    return pl.pallas_call(
        functools.partial(_outproj_kernel, ctx_tiles=geo.ctx_tiles),
        grid=(n // ROW_TILE,),
        in_specs=[geo.ctx_rows(w) for w in widths] + [geo.lat_rows(w) for w in widths] + [
                  row(256),
                  pl.BlockSpec((1, 6, D_MODEL), lambda i: (geo.mod_group(i), 0, 0)),
                  full(retn), full(g1), full(g2), full(wout), full(rw), full(rb)],
        out_specs=[row(D_MODEL), row(D_MODEL), row(LANES), pl.BlockSpec((8, LANES), lambda i: (0, 0))],
        out_shape=[jax.ShapeDtypeStruct((n, D_MODEL), F32), jax.ShapeDtypeStruct((n, D_MODEL), F32),
                   jax.ShapeDtypeStruct((n, LANES), F32), jax.ShapeDtypeStruct((8, LANES), F32)],
        scratch_shapes=[pltpu.VMEM((8, LANES), F32)],
        compiler_params=_cparams(("arbitrary",)),
        name="out_projection",
    )(*ctx_mix, *lat_mix, rg, mod, retn, g1, g2, wout, rw, rb)


def _row_copy(src_hbm, src_row, dst, dst_row, sem):
    return pltpu.make_async_copy(src_hbm.at[pl.ds(src_row, 1), :], dst.at[pl.ds(dst_row, 1), :], sem)


def _dispatch_kernel(pe_ref, slot_ref, h_ref, xs_hbm, zero_scr, sem):
    i = pl.program_id(0)
    n_slots = xs_hbm.shape[0]

    @pl.when(i == 0)
    def _zero_pads():
        zero_scr[...] = jnp.zeros_like(zero_scr)

        def fill(e):
            end = pe_ref[e]
            start = 0 if e == 0 else pe_ref[e - 1]
            dst = xs_hbm.at[pl.ds(pl.multiple_of(jnp.maximum(end - MOE_TILE, 0), MOE_TILE), MOE_TILE), :]
            return end > start, pltpu.make_async_copy(zero_scr, dst, sem)

        def fill_tail(j):
            row = pe_ref[N_EXPERTS - 1] + j * MOE_TILE
            dst = xs_hbm.at[pl.ds(pl.multiple_of(jnp.minimum(row, n_slots - MOE_TILE), MOE_TILE), MOE_TILE), :]
            return row < n_slots, pltpu.make_async_copy(zero_scr, dst, sem)

        for e in range(N_EXPERTS):
            for todo, cp in (fill(e), fill_tail(e)):
                pl.when(todo)(cp.start)
        for e in range(N_EXPERTS):
            for todo, cp in (fill(e), fill_tail(e)):
                pl.when(todo)(cp.wait)

    for r in range(ROW_TILE):
        for k in range(TOP_K):
            dst = xs_hbm.at[pl.ds(slot_ref[0, 0, k * ROW_TILE + r], 1), :]
            pltpu.make_async_copy(h_ref.at[pl.ds(r, 1), :], dst, sem).start(priority=k % 2)
    for k in range(TOP_K):
        pltpu.make_async_copy(h_ref, xs_hbm.at[pl.ds(0, ROW_TILE), :], sem).wait()


def _dispatch(h2, slots3, pad_end, n_slots):
    n = h2.shape[0]
    grid_spec = pltpu.PrefetchScalarGridSpec(
        num_scalar_prefetch=1,
        grid=(n // ROW_TILE,),
        in_specs=[
            pl.BlockSpec((1, 1, TOP_K * ROW_TILE), lambda i, pe: (i, 0, 0), memory_space=pltpu.SMEM),
            pl.BlockSpec((ROW_TILE, D_MODEL), lambda i, pe: (i, 0)),
        ],
        out_specs=pl.BlockSpec(memory_space=pl.ANY),
        scratch_shapes=[pltpu.VMEM((MOE_TILE, D_MODEL), F32), pltpu.SemaphoreType.DMA(())],
    )
    return pl.pallas_call(
        _dispatch_kernel,
        grid_spec=grid_spec,
        out_shape=jax.ShapeDtypeStruct((n_slots, D_MODEL), F32),
        compiler_params=_cparams(("arbitrary",)),
        name="moe_dispatch",
    )(pad_end, slots3, h2)


def _moe_kernel(te_ref, nu_ref, x_ref, wgu_ref, bgu_ref, wd_ref, bd_ref, y_ref):
    del te_ref
    t = pl.program_id(0)
    n_used = nu_ref[0]

    @pl.when(t < n_used)
    def _compute():
        x = x_ref[...].astype(BF16)
        gu = jnp.dot(x, wgu_ref[0], preferred_element_type=F32) + bgu_ref[0]
        g = jnp.minimum(gu[:, :D_EXPERT], SWIGLU_LIMIT)
        u = jnp.clip(gu[:, D_EXPERT:], -SWIGLU_LIMIT, SWIGLU_LIMIT)
        act = g * jax.nn.sigmoid(SWIGLU_ALPHA * g) * (u + 1.0)
        y_ref[...] = jnp.dot(act.astype(BF16), wd_ref[0], preferred_element_type=F32) + bd_ref[0]

    @pl.when(t >= n_used)
    def _idle():
        y_ref[...] = jnp.zeros_like(y_ref)


def _moe(xs, tile_expert, n_used, wgu, bgu, wd, bd):
    n_tiles = tile_expert.shape[0]
    grid_spec = pltpu.PrefetchScalarGridSpec(
        num_scalar_prefetch=2,
        grid=(n_tiles,),
        in_specs=[
            pl.BlockSpec((MOE_TILE, D_MODEL), lambda t, te, nu: (jnp.minimum(t, jnp.maximum(nu[0] - 1, 0)), 0)),
            pl.BlockSpec((1, D_MODEL, 2 * D_EXPERT), lambda t, te, nu: (te[t], 0, 0)),
            pl.BlockSpec((1, 1, 2 * D_EXPERT), lambda t, te, nu: (te[t], 0, 0)),
            pl.BlockSpec((1, D_EXPERT, D_MODEL), lambda t, te, nu: (te[t], 0, 0)),
            pl.BlockSpec((1, 1, D_MODEL), lambda t, te, nu: (te[t], 0, 0)),
        ],
        out_specs=pl.BlockSpec((MOE_TILE, D_MODEL), lambda t, te, nu: (t, 0)),
    )
    return pl.pallas_call(
        _moe_kernel,
        grid_spec=grid_spec,
        out_shape=jax.ShapeDtypeStruct((n_tiles * MOE_TILE, D_MODEL), F32),
        compiler_params=_cparams(("arbitrary",)),
        name="moe_experts",
    )(tile_expert, n_used, xs, wgu, bgu.reshape(N_EXPERTS, 1, -1), wd, bd.reshape(N_EXPERTS, 1, -1))


def _combine_kernel(inv_ref, y_hbm, gate_ref, x1_ref, mod_ref, g3_ref, oc_ref, ol_ref, buf, sem, *, ctx_tiles):
    for r in range(ROW_TILE):
        for k in range(TOP_K):
            _row_copy(y_hbm, inv_ref[0, 0, k * ROW_TILE + r], buf.at[k], r, sem).start(priority=k % 2)
    for k in range(TOP_K):
        pltpu.make_async_copy(y_hbm.at[pl.ds(0, ROW_TILE), :], buf.at[k], sem).wait()
    gate = gate_ref[...]
    ff = gate[:, 0:1] * buf[0]
    for k in range(1, TOP_K):
        ff = ff + gate[:, k:k + 1] * buf[k]
    out = x1_ref[...] + mod_ref[0][5:6] * _rms(ff, g3_ref[...])
    is_ctx = pl.program_id(0) < ctx_tiles

    @pl.when(is_ctx)
    def _ctx():
        oc_ref[...] = out

    @pl.when(jnp.logical_not(is_ctx))
    def _lat():
        ol_ref[...] = out


def _combine(y_sorted, inv3, gate, x1, mod, g3, geo):
    n = x1.shape[0]
    nt = n // ROW_TILE
    return pl.pallas_call(
        functools.partial(_combine_kernel, ctx_tiles=geo.ctx_tiles),
        grid=(nt,),
        in_specs=[
            pl.BlockSpec((1, 1, TOP_K * ROW_TILE), lambda i: (i, 0, 0), memory_space=pltpu.SMEM),
            pl.BlockSpec(memory_space=pl.ANY),
            pl.BlockSpec((ROW_TILE, TOP_K), lambda i: (i, 0)),
            pl.BlockSpec((ROW_TILE, D_MODEL), lambda i: (i, 0)),
            pl.BlockSpec((1, 6, D_MODEL), lambda i: (geo.mod_group(i), 0, 0)),
            pl.BlockSpec((1, D_MODEL), lambda i: (0, 0)),
        ],
        out_specs=[geo.ctx_rows(D_MODEL), geo.lat_rows(D_MODEL)],
        out_shape=[jax.ShapeDtypeStruct((geo.n_ctx, D_MODEL), F32), jax.ShapeDtypeStruct((n - geo.n_ctx, D_MODEL), F32)],
        scratch_shapes=[pltpu.VMEM((TOP_K, ROW_TILE, D_MODEL), F32), pltpu.SemaphoreType.DMA(())],
        compiler_params=_cparams(("arbitrary",)),
        name="moe_combine",
    )(inv3, y_sorted, gate, x1, mod, g3)


def _plan(route, counts):
    n = route.shape[0]
    nt = n // ROW_TILE
    expert = route[:, 0:TOP_K].astype(jnp.int32)
    gate = route[:, TOP_K:2 * TOP_K]
    rank = route[:, 2 * TOP_K:3 * TOP_K].astype(jnp.int32)
    cnt = counts[0, :N_EXPERTS].astype(jnp.int32)
    padded = (cnt + MOE_TILE - 1) // MOE_TILE * MOE_TILE
    pad_end = jnp.cumsum(padded).astype(jnp.int32)
    pad_start = pad_end - padded
    start_of = jnp.sum(jnp.where(expert[:, :, None] == jnp.arange(N_EXPERTS), pad_start, 0), axis=-1)
    slots = start_of + rank
    slots3 = slots.reshape(nt, ROW_TILE, TOP_K).transpose(0, 2, 1).reshape(nt, 1, TOP_K * ROW_TILE)
    n_tiles = n * TOP_K // MOE_TILE + N_EXPERTS
    tile_start = jnp.arange(n_tiles, dtype=jnp.int32) * MOE_TILE
    tile_expert = jnp.minimum(jnp.sum((tile_start[:, None] >= pad_end[None, :]).astype(jnp.int32), axis=1),
                              N_EXPERTS - 1)
    n_used = (pad_end[-1] // MOE_TILE).reshape(1)
    return gate, slots3, pad_end, tile_expert, n_used, n_tiles * MOE_TILE


class _Geometry:
    def __init__(self, nb_ctx, t_ctx, nb_lat, t_lat):
        self.nb_ctx, self.t_ctx, self.nb_lat, self.t_lat = nb_ctx, t_ctx, nb_lat, t_lat
        self.n_ctx = nb_ctx * t_ctx
        self.n = self.n_ctx + nb_lat * t_lat
        self.ctx_tiles = self.n_ctx // ROW_TILE
        self.lat_tiles = t_lat // ROW_TILE

    def mod_group(self, i):
        return jnp.where(i < self.ctx_tiles, 0, 1 + (i - self.ctx_tiles) // self.lat_tiles)

    def ctx_rows(self, w):
        return pl.BlockSpec((ROW_TILE, w), lambda i: (jnp.minimum(i, self.ctx_tiles - 1), 0))

    def lat_rows(self, w):
        return pl.BlockSpec((ROW_TILE, w), lambda i: (jnp.maximum(i - self.ctx_tiles, 0), 0))

    def rope_block(self, i):
        return jnp.where(i < self.ctx_tiles, self.lat_tiles, (i - self.ctx_tiles) % self.lat_tiles)


def _rope_tables(t):
    pos = jnp.arange(t)
    row = (pos // GRID_W).astype(F32)
    col = (pos % GRID_W).astype(F32)

    def cs(dim):
        q = dim // 4
        inv = ROPE_BASE ** (-jnp.arange(q, dtype=F32) / q)
        ar = row[:, None] * inv
        ac = col[:, None] * inv
        c = jnp.concatenate([jnp.cos(ar), jnp.cos(ar), jnp.cos(ac), jnp.cos(ac)], axis=-1)
        s = jnp.concatenate([-jnp.sin(ar), jnp.sin(ar), -jnp.sin(ac), jnp.sin(ac)], axis=-1)
        return c, s

    def with_identity(c, s):
        return (jnp.concatenate([c, jnp.ones((ROW_TILE, c.shape[1]), F32)], axis=0),
                jnp.concatenate([s, jnp.zeros((ROW_TILE, s.shape[1]), F32)], axis=0))

    c64, s64 = cs(RET_DK)
    rc, rs = with_identity(jnp.tile(c64, (1, RET_H)), jnp.tile(s64, (1, RET_H)))
    c32, s32 = cs(MLA_DR)
    pad = lambda a, v: jnp.concatenate([jnp.full((t, MLA_DN), v, F32), a, jnp.full((t, LANES - MLA_DN - MLA_DR), v, F32)], axis=-1)
    mc, ms = with_identity(pad(c32, 1.0), pad(s32, 0.0))
    return rc, rs, mc, ms


def _layer_params(l, norm_g, w_in, ret_decay, ret_norm, conv_w, conv_b, lru_gate_w, lru_gate_b, lru_lambda,
                  mla_q_norm, mla_kv_norm, mla_w_uq, mla_w_ukv, w_out, router_w, router_b,
                  moe_w_gu, moe_b_gu, moe_w_down, moe_b_down):
    p = {}
    p['g'] = [norm_g[l, i].reshape(1, D_MODEL) for i in range(4)]
    kr0 = 1920
    p['win'] = jnp.concatenate([w_in[l][:, :kr0], jnp.zeros((D_MODEL, MLA_DN), F32), w_in[l][:, kr0:],
                                jnp.zeros((D_MODEL, LANES - MLA_DN - MLA_DR), F32)], axis=1).astype(BF16)
    p['lg'] = jax.nn.log_sigmoid(ret_decay[l].astype(F32))
    p['retn'] = ret_norm[l].reshape(1, RET_W)
    p['qn'] = mla_q_norm[l].reshape(1, Q_RANK)
    p['kvn'] = mla_kv_norm[l].reshape(1, KV_RANK)
    wuq = mla_w_uq[l].reshape(Q_RANK, MLA_H, MLA_DN + MLA_DR)
    p['wuq'] = jnp.pad(wuq, ((0, 0), (0, 0), (0, LANES - MLA_DN - MLA_DR))).reshape(Q_RANK, MLA_H * LANES).astype(BF16)
    wukv = mla_w_ukv[l].reshape(KV_RANK, MLA_H, MLA_DN + MLA_DV)
    p['wk'] = jnp.pad(wukv[:, :, :MLA_DN], ((0, 0), (0, 0), (0, LANES - MLA_DN))).reshape(KV_RANK, MLA_H * LANES).astype(BF16)
    wv_t = jnp.pad(wukv[:, :, MLA_DN:].transpose(1, 2, 0), ((0, 0), (0, MLA_VT_ROWS - MLA_DV), (0, 0)))
    p['wv_t'] = wv_t.reshape(MLA_H * MLA_VT_ROWS, KV_RANK).astype(BF16)
    p['wout'] = w_out[l].astype(BF16)
    p['rw'] = jnp.pad(router_w[l], ((0, 0), (0, LANES - N_EXPERTS))).astype(BF16)
    p['rb'] = jnp.pad(router_b[l], (0, LANES - N_EXPERTS)).reshape(1, LANES)
    p['cw'] = conv_w[l].reshape(4, 2, LANES).transpose(1, 0, 2)
    p['cb'] = conv_b[l].reshape(2, 1, LANES)
    gw = lru_gate_w[l]
    halves = []
    for hh in range(2):
        cols = []
        for d in range(2):
            for g in range(2):
                blk = jnp.zeros((LANES, LANES), F32)
                for j in range(2):
                    blk = blk.at[j * LRU_BW:(j + 1) * LRU_BW, j * LRU_BW:(j + 1) * LRU_BW].set(gw[d, g, 2 * hh + j])
                cols.append(blk)
        halves.append(jnp.concatenate(cols, axis=1))
    p['wg'] = jnp.stack(halves).astype(BF16)
    gb = lru_gate_b[l].reshape(2, 2, 2, LANES)
    p['bg'] = gb.transpose(2, 0, 1, 3).reshape(2, 1, 4 * LANES)
    p['c8'] = (8.0 * jax.nn.log_sigmoid(lru_lambda[l].astype(F32))).reshape(2, 2, LANES).transpose(1, 0, 2)
    p['wgu'] = moe_w_gu[l].astype(BF16)
    p['bgu'] = moe_b_gu[l]
    p['wd'] = moe_w_down[l].astype(BF16)
    p['bd'] = moe_b_down[l]
    return p


def _forward(x_prompt, x_sample, c, state_ret, state_lru, cache_mla_ckv, cache_mla_krope, c_ctx, ada_w, ada_b,
             *weights):
    nb_ctx, t_ctx, _ = x_prompt.shape
    nb_lat, t_lat, _ = x_sample.shape
    geo = _Geometry(nb_ctx, t_ctx, nb_lat, t_lat)
    n_c = geo.n_ctx
    x_ctx = x_prompt.reshape(n_c, D_MODEL)
    x_lat = x_sample.reshape(-1, D_MODEL)
    cond =jnp.concatenate([c_ctx[None, :], c, jnp.zeros((16 - 1 - nb_lat, D_MODEL), F32)], axis=0)
    mod_all = _modulation(cond, ada_w, ada_b)[:, :1 + nb_lat].reshape(DEPTH, 1 + nb_lat, 6, D_MODEL)
    tabs = _rope_tables(t_lat)
    krope_pad = jnp.pad(cache_mla_krope, ((0, 0), (0, 0), (0, 0), (MLA_DN, LANES - MLA_DN - MLA_DR)))
    ret_out, lru_out, ckv_out, kr_out = [], [], [], []
    for l in range(DEPTH):
        p = _layer_params(l, *weights)
        mod = mod_all[l]
        rq, rk, rv, rg, lx, ly, q, ckvn, kr = _in_projection(x_ctx, x_lat, mod, p['g'][0], p['win'], tabs, p['qn'],
                                                             p['kvn'], p['wuq'], geo)
        zero_s = jnp.zeros((nb_ctx, 2, 2, LANES, LANES), F32)
        ofc, obc, s_ctx = _retention(rq, rk, rv, p['lg'], zero_s, nb_ctx, t_ctx, 0)
        ofl, obl, _ = _retention(rq, rk, rv, p['lg'], _state_to_pairs(state_ret[:, l]), nb_lat, t_lat, n_c)
        lru_c, h_ctx = _lru(lx, ly, jnp.zeros((nb_ctx, 2, LRU_W), F32), p['c8'], p['cw'], p['cb'], p['wg'], p['bg'],
                            nb_ctx, t_ctx, 0)
        lru_l, _ = _lru(lx, ly, state_lru[:, l], p['c8'], p['cw'], p['cb'], p['wg'], p['bg'], nb_lat, t_lat, n_c)
        mla_c = _mla(q, ckvn, kr, None, p['wk'], p['wv_t'], nb_ctx, t_ctx, 0)
        mla_l = _mla(q, ckvn, kr, (cache_mla_ckv[:, l], krope_pad[:, l]), p['wk'], p['wv_t'], nb_lat, t_lat, n_c)
        x1, h2, route, counts = _out_projection((ofc, obc, lru_c, mla_c, x_ctx), (ofl, obl, lru_l, mla_l, x_lat), rg,
                                                mod, p['retn'], p['g'][1], p['g'][2], p['wout'], p['rw'], p['rb'], geo)
        gate, slots3, pad_end, tile_expert, n_used, n_slots = _plan(route, counts)
        xs = _dispatch(h2, slots3, pad_end, n_slots)
        y_sorted = _moe(xs, tile_expert, n_used, p['wgu'], p['bgu'], p['wd'], p['bd'])
        x_ctx, x_lat = _combine(y_sorted, slots3, gate, x1, mod, p['g'][3], geo)
        ret_out.append(_pairs_to_state(s_ctx))
        lru_out.append(h_ctx)
        ckv_out.append(ckvn[:n_c].reshape(nb_ctx, t_ctx, KV_RANK))
        kr_out.append(kr[:n_c, MLA_DN:MLA_DN + MLA_DR].reshape(nb_ctx, t_ctx, MLA_DR))
    y_prompt = x_ctx.reshape(nb_ctx, t_ctx, D_MODEL)
    y_sample = x_lat.reshape(nb_lat, t_lat, D_MODEL)
    return (y_prompt, y_sample, jnp.stack(ret_out, axis=1), jnp.stack(lru_out, axis=1),
            jnp.stack(ckv_out, axis=1), jnp.stack(kr_out, axis=1))


def kernel(x_prompt, x_sample, c, state_ret, state_lru, cache_mla_ckv, cache_mla_krope, c_ctx, ada_w, ada_b, norm_g, w_in, ret_decay, ret_norm, conv_w, conv_b, lru_gate_w, lru_gate_b, lru_lambda, mla_q_norm, mla_kv_norm, mla_w_uq, mla_w_ukv, w_out, router_w, router_b, moe_w_gu, moe_b_gu, moe_w_down, moe_b_down):
    return _forward(x_prompt, x_sample, c, state_ret, state_lru, cache_mla_ckv, cache_mla_krope, c_ctx, ada_w, ada_b,
                    norm_g, w_in, ret_decay, ret_norm, conv_w, conv_b, lru_gate_w, lru_gate_b, lru_lambda,
                    mla_q_norm, mla_kv_norm, mla_w_uq, mla_w_ukv, w_out, router_w, router_b,
                    moe_w_gu, moe_b_gu, moe_w_down, moe_b_down)
```

```python
import functools
import math

import jax
import jax.numpy as jnp
from jax import lax
from jax.experimental import pallas as pl
from jax.experimental.pallas import tpu as pltpu

F32 = jnp.float32
BF16 = jnp.bfloat16

D_MODEL = 1024
DEPTH = 2
GRID_W = 64
RET_H = 4
RET_DK = 64
RET_W = 256
RET_CHUNK = 128
LRU_W = 256
LRU_BLOCKS = 4
LRU_BW = 64
MLA_H = 8
MLA_DN = 64
MLA_DR = 32
MLA_DV = 64
MLA_W = MLA_H * MLA_DV
MLA_VT_ROWS = 80
Q_RANK = 256
KV_RANK = 128
ROPE_BASE = 10000.0
N_EXPERTS = 32
TOP_K = 4
D_EXPERT = 1024
SWIGLU_ALPHA = 1.702
SWIGLU_LIMIT = 7.0
EPS = 1e-6

LANES = 128
ROW_TILE = 256
IN_TILE = 512
MOE_TILE = 256
MOE_HCHUNK = 256
LRU_ROWS = 256
ATT_TQ = 256
ATT_KC = 256
ATT_AHEAD = 4
KV_BUILD_ROWS = 512
PROJ_PAD = 2048
VMEM_LIMIT = 56 * 1024 * 1024


def _cparams(sem):
    return pltpu.CompilerParams(dimension_semantics=sem, vmem_limit_bytes=VMEM_LIMIT)


def _rms(x, g):
    return x * lax.rsqrt(jnp.mean(x * x, axis=-1, keepdims=True) + EPS) * g


def _rope(x, c, s, quarter):
    w = x.shape[-1]
    lane = lax.broadcasted_iota(jnp.int32, x.shape, 1)
    first = (lane % (2 * quarter)) < quarter
    partner = jnp.where(first, pltpu.roll(x, w - quarter, 1), pltpu.roll(x, quarter, 1))
    return x * c + partner * s


def _mod_kernel(cond_ref, w_ref, b_ref, o_ref):
    c = cond_ref[...]
    s = (c * jax.nn.sigmoid(c)).astype(BF16)
    o_ref[0] = jnp.dot(s, w_ref[0].astype(BF16), preferred_element_type=F32) + b_ref[0]


def _modulation(cond, ada_w, ada_b):
    nblk = 6
    return pl.pallas_call(
        _mod_kernel,
        grid=(DEPTH, nblk),
        in_specs=[
            pl.BlockSpec((16, D_MODEL), lambda l, j: (0, 0)),
            pl.BlockSpec((1, D_MODEL, D_MODEL), lambda l, j: (l, 0, j)),
            pl.BlockSpec((1, 1, D_MODEL), lambda l, j: (l, 0, j)),
        ],
        out_specs=pl.BlockSpec((1, 16, D_MODEL), lambda l, j: (l, 0, j)),
        out_shape=jax.ShapeDtypeStruct((DEPTH, 16, 6 * D_MODEL), F32),
        compiler_params=_cparams(("arbitrary", "arbitrary")),
        name="adaln_mod",
    )(cond, ada_w, ada_b.reshape(DEPTH, 1, 6 * D_MODEL))


def _inproj_kernel(xc_ref, xl_ref, mod_ref, g0_ref, win_ref, rc_ref, rs_ref, mc_ref, ms_ref, qn_ref, kvn_ref, wuq_ref,
                   rq_ref, rk_ref, rv_ref, rg_ref, lx_ref, ly_ref, q_ref, ckv_ref, kr_ref, *, ctx_tiles):
    x = jnp.where(pl.program_id(0) < ctx_tiles, xc_ref[...], xl_ref[...])
    mod = mod_ref[0]
    h = _rms(x, g0_ref[...]) * (1.0 + mod[1:2]) + mod[0:1]
    hb = h.astype(BF16)
    z_mla = jnp.dot(hb, win_ref[:, 1536:2048], preferred_element_type=F32)
    z_qk = jnp.dot(hb, win_ref[:, 0:512], preferred_element_type=F32)
    cqn = _rms(z_mla[:, 0:256], qn_ref[...])
    q = jnp.dot(cqn.astype(BF16), wuq_ref[...], preferred_element_type=F32)
    z_rest = jnp.dot(hb, win_ref[:, 512:1536], preferred_element_type=F32)
    rc = rc_ref[...]
    rs = rs_ref[...]
    rq_ref[...] = _rope(z_qk[:, 0:256], rc, rs, 16)
    rk_ref[...] = _rope(z_qk[:, 256:512] * (RET_DK ** -0.5), rc, rs, 16)
    mc = mc_ref[...]
    ms = ms_ref[...]
    ckv_ref[...] = _rms(z_mla[:, 256:384], kvn_ref[...])
    kr_ref[...] = _rope(z_mla[:, 384:512], mc, ms, 8)
    for h_i in range(MLA_H):
        sl = slice(h_i * LANES, (h_i + 1) * LANES)
        q_ref[:, sl] = _rope(q[:, sl], mc, ms, 8).astype(BF16)
    rv_ref[...] = z_rest[:, 0:256]
    rg_ref[...] = z_rest[:, 256:512]
    lx_ref[...] = z_rest[:, 512:768]
    ly_ref[...] = z_rest[:, 768:1024]


def _in_projection(x_ctx, x_lat, mod, g0, win_p, tabs, qn, kvn, wuq_p, geo):
    n = geo.n
    nt = n // geo.tile
    rc, rs, mc, ms = tabs
    row = lambda w: pl.BlockSpec((geo.tile, w), lambda i: (i, 0))
    full = lambda a: pl.BlockSpec(a.shape, lambda i: (0,) * a.ndim)
    tab = lambda w: pl.BlockSpec((geo.tile, w), lambda i: (geo.rope_block(i), 0))
    outs = [(256, F32)] * 6 + [(MLA_H * LANES, BF16), (LANES, F32), (LANES, F32)]
    return pl.pallas_call(
        functools.partial(_inproj_kernel, ctx_tiles=geo.ctx_tiles),
        grid=(nt,),
        in_specs=[geo.ctx_rows(D_MODEL), geo.lat_rows(D_MODEL),
                  pl.BlockSpec((1, 6, D_MODEL), lambda i: (geo.mod_group(i), 0, 0)),
                  full(g0), full(win_p), tab(256), tab(256), tab(LANES), tab(LANES), full(qn), full(kvn), full(wuq_p)],
        out_specs=[row(w) for w, _ in outs],
        out_shape=[jax.ShapeDtypeStruct((n, w), dt) for w, dt in outs],
        compiler_params=_cparams(("arbitrary",)),
        name="in_projection",
    )(x_ctx, x_lat, mod, g0, win_p, rc, rs, mc, ms, qn, kvn, wuq_p)


def _ret_kernel(lg_ref, qf_ref, kf_ref, vf_ref, qb_ref, kb_ref, vb_ref, s0_ref,
                of_ref, ob_ref, sfin_ref, s_scr, intra_scr, cross_scr, into_scr, carry_scr):
    j = pl.program_id(1)
    c = RET_CHUNK
    row = lax.broadcasted_iota(jnp.int32, (c, c), 0)
    lane = lax.broadcasted_iota(jnp.int32, (c, c), 1)
    rowf = row.astype(F32)
    lanef = lane.astype(F32)

    @pl.when(j == 0)
    def _init():
        s_scr[...] = s0_ref[0]
        for d in range(2):
            for h in range(RET_H):
                lg = lg_ref[d, h]
                if d == 0:
                    keep = row >= lane
                    dist = rowf - lanef
                else:
                    keep = lane >= row
                    dist = lanef - rowf
                intra_scr[d, h] = jnp.where(keep, jnp.exp(jnp.where(keep, dist, 0.0) * lg), 0.0)
            for p in range(2):
                lgl = jnp.where(lane < RET_DK, lg_ref[d, 2 * p], lg_ref[d, 2 * p + 1])
                if d == 0:
                    cross_scr[d, p] = jnp.exp((rowf + 1.0) * lgl)
                    into_scr[d, p] = jnp.exp((c - 1.0 - rowf) * lgl)
                else:
                    cross_scr[d, p] = jnp.exp((c - rowf) * lgl)
                    into_scr[d, p] = jnp.exp(rowf * lgl)
                carry_scr[d, p] = jnp.exp(float(c) * lgl)

    same_head = (row < RET_DK) == (lane < RET_DK)

    chains = [(d, p, refs) for d, refs in ((0, (qf_ref, kf_ref, vf_ref, of_ref)), (1, (qb_ref, kb_ref, vb_ref, ob_ref)))
              for p in range(2)]
    staged = []
    for d, p, (q_ref, k_ref, v_ref, _) in chains:
        sl = slice(p * LANES, (p + 1) * LANES)
        q2b = q_ref[:, sl].astype(BF16)
        k2 = k_ref[:, sl]
        v2 = v_ref[:, sl]
        scores, vals = [], []
        for e in range(2):
            sel = (lane >= RET_DK) if e else (lane < RET_DK)
            ke = jnp.where(sel, k2, 0.0).astype(BF16)
            vals.append(jnp.where(sel, v2, 0.0).astype(BF16))
            scores.append(lax.dot_general(q2b, ke, (((1,), (1,)), ((), ())), preferred_element_type=F32))
        st = s_scr[d, p]
        from_state = jnp.dot(q2b, st.astype(BF16), preferred_element_type=F32)
        kw = (k2 * into_scr[d, p]).astype(BF16)
        upd = lax.dot_general(kw, v2.astype(BF16), (((0,), (0,)), ((), ())), preferred_element_type=F32)
        staged.append((scores, vals, st, from_state, upd))
    for (d, p, (_, _, _, o_ref)), (scores, vals, st, from_state, upd) in zip(chains, staged):
        o = from_state * cross_scr[d, p]
        for e in range(2):
            a = (scores[e] * intra_scr[d, 2 * p + e]).astype(BF16)
            o = o + jnp.dot(a, vals[e], preferred_element_type=F32)
        s_scr[d, p] = st * carry_scr[d, p] + jnp.where(same_head, upd, 0.0)
        o_ref[:, p * LANES:(p + 1) * LANES] = o

    @pl.when(j == pl.num_programs(1) - 1)
    def _fin():
        sfin_ref[0] = s_scr[...]


def _retention(rq, rk, rv, lg, s0, nb, t, row0):
    n = nb * t
    nc = t // RET_CHUNK
    base = row0 // RET_CHUNK
    fwd = lambda off: pl.BlockSpec((RET_CHUNK, RET_W), lambda b, j: (off + b * nc + j, 0))
    bwd = lambda off: pl.BlockSpec((RET_CHUNK, RET_W), lambda b, j: (off + b * nc + nc - 1 - j, 0))
    st = pl.BlockSpec((1, 2, 2, LANES, LANES), lambda b, j: (b, 0, 0, 0, 0))
    return pl.pallas_call(
        _ret_kernel,
        grid=(nb, nc),
        in_specs=[pl.BlockSpec(memory_space=pltpu.SMEM), fwd(base), fwd(base), fwd(base), bwd(base), bwd(base),
                  bwd(base), st],
        out_specs=[fwd(0), bwd(0), st],
        out_shape=[jax.ShapeDtypeStruct((n, RET_W), F32), jax.ShapeDtypeStruct((n, RET_W), F32),
                   jax.ShapeDtypeStruct((nb, 2, 2, LANES, LANES), F32)],
        scratch_shapes=[pltpu.VMEM((2, 2, LANES, LANES), F32), pltpu.VMEM((2, RET_H, RET_CHUNK, RET_CHUNK), F32),
                        pltpu.VMEM((2, 2, RET_CHUNK, LANES), F32), pltpu.VMEM((2, 2, RET_CHUNK, LANES), F32),
                        pltpu.VMEM((2, 2, RET_CHUNK, LANES), F32)],
        compiler_params=_cparams(("arbitrary", "arbitrary")),
        name="retention",
    )(lg, rq, rk, rv, rq, rk, rv, s0)


def _state_to_pairs(s):
    b = s.shape[0]
    s = s.reshape(b, 2, 2, 2, RET_DK, RET_DK)
    z = jnp.zeros_like(s[:, :, :, 0])
    top = jnp.concatenate([s[:, :, :, 0], z], axis=-1)
    bot = jnp.concatenate([z, s[:, :, :, 1]], axis=-1)
    return jnp.concatenate([top, bot], axis=-2)


def _pairs_to_state(s):
    b = s.shape[0]
    a = s[..., :RET_DK, :RET_DK]
    c = s[..., RET_DK:, RET_DK:]
    return jnp.stack([a, c], axis=3).reshape(b, 2, RET_H, RET_DK, RET_DK)


def _scan_rows(a, b, reverse):
    r = a.shape[0]
    rows = lax.broadcasted_iota(jnp.int32, a.shape, 0)
    s = 1
    while s < r:
        if reverse:
            a_s = pltpu.roll(a, r - s, 0)
            b_s = pltpu.roll(b, r - s, 0)
            m = rows < r - s
        else:
            a_s = pltpu.roll(a, s, 0)
            b_s = pltpu.roll(b, s, 0)
            m = rows >= s
        b = jnp.where(m, a * b_s + b, b)
        a = jnp.where(m, a * a_s, a)
        s *= 2
    return a, b


def _lru_kernel(c8_ref, cw_ref, cb_ref, wg_ref, bg_ref, lx_ref, ly_ref, h0_ref, o_ref, hfin_ref, xc_scr, hf_scr,
                *, t, r):
    nch = t // r
    cw = cw_ref[0]
    cb = cb_ref[0]
    wg = wg_ref[0]
    bg = bg_ref[0]
    c8 = c8_ref[0]

    def conv_body(c, carry):
        r0 = pl.multiple_of(c * r, r)
        cur = lx_ref[pl.ds(r0, r), :]
        prev = lx_ref[pl.ds(pl.multiple_of(jnp.maximum(r0 - 8, 0), 8), 8), :]
        nxt = lx_ref[pl.ds(pl.multiple_of(jnp.minimum(r0 + r, t - 8), 8), 8), :]
        prev = jnp.where(c > 0, prev, 0.0)
        nxt = jnp.where(c < nch - 1, nxt, 0.0)
        ext = jnp.concatenate([prev, cur, nxt], axis=0)
        acc = jnp.broadcast_to(cb, (r, LANES))
        for tap in range(4):
            sh = (2 - tap) % (r + 16)
            xs = ext if sh == 0 else pltpu.roll(ext, sh, 0)
            acc = acc + xs[8:8 + r] * cw[tap:tap + 1]
        xc_scr[pl.ds(r0, r), :] = acc
        return carry

    lax.fori_loop(0, nch, conv_body, 0)

    def gates(xc, d):
        g = jnp.dot(xc.astype(BF16), wg[:, d * 256:(d + 1) * 256], preferred_element_type=F32) + bg[:, d * 256:(d + 1) * 256]
        rg = jax.nn.sigmoid(g[:, :LANES])
        ig = jax.nn.sigmoid(g[:, LANES:])
        log_a = c8[d:d + 1] * rg
        a = jnp.exp(log_a)
        b = jnp.sqrt(jnp.tanh(-log_a) * (a * a + 1.0)) * (ig * xc)
        return a, b

    def fwd_body(c, h):
        r0 = pl.multiple_of(c * r, r)
        a, b = gates(xc_scr[pl.ds(r0, r), :], 0)
        a, b = _scan_rows(a, b, False)
        hc = a * h + b
        hf_scr[pl.ds(r0, r), :] = hc
        return hc[r - 1:r]

    h_f = lax.fori_loop(0, nch, fwd_body, h0_ref[0, 0:1, :])

    def bwd_body(i, h):
        r0 = pl.multiple_of((nch - 1 - i) * r, r)
        a, b = gates(xc_scr[pl.ds(r0, r), :], 1)
        a, b = _scan_rows(a, b, True)
        hc = a * h + b
        o_ref[pl.ds(r0, r), :] = ((hf_scr[pl.ds(r0, r), :] + hc) * jax.nn.gelu(ly_ref[pl.ds(r0, r), :])).astype(BF16)
        return hc[0:1]

    h_b = lax.fori_loop(0, nch, bwd_body, h0_ref[0, 1:2, :])
    hfin_ref[0, 0:1, :] = h_f
    hfin_ref[0, 1:2, :] = h_b


def _lru(lx, ly, h0, c8, cw, cb, wg, bg, nb, t, row0):
    base = row0 // t
    r = min(LRU_ROWS, t)
    seq = lambda off: pl.BlockSpec((t, LANES), lambda b, hh: (off + b, hh))
    par = lambda a: pl.BlockSpec((1,) + a.shape[1:], lambda b, hh: (hh,) + (0,) * (a.ndim - 1))
    st = pl.BlockSpec((1, 2, LANES), lambda b, hh: (b, 0, hh))
    return pl.pallas_call(
        functools.partial(_lru_kernel, t=t, r=r),
        grid=(nb, 2),
        in_specs=[par(c8), par(cw), par(cb), par(wg), par(bg), seq(base), seq(base), st],
        out_specs=[seq(0), st],
        out_shape=[jax.ShapeDtypeStruct((nb * t, LRU_W), BF16), jax.ShapeDtypeStruct((nb, 2, LRU_W), F32)],
        scratch_shapes=[pltpu.VMEM((t, LANES), F32), pltpu.VMEM((t, LANES), F32)],
        compiler_params=_cparams(("arbitrary", "arbitrary")),
        name="rg_lru",
    )(c8, cw, cb, wg, bg, lx, ly, h0)


def _mla_kernel(*refs, n_ctx, t, tq):
    if n_ctx:
        q_ref, ckv_ref, kr_ref, cckv_ref, ckr_ref, wk_ref, wv_ref = refs[:7]
    else:
        q_ref, ckv_ref, kr_ref, wk_ref, wv_ref = refs[:5]
    o_ref, k_scr, vt_scr, qt_scr, ot_scr = refs[-5:]
    qi = pl.program_id(1)
    cexp = (MLA_DN + MLA_DR) ** -0.5 * math.log2(math.e)
    ones_row = (lax.broadcasted_iota(jnp.int32, (MLA_H * MLA_VT_ROWS, 1), 0) % MLA_VT_ROWS == MLA_DV).astype(F32)

    def put(ckv, kr, c0):
        nrow = ckv.shape[0]
        ckv_b = ckv.astype(BF16)
        kn = jnp.dot(ckv_b, wk_ref[...], preferred_element_type=F32)
        vt = lax.dot_general(wv_ref[...], ckv_b, (((1,), (1,)), ((), ())), preferred_element_type=F32) + ones_row
        for h in range(MLA_H):
            k_scr[h, c0:c0 + nrow, :] = ((kn[:, h * LANES:(h + 1) * LANES] + kr) * cexp).astype(BF16)
            vt_scr[h, :, c0:c0 + nrow] = vt[h * MLA_VT_ROWS:(h + 1) * MLA_VT_ROWS].astype(BF16)

    @pl.when(qi == 0)
    def _build():
        if n_ctx:
            put(cckv_ref[0], ckr_ref[0], 0)
        step = min(KV_BUILD_ROWS, t)
        for c in range(t // step):
            put(ckv_ref[c * step:(c + 1) * step, :], kr_ref[c * step:(c + 1) * step, :], n_ctx + c * step)

    qt_scr[...] = q_ref[...].astype(F32).T.astype(BF16)
    s_len = n_ctx + t

    nchunk = s_len // ATT_KC
    steps = [(h, c) for h in range(MLA_H) for c in range(nchunk)]

    def score(i):
        h, c = steps[i]
        return jnp.dot(k_scr[h, c * ATT_KC:(c + 1) * ATT_KC, :], qt_scr[h * LANES:(h + 1) * LANES, :],
                       preferred_element_type=F32)

    pending = {i: score(i) for i in range(min(ATT_AHEAD, len(steps)))}
    held = None
    m = o = None
    for i in range(len(steps) + 1):
        if i + ATT_AHEAD < len(steps):
            pending[i + ATT_AHEAD] = score(i + ATT_AHEAD)
        if i < len(steps):
            h, c = steps[i]
            s = pending.pop(i)
            m_old = jnp.full((1, tq), -1e30, F32) if c == 0 else m
            m = jnp.maximum(m_old, jnp.max(s, axis=0, keepdims=True))
            p = jnp.exp2(s - m).astype(BF16)
            alpha = jnp.exp2(m_old - m)
        if held is not None:
            hh, cc, p_h, alpha_h = held
            pv = jnp.dot(vt_scr[hh, :, cc * ATT_KC:(cc + 1) * ATT_KC], p_h, preferred_element_type=F32)
            o = pv if cc == 0 else o * alpha_h + pv
            if cc == nchunk - 1:
                ot_scr[hh] = o[:MLA_DV] / o[MLA_DV:MLA_DV + 1]
        held = (h, c, p, alpha) if i < len(steps) else None
    for pp in range(MLA_H // 2):
        pair = jnp.concatenate([ot_scr[2 * pp], ot_scr[2 * pp + 1]], axis=0)
        o_ref[:, pp * LANES:(pp + 1) * LANES] = pair.T.astype(BF16)


def _mla(q, ckvn, kr, cache, wk, wv_t, nb, t, row0):
    tq = min(ATT_TQ, t)
    nq = t // tq
    n_ctx = 0 if cache is None else cache[0].shape[1]
    s_len = n_ctx + t
    qspec = pl.BlockSpec((tq, MLA_H * LANES), lambda b, i: (row0 // tq + b * nq + i, 0))
    seq = pl.BlockSpec((t, LANES), lambda b, i: (row0 // t + b, 0))
    full = lambda a: pl.BlockSpec(a.shape, lambda b, i: (0,) * a.ndim)
    ins = [q, ckvn, kr]
    specs = [qspec, seq, seq]
    if n_ctx:
        cspec = pl.BlockSpec((1, n_ctx, LANES), lambda b, i: (b, 0, 0))
        ins += [cache[0], cache[1]]
        specs += [cspec, cspec]
    ins += [wk, wv_t]
    specs += [full(wk), full(wv_t)]
    return pl.pallas_call(
        functools.partial(_mla_kernel, n_ctx=n_ctx, t=t, tq=tq),
        grid=(nb, nq),
        in_specs=specs,
        out_specs=pl.BlockSpec((tq, MLA_W), lambda b, i: (b * nq + i, 0)),
        out_shape=jax.ShapeDtypeStruct((nb * t, MLA_W), BF16),
        scratch_shapes=[pltpu.VMEM((MLA_H, s_len, LANES), BF16), pltpu.VMEM((MLA_H, MLA_VT_ROWS, s_len), BF16),
                        pltpu.VMEM((MLA_H * LANES, tq), BF16), pltpu.VMEM((MLA_H, MLA_DV, tq), F32)],
        compiler_params=_cparams(("arbitrary", "arbitrary")),
        name="mla_attention",
    )(*ins)


def _outproj_kernel(ofc_ref, obc_ref, lruc_ref, mlac_ref, xc_ref, ofl_ref, obl_ref, lrul_ref, mlal_ref, xl_ref, rg_ref,
                    mod_ref, retn_ref, g1_ref, g2_ref, wout_ref, rw_ref, rb_ref, x1_ref, h2_ref, route_ref, cnt_ref,
                    cnt_scr, *, ctx_tiles):
    is_ctx = pl.program_id(0) < ctx_tiles
    pick = lambda c_ref, l_ref: jnp.where(is_ctx, c_ref[...], l_ref[...])
    o = pick(ofc_ref, ofl_ref) + pick(obc_ref, obl_ref)
    hid = lax.broadcasted_iota(jnp.int32, o.shape, 1) // RET_DK

    def head_sum(v):
        tot = jnp.zeros_like(v)
        for hh in range(RET_H):
            msk = hid == hh
            tot = jnp.where(msk, jnp.sum(jnp.where(msk, v, 0.0), axis=1, keepdims=True), tot)
        return tot

    mu = head_sum(o) * (1.0 / RET_DK)
    dl = o - mu
    var = head_sum(dl * dl) * (1.0 / RET_DK)
    rg = rg_ref[...]
    ret = dl * lax.rsqrt(var + EPS) * retn_ref[...] * (rg * jax.nn.sigmoid(rg))
    mix = jnp.concatenate([ret.astype(BF16), pick(lruc_ref, lrul_ref), pick(mlac_ref, mlal_ref)], axis=1)
    mo = jnp.dot(mix, wout_ref[...], preferred_element_type=F32)
    mod = mod_ref[0]
    x1 = pick(xc_ref, xl_ref) + mod[2:3] * _rms(mo, g1_ref[...])
    x1_ref[...] = x1
    h2 = _rms(x1, g2_ref[...]) * (1.0 + mod[4:5]) + mod[3:4]
    h2_ref[...] = h2

    tm = h2.shape[0]
    lg = jnp.dot(h2.astype(BF16), rw_ref[...], preferred_element_type=F32) + rb_ref[...]
    lgt = lg.T[:N_EXPERTS, :]
    rowf = lax.broadcasted_iota(jnp.int32, (N_EXPERTS, tm), 0).astype(F32)
    tops, idxs, hots = [], [], []
    for _ in range(TOP_K):
        m = jnp.max(lgt, axis=0, keepdims=True)
        idx = jnp.min(jnp.where(lgt == m, rowf, float(N_EXPERTS)), axis=0, keepdims=True)
        hot = rowf == idx
        lgt = jnp.where(hot, -jnp.inf, lgt)
        tops.append(m)
        idxs.append(idx)
        hots.append(hot)
    exps = [jnp.exp(t - tops[0]) for t in tops]
    den = exps[0] + exps[1] + exps[2] + exps[3]
    member = jnp.zeros((N_EXPERTS, tm), F32)
    for hot in hots:
        member = member + hot.astype(F32)

    @pl.when(pl.program_id(0) == 0)
    def _zero_counts():
        cnt_scr[...] = jnp.zeros_like(cnt_scr)

    earlier = (lax.broadcasted_iota(jnp.int32, (tm, tm), 0) < lax.broadcasted_iota(jnp.int32, (tm, tm), 1))
    counts = cnt_scr[...]
    before = jnp.dot(member.astype(BF16), earlier.astype(BF16), preferred_element_type=F32) + counts[:, 0:1]
    ranks = [jnp.sum(jnp.where(hot, before, 0.0), axis=0, keepdims=True) for hot in hots]
    gates = [e / den for e in exps]
    route_ref[0] = jnp.concatenate(idxs + gates + ranks + [jnp.zeros((TOP_K, tm), F32)], axis=0)
    counts = counts + jnp.sum(member, axis=1, keepdims=True)
    cnt_scr[...] = counts
    cnt_ref[...] = counts


def _out_projection(ctx_mix, lat_mix, rg, mod, retn, g1, g2, wout, rw, rb, geo):
    n = geo.n
    row = lambda w: pl.BlockSpec((ROW_TILE, w), lambda i: (i, 0))
    full = lambda a: pl.BlockSpec(a.shape, lambda i: (0,) * a.ndim)
    widths = (256, 256, 256, MLA_W, D_MODEL)
    return pl.pallas_call(
        functools.partial(_outproj_kernel, ctx_tiles=geo.ctx_tiles),
        grid=(n // ROW_TILE,),
        in_specs=[geo.ctx_rows(w) for w in widths] + [geo.lat_rows(w) for w in widths] + [
                  row(256),
                  pl.BlockSpec((1, 6, D_MODEL), lambda i: (geo.mod_group(i), 0, 0)),
                  full(retn), full(g1), full(g2), full(wout), full(rw), full(rb)],
        out_specs=[row(D_MODEL), row(D_MODEL), pl.BlockSpec((1, 4 * TOP_K, ROW_TILE), lambda i: (i, 0, 0)),
                   pl.BlockSpec((N_EXPERTS, LANES), lambda i: (0, 0))],
        out_shape=[jax.ShapeDtypeStruct((n, D_MODEL), F32), jax.ShapeDtypeStruct((n, D_MODEL), F32),
                   jax.ShapeDtypeStruct((n // ROW_TILE, 4 * TOP_K, ROW_TILE), F32),
                   jax.ShapeDtypeStruct((N_EXPERTS, LANES), F32)],
        scratch_shapes=[pltpu.VMEM((N_EXPERTS, LANES), F32)],
        compiler_params=_cparams(("arbitrary",)),
        name="out_projection",
    )(*ctx_mix, *lat_mix, rg, mod, retn, g1, g2, wout, rw, rb)


def _row_copy(src_hbm, src_row, dst, dst_row, sem):
    return pltpu.make_async_copy(src_hbm.at[pl.ds(src_row, 1), :], dst.at[pl.ds(dst_row, 1), :], sem)


def _dispatch_kernel(pe_ref, slot_ref, h_ref, xs_hbm, zero_scr, sem):
    i = pl.program_id(0)
    n_slots = xs_hbm.shape[0]

    @pl.when(i == 0)
    def _zero_pads():
        zero_scr[...] = jnp.zeros_like(zero_scr)

        def fill(e):
            end = pe_ref[e]
            start = 0 if e == 0 else pe_ref[e - 1]
            dst = xs_hbm.at[pl.ds(pl.multiple_of(jnp.maximum(end - MOE_TILE, 0), MOE_TILE), MOE_TILE), :]
            return end > start, pltpu.make_async_copy(zero_scr, dst, sem)

        def fill_tail(j):
            row = pe_ref[N_EXPERTS - 1] + j * MOE_TILE
            dst = xs_hbm.at[pl.ds(pl.multiple_of(jnp.minimum(row, n_slots - MOE_TILE), MOE_TILE), MOE_TILE), :]
            return row < n_slots, pltpu.make_async_copy(zero_scr, dst, sem)

        for e in range(N_EXPERTS):
            for todo, cp in (fill(e), fill_tail(e)):
                pl.when(todo)(cp.start)
        for e in range(N_EXPERTS):
            for todo, cp in (fill(e), fill_tail(e)):
                pl.when(todo)(cp.wait)

    for r in range(ROW_TILE):
        for k in range(TOP_K):
            dst = xs_hbm.at[pl.ds(slot_ref[0, 0, k * ROW_TILE + r], 1), :]
            pltpu.make_async_copy(h_ref.at[pl.ds(r, 1), :], dst, sem).start(priority=k % 2)
    for k in range(TOP_K):
        pltpu.make_async_copy(h_ref, xs_hbm.at[pl.ds(0, ROW_TILE), :], sem).wait()


def _dispatch(h2, slots3, pad_end, n_slots):
    n = h2.shape[0]
    grid_spec = pltpu.PrefetchScalarGridSpec(
        num_scalar_prefetch=1,
        grid=(n // ROW_TILE,),
        in_specs=[
            pl.BlockSpec((1, 1, TOP_K * ROW_TILE), lambda i, pe: (i, 0, 0), memory_space=pltpu.SMEM),
            pl.BlockSpec((ROW_TILE, D_MODEL), lambda i, pe: (i, 0)),
        ],
        out_specs=pl.BlockSpec(memory_space=pl.ANY),
        scratch_shapes=[pltpu.VMEM((MOE_TILE, D_MODEL), F32), pltpu.SemaphoreType.DMA(())],
    )
    return pl.pallas_call(
        _dispatch_kernel,
        grid_spec=grid_spec,
        out_shape=jax.ShapeDtypeStruct((n_slots, D_MODEL), F32),
        compiler_params=_cparams(("arbitrary",)),
        name="moe_dispatch",
    )(pad_end, slots3, h2)


def _moe_kernel(te_ref, nu_ref, x_ref, wgu_ref, bgu_ref, wd_ref, bd_ref, y_ref, wgu_b, wd_b):
    t = pl.program_id(0)
    n_used = nu_ref[0]
    nch = D_EXPERT // MOE_HCHUNK
    new_expert = jnp.logical_or(t == 0, te_ref[t] != te_ref[jnp.maximum(t - 1, 0)])

    @pl.when(jnp.logical_and(new_expert, t < n_used))
    def _cast_weights():
        for j in range(nch):
            cols = slice(j * MOE_HCHUNK, (j + 1) * MOE_HCHUNK)
            ucols = slice(D_EXPERT + j * MOE_HCHUNK, D_EXPERT + (j + 1) * MOE_HCHUNK)
            wgu_b[j, :, :MOE_HCHUNK] = wgu_ref[0, 0, :, cols].astype(BF16)
            wgu_b[j, :, MOE_HCHUNK:] = wgu_ref[0, 0, :, ucols].astype(BF16)
        wd_b[...] = wd_ref[0, 0].astype(BF16)

    @pl.when(t < n_used)
    def _compute():
        x = x_ref[...].astype(BF16)
        bgu = bgu_ref[0, 0]

        def gate_up(j):
            return jnp.dot(x, wgu_b[j], preferred_element_type=F32)

        pending = gate_up(0)
        y = None
        for j in range(nch):
            gu = pending
            if j + 1 < nch:
                pending = gate_up(j + 1)
            cols = slice(j * MOE_HCHUNK, (j + 1) * MOE_HCHUNK)
            ucols = slice(D_EXPERT + j * MOE_HCHUNK, D_EXPERT + (j + 1) * MOE_HCHUNK)
            g = jnp.minimum(gu[:, :MOE_HCHUNK] + bgu[:, cols], SWIGLU_LIMIT)
            u = jnp.clip(gu[:, MOE_HCHUNK:] + bgu[:, ucols], -SWIGLU_LIMIT, SWIGLU_LIMIT)
            act = (g * jax.nn.sigmoid(SWIGLU_ALPHA * g) * (u + 1.0)).astype(BF16)
            part = jnp.dot(act, wd_b[cols, :], preferred_element_type=F32)
            y = part if y is None else y + part
        y_ref[...] = y + bd_ref[0, 0]

    @pl.when(t >= n_used)
    def _idle():
        y_ref[...] = jnp.zeros_like(y_ref)


def _moe(xs, tile_expert, n_used, layer, wgu, bgu, wd, bd):
    n_tiles = tile_expert.shape[0]
    wspec = lambda r, c: pl.BlockSpec((1, 1, r, c), lambda t, te, nu: (layer, te[t], 0, 0))
    grid_spec = pltpu.PrefetchScalarGridSpec(
        num_scalar_prefetch=2,
        grid=(n_tiles,),
        in_specs=[
            pl.BlockSpec((MOE_TILE, D_MODEL), lambda t, te, nu: (jnp.minimum(t, jnp.maximum(nu[0] - 1, 0)), 0)),
            wspec(D_MODEL, 2 * D_EXPERT), wspec(1, 2 * D_EXPERT), wspec(D_EXPERT, D_MODEL), wspec(1, D_MODEL),
        ],
        out_specs=pl.BlockSpec((MOE_TILE, D_MODEL), lambda t, te, nu: (t, 0)),
        scratch_shapes=[pltpu.VMEM((D_EXPERT // MOE_HCHUNK, D_MODEL, 2 * MOE_HCHUNK), BF16),
                        pltpu.VMEM((D_EXPERT, D_MODEL), BF16)],
    )
    return pl.pallas_call(
        _moe_kernel,
        grid_spec=grid_spec,
        out_shape=jax.ShapeDtypeStruct((n_tiles * MOE_TILE, D_MODEL), F32),
        compiler_params=_cparams(("arbitrary",)),
        name="moe_experts",
    )(tile_expert, n_used, xs, wgu, bgu.reshape(DEPTH, N_EXPERTS, 1, -1), wd, bd.reshape(DEPTH, N_EXPERTS, 1, -1))


def _combine_kernel(inv_ref, y_hbm, gate_ref, x1_ref, mod_ref, g3_ref, oc_ref, ol_ref, buf, sem, *, ctx_tiles):
    for r in range(ROW_TILE):
        for k in range(TOP_K):
            _row_copy(y_hbm, inv_ref[0, 0, k * ROW_TILE + r], buf.at[k], r, sem).start(priority=k % 2)
    for k in range(TOP_K):
        pltpu.make_async_copy(y_hbm.at[pl.ds(0, ROW_TILE), :], buf.at[k], sem).wait()
    gate = gate_ref[...]
    ff = gate[:, 0:1] * buf[0]
    for k in range(1, TOP_K):
        ff = ff + gate[:, k:k + 1] * buf[k]
    out = x1_ref[...] + mod_ref[0][5:6] * _rms(ff, g3_ref[...])
    is_ctx = pl.program_id(0) < ctx_tiles

    @pl.when(is_ctx)
    def _ctx():
        oc_ref[...] = out

    @pl.when(jnp.logical_not(is_ctx))
    def _lat():
        ol_ref[...] = out


def _combine(y_sorted, inv3, gate, x1, mod, g3, geo):
    n = x1.shape[0]
    nt = n // ROW_TILE
    return pl.pallas_call(
        functools.partial(_combine_kernel, ctx_tiles=geo.ctx_tiles),
        grid=(nt,),
        in_specs=[
            pl.BlockSpec((1, 1, TOP_K * ROW_TILE), lambda i: (i, 0, 0), memory_space=pltpu.SMEM),
            pl.BlockSpec(memory_space=pl.ANY),
            pl.BlockSpec((ROW_TILE, TOP_K), lambda i: (i, 0)),
            pl.BlockSpec((ROW_TILE, D_MODEL), lambda i: (i, 0)),
            pl.BlockSpec((1, 6, D_MODEL), lambda i: (geo.mod_group(i), 0, 0)),
            pl.BlockSpec((1, D_MODEL), lambda i: (0, 0)),
        ],
        out_specs=[geo.ctx_rows(D_MODEL), geo.lat_rows(D_MODEL)],
        out_shape=[jax.ShapeDtypeStruct((geo.n_ctx, D_MODEL), F32), jax.ShapeDtypeStruct((n - geo.n_ctx, D_MODEL), F32)],
        scratch_shapes=[pltpu.VMEM((TOP_K, ROW_TILE, D_MODEL), F32), pltpu.SemaphoreType.DMA(())],
        compiler_params=_cparams(("arbitrary",)),
        name="moe_combine",
    )(inv3, y_sorted, gate, x1, mod, g3)


def _plan(route, counts):
    nt = route.shape[0]
    n = nt * ROW_TILE
    expert = route[:, 0:TOP_K, :].astype(jnp.int32)
    gate = route[:, TOP_K:2 * TOP_K, :].transpose(0, 2, 1).reshape(n, TOP_K)
    rank = route[:, 2 * TOP_K:3 * TOP_K, :].astype(jnp.int32)
    cnt = counts[:, 0].astype(jnp.int32)
    padded = (cnt + MOE_TILE - 1) // MOE_TILE * MOE_TILE
    pad_end = jnp.cumsum(padded).astype(jnp.int32)
    pad_start = pad_end - padded
    start_of = jnp.sum(jnp.where(expert[..., None] == jnp.arange(N_EXPERTS), pad_start, 0), axis=-1)
    slots3 = (start_of + rank).reshape(nt, 1, TOP_K * ROW_TILE)
    n_tiles = n * TOP_K // MOE_TILE + N_EXPERTS
    tile_start = jnp.arange(n_tiles, dtype=jnp.int32) * MOE_TILE
    tile_expert = jnp.minimum(jnp.sum((tile_start[:, None] >= pad_end[None, :]).astype(jnp.int32), axis=1),
                              N_EXPERTS - 1)
    n_used = (pad_end[-1] // MOE_TILE).reshape(1)
    return gate, slots3, pad_end, tile_expert, n_used, n_tiles * MOE_TILE


class _Geometry:
    def __init__(self, nb_ctx, t_ctx, nb_lat, t_lat, tile):
        self.nb_ctx, self.t_ctx, self.nb_lat, self.t_lat, self.tile = nb_ctx, t_ctx, nb_lat, t_lat, tile
        self.n_ctx = nb_ctx * t_ctx
        self.n = self.n_ctx + nb_lat * t_lat
        self.ctx_tiles = self.n_ctx // tile
        self.lat_tiles = t_lat // tile

    def mod_group(self, i):
        return jnp.where(i < self.ctx_tiles, 0, 1 + (i - self.ctx_tiles) // self.lat_tiles)

    def ctx_rows(self, w):
        return pl.BlockSpec((self.tile, w), lambda i: (jnp.minimum(i, self.ctx_tiles - 1), 0))

    def lat_rows(self, w):
        return pl.BlockSpec((self.tile, w), lambda i: (jnp.maximum(i - self.ctx_tiles, 0), 0))

    def rope_block(self, i):
        return jnp.where(i < self.ctx_tiles, self.lat_tiles, (i - self.ctx_tiles) % self.lat_tiles)


def _rope_tables(t, tile):
    pos = jnp.arange(t)
    row = (pos // GRID_W).astype(F32)
    col = (pos % GRID_W).astype(F32)

    def cs(dim):
        q = dim // 4
        inv = ROPE_BASE ** (-jnp.arange(q, dtype=F32) / q)
        ar = row[:, None] * inv
        ac = col[:, None] * inv
        c = jnp.concatenate([jnp.cos(ar), jnp.cos(ar), jnp.cos(ac), jnp.cos(ac)], axis=-1)
        s = jnp.concatenate([-jnp.sin(ar), jnp.sin(ar), -jnp.sin(ac), jnp.sin(ac)], axis=-1)
        return c, s

    def with_identity(c, s):
        return (jnp.concatenate([c, jnp.ones((tile, c.shape[1]), F32)], axis=0),
                jnp.concatenate([s, jnp.zeros((tile, s.shape[1]), F32)], axis=0))

    c64, s64 = cs(RET_DK)
    rc, rs = with_identity(jnp.tile(c64, (1, RET_H)), jnp.tile(s64, (1, RET_H)))
    c32, s32 = cs(MLA_DR)
    pad = lambda a, v: jnp.concatenate([jnp.full((t, MLA_DN), v, F32), a, jnp.full((t, LANES - MLA_DN - MLA_DR), v, F32)], axis=-1)
    mc, ms = with_identity(pad(c32, 1.0), pad(s32, 0.0))
    return rc, rs, mc, ms


def _layer_params(l, norm_g, w_in, ret_decay, ret_norm, conv_w, conv_b, lru_gate_w, lru_gate_b, lru_lambda,
                  mla_q_norm, mla_kv_norm, mla_w_uq, mla_w_ukv, w_out, router_w, router_b,
                  moe_w_gu, moe_b_gu, moe_w_down, moe_b_down):
    p = {}
    p['g'] = [norm_g[l, i].reshape(1, D_MODEL) for i in range(4)]
    kr0 = 1920
    p['win'] = jnp.concatenate([w_in[l][:, :kr0], jnp.zeros((D_MODEL, MLA_DN), F32), w_in[l][:, kr0:],
                                jnp.zeros((D_MODEL, LANES - MLA_DN - MLA_DR), F32)], axis=1).astype(BF16)
    p['lg'] = jax.nn.log_sigmoid(ret_decay[l].astype(F32))
    p['retn'] = ret_norm[l].reshape(1, RET_W)
    p['qn'] = mla_q_norm[l].reshape(1, Q_RANK)
    p['kvn'] = mla_kv_norm[l].reshape(1, KV_RANK)
    wuq = mla_w_uq[l].reshape(Q_RANK, MLA_H, MLA_DN + MLA_DR)
    p['wuq'] = jnp.pad(wuq, ((0, 0), (0, 0), (0, LANES - MLA_DN - MLA_DR))).reshape(Q_RANK, MLA_H * LANES).astype(BF16)
    wukv = mla_w_ukv[l].reshape(KV_RANK, MLA_H, MLA_DN + MLA_DV)
    p['wk'] = jnp.pad(wukv[:, :, :MLA_DN], ((0, 0), (0, 0), (0, LANES - MLA_DN))).reshape(KV_RANK, MLA_H * LANES).astype(BF16)
    wv_t = jnp.pad(wukv[:, :, MLA_DN:].transpose(1, 2, 0), ((0, 0), (0, MLA_VT_ROWS - MLA_DV), (0, 0)))
    p['wv_t'] = wv_t.reshape(MLA_H * MLA_VT_ROWS, KV_RANK).astype(BF16)
    p['wout'] = w_out[l].astype(BF16)
    p['rw'] = jnp.pad(router_w[l], ((0, 0), (0, LANES - N_EXPERTS))).astype(BF16)
    p['rb'] = jnp.pad(router_b[l], (0, LANES - N_EXPERTS)).reshape(1, LANES)
    p['cw'] = conv_w[l].reshape(4, 2, LANES).transpose(1, 0, 2)
    p['cb'] = conv_b[l].reshape(2, 1, LANES)
    gw = lru_gate_w[l]
    halves = []
    for hh in range(2):
        cols = []
        for d in range(2):
            for g in range(2):
                blk = jnp.zeros((LANES, LANES), F32)
                for j in range(2):
                    blk = blk.at[j * LRU_BW:(j + 1) * LRU_BW, j * LRU_BW:(j + 1) * LRU_BW].set(gw[d, g, 2 * hh + j])
                cols.append(blk)
        halves.append(jnp.concatenate(cols, axis=1))
    p['wg'] = jnp.stack(halves).astype(BF16)
    gb = lru_gate_b[l].reshape(2, 2, 2, LANES)
    p['bg'] = gb.transpose(2, 0, 1, 3).reshape(2, 1, 4 * LANES)
    p['c8'] = (8.0 * jax.nn.log_sigmoid(lru_lambda[l].astype(F32))).reshape(2, 2, LANES).transpose(1, 0, 2)
    return p


def _forward(x_prompt, x_sample, c, state_ret, state_lru, cache_mla_ckv, cache_mla_krope, c_ctx, ada_w, ada_b,
             *weights):
    nb_ctx, t_ctx, _ = x_prompt.shape
    nb_lat, t_lat, _ = x_sample.shape
    geo = _Geometry(nb_ctx, t_ctx, nb_lat, t_lat, ROW_TILE)
    geo_in = _Geometry(nb_ctx, t_ctx, nb_lat, t_lat, IN_TILE)
    n_c = geo.n_ctx
    x_ctx = x_prompt.reshape(n_c, D_MODEL)
    x_lat = x_sample.reshape(-1, D_MODEL)
    cond = jnp.concatenate([c_ctx[None, :], c, jnp.zeros((16 - 1 - nb_lat, D_MODEL), F32)], axis=0)
    mod_all = _modulation(cond, ada_w, ada_b)[:, :1 + nb_lat].reshape(DEPTH, 1 + nb_lat, 6, D_MODEL)
    tabs = _rope_tables(t_lat, IN_TILE)
    krope_pad = jnp.pad(cache_mla_krope, ((0, 0), (0, 0), (0, 0), (MLA_DN, LANES - MLA_DN - MLA_DR)))
    ret_out, lru_out, ckv_out, kr_out = [], [], [], []
    for l in range(DEPTH):
        p = _layer_params(l, *weights)
        mod = mod_all[l]
        rq, rk, rv, rg, lx, ly, q, ckvn, kr = _in_projection(x_ctx, x_lat, mod, p['g'][0], p['win'], tabs, p['qn'],
                                                             p['kvn'], p['wuq'], geo_in)
        zero_s = jnp.zeros((nb_ctx, 2, 2, LANES, LANES), F32)
        ofc, obc, s_ctx = _retention(rq, rk, rv, p['lg'], zero_s, nb_ctx, t_ctx, 0)
        ofl, obl, _ = _retention(rq, rk, rv, p['lg'], _state_to_pairs(state_ret[:, l]), nb_lat, t_lat, n_c)
        lru_c, h_ctx = _lru(lx, ly, jnp.zeros((nb_ctx, 2, LRU_W), F32), p['c8'], p['cw'], p['cb'], p['wg'], p['bg'],
                            nb_ctx, t_ctx, 0)
        lru_l, _ = _lru(lx, ly, state_lru[:, l], p['c8'], p['cw'], p['cb'], p['wg'], p['bg'], nb_lat, t_lat, n_c)
        mla_c = _mla(q, ckvn, kr, None, p['wk'], p['wv_t'], nb_ctx, t_ctx, 0)
        mla_l = _mla(q, ckvn, kr, (cache_mla_ckv[:, l], krope_pad[:, l]), p['wk'], p['wv_t'], nb_lat, t_lat, n_c)
        x1, h2, route, counts = _out_projection((ofc, obc, lru_c, mla_c, x_ctx), (ofl, obl, lru_l, mla_l, x_lat), rg,
                                                mod, p['retn'], p['g'][1], p['g'][2], p['wout'], p['rw'], p['rb'], geo)
        gate, slots3, pad_end, tile_expert, n_used, n_slots = _plan(route, counts)
        xs = _dispatch(h2, slots3, pad_end, n_slots)
        y_sorted = _moe(xs, tile_expert, n_used, l, *weights[-4:])
        x_ctx, x_lat = _combine(y_sorted, slots3, gate, x1, mod, p['g'][3], geo)
        ret_out.append(_pairs_to_state(s_ctx))
        lru_out.append(h_ctx)
        ckv_out.append(ckvn[:n_c].reshape(nb_ctx, t_ctx, KV_RANK))
        kr_out.append(kr[:n_c, MLA_DN:MLA_DN + MLA_DR].reshape(nb_ctx, t_ctx, MLA_DR))
    y_prompt = x_ctx.reshape(nb_ctx, t_ctx, D_MODEL)
    y_sample = x_lat.reshape(nb_lat, t_lat, D_MODEL)
    return (y_prompt, y_sample, jnp.stack(ret_out, axis=1), jnp.stack(lru_out, axis=1),
            jnp.stack(ckv_out, axis=1), jnp.stack(kr_out, axis=1))


def kernel(x_prompt, x_sample, c, state_ret, state_lru, cache_mla_ckv, cache_mla_krope, c_ctx, ada_w, ada_b, norm_g, w_in, ret_decay, ret_norm, conv_w, conv_b, lru_gate_w, lru_gate_b, lru_lambda, mla_q_norm, mla_kv_norm, mla_w_uq, mla_w_ukv, w_out, router_w, router_b, moe_w_gu, moe_b_gu, moe_w_down, moe_b_down):
    return _forward(x_prompt, x_sample, c, state_ret, state_lru, cache_mla_ckv, cache_mla_krope, c_ctx, ada_w, ada_b,
                    norm_g, w_in, ret_decay, ret_norm, conv_w, conv_b, lru_gate_w, lru_gate_b, lru_lambda,
                    mla_q_norm, mla_kv_norm, mla_w_uq, mla_w_ukv, w_out, router_w, router_b,
                    moe_w_gu, moe_b_gu, moe_w_down, moe_b_down)
```

```python
import functools
import math

import jax
import jax.numpy as jnp
from jax import lax
from jax.experimental import pallas as pl
from jax.experimental.pallas import tpu as pltpu

F32 = jnp.float32
BF16 = jnp.bfloat16

D_MODEL = 1024
DEPTH = 2
GRID_W = 64
RET_H = 4
RET_DK = 64
RET_W = 256
RET_CHUNK = 128
LRU_W = 256
LRU_BLOCKS = 4
LRU_BW = 64
MLA_H = 8
MLA_DN = 64
MLA_DR = 32
MLA_DV = 64
MLA_W = MLA_H * MLA_DV
MLA_VT_ROWS = 80
Q_RANK = 256
KV_RANK = 128
ROPE_BASE = 10000.0
N_EXPERTS = 32
TOP_K = 4
D_EXPERT = 1024
SWIGLU_ALPHA = 1.702
SWIGLU_LIMIT = 7.0
EPS = 1e-6

LANES = 128
ROW_TILE = 256
IN_TILE = 512
MOE_TILE = 256
MOE_HCHUNK = 256
LRU_ROWS = 256
ATT_TQ = 256
ATT_KC = 256
ATT_AHEAD = 4
KV_BUILD_ROWS = 512
PROJ_PAD = 2048
VMEM_LIMIT = 56 * 1024 * 1024


def _cparams(sem):
    return pltpu.CompilerParams(dimension_semantics=sem, vmem_limit_bytes=VMEM_LIMIT)


def _rms(x, g):
    return x * lax.rsqrt(jnp.mean(x * x, axis=-1, keepdims=True) + EPS) * g


ROW_CHUNKS = D_MODEL // LANES


def _store_row_tiles(ref, x):
    for j in range(ROW_CHUNKS):
        ref[pl.ds(j, x.shape[0], stride=ROW_CHUNKS), :] = x[:, j * LANES:(j + 1) * LANES]


def _load_row_tiles(ref):
    rows = ref.shape[0] // ROW_CHUNKS
    return jnp.concatenate([ref[pl.ds(j, rows, stride=ROW_CHUNKS), :] for j in range(ROW_CHUNKS)], axis=1)


def _rope(x, c, s, quarter):
    w = x.shape[-1]
    lane = lax.broadcasted_iota(jnp.int32, x.shape, 1)
    first = (lane % (2 * quarter)) < quarter
    partner = jnp.where(first, pltpu.roll(x, w - quarter, 1), pltpu.roll(x, quarter, 1))
    return x * c + partner * s


def _mod_kernel(cond_ref, w_ref, b_ref, o_ref):
    c = cond_ref[...]
    s = (c * jax.nn.sigmoid(c)).astype(BF16)
    o_ref[0] = jnp.dot(s, w_ref[0].astype(BF16), preferred_element_type=F32) + b_ref[0]


def _modulation(cond, ada_w, ada_b):
    nblk = 6
    return pl.pallas_call(
        _mod_kernel,
        grid=(DEPTH, nblk),
        in_specs=[
            pl.BlockSpec((16, D_MODEL), lambda l, j: (0, 0)),
            pl.BlockSpec((1, D_MODEL, D_MODEL), lambda l, j: (l, 0, j)),
            pl.BlockSpec((1, 1, D_MODEL), lambda l, j: (l, 0, j)),
        ],
        out_specs=pl.BlockSpec((1, 16, D_MODEL), lambda l, j: (l, 0, j)),
        out_shape=jax.ShapeDtypeStruct((DEPTH, 16, 6 * D_MODEL), F32),
        compiler_params=_cparams(("arbitrary", "arbitrary")),
        name="adaln_mod",
    )(cond, ada_w, ada_b.reshape(DEPTH, 1, 6 * D_MODEL))


def _inproj_kernel(xc_ref, xl_ref, mod_ref, g0_ref, win_ref, rc_ref, rs_ref, mc_ref, ms_ref, qn_ref, kvn_ref, wuq_ref,
                   rq_ref, rk_ref, rv_ref, rg_ref, lx_ref, ly_ref, q_ref, ckv_ref, kr_ref, *, ctx_tiles):
    x = jnp.where(pl.program_id(0) < ctx_tiles, xc_ref[...], xl_ref[...])
    mod = mod_ref[0]
    h = _rms(x, g0_ref[...]) * (1.0 + mod[1:2]) + mod[0:1]
    hb = h.astype(BF16)
    z_mla = jnp.dot(hb, win_ref[:, 1536:2048], preferred_element_type=F32)
    z_qk = jnp.dot(hb, win_ref[:, 0:512], preferred_element_type=F32)
    cqn = _rms(z_mla[:, 0:256], qn_ref[...])
    q = jnp.dot(cqn.astype(BF16), wuq_ref[...], preferred_element_type=F32)
    z_rest = jnp.dot(hb, win_ref[:, 512:1536], preferred_element_type=F32)
    rc = rc_ref[...]
    rs = rs_ref[...]
    rq_ref[...] = _rope(z_qk[:, 0:256], rc, rs, 16)
    rk_ref[...] = _rope(z_qk[:, 256:512] * (RET_DK ** -0.5), rc, rs, 16)
    mc = mc_ref[...]
    ms = ms_ref[...]
    ckv_ref[...] = _rms(z_mla[:, 256:384], kvn_ref[...])
    kr_ref[...] = _rope(z_mla[:, 384:512], mc, ms, 8)
    for h_i in range(MLA_H):
        sl = slice(h_i * LANES, (h_i + 1) * LANES)
        q_ref[:, sl] = _rope(q[:, sl], mc, ms, 8).astype(BF16)
    rv_ref[...] = z_rest[:, 0:256]
    rg_ref[...] = z_rest[:, 256:512]
    lx_ref[...] = z_rest[:, 512:768]
    ly_ref[...] = z_rest[:, 768:1024]


def _in_projection(x_ctx, x_lat, mod, g0, win_p, tabs, qn, kvn, wuq_p, geo):
    n = geo.n
    nt = n // geo.tile
    rc, rs, mc, ms = tabs
    row = lambda w: pl.BlockSpec((geo.tile, w), lambda i: (i, 0))
    full = lambda a: pl.BlockSpec(a.shape, lambda i: (0,) * a.ndim)
    tab = lambda w: pl.BlockSpec((geo.tile, w), lambda i: (geo.rope_block(i), 0))
    outs = [(256, F32)] * 6 + [(MLA_H * LANES, BF16), (LANES, F32), (LANES, F32)]
    return pl.pallas_call(
        functools.partial(_inproj_kernel, ctx_tiles=geo.ctx_tiles),
        grid=(nt,),
        in_specs=[geo.ctx_rows(D_MODEL), geo.lat_rows(D_MODEL),
                  pl.BlockSpec((1, 6, D_MODEL), lambda i: (geo.mod_group(i), 0, 0)),
                  full(g0), full(win_p), tab(256), tab(256), tab(LANES), tab(LANES), full(qn), full(kvn), full(wuq_p)],
        out_specs=[row(w) for w, _ in outs],
        out_shape=[jax.ShapeDtypeStruct((n, w), dt) for w, dt in outs],
        compiler_params=_cparams(("arbitrary",)),
        name="in_projection",
    )(x_ctx, x_lat, mod, g0, win_p, rc, rs, mc, ms, qn, kvn, wuq_p)


def _ret_kernel(lg_ref, qf_ref, kf_ref, vf_ref, qb_ref, kb_ref, vb_ref, s0_ref,
                of_ref, ob_ref, sfin_ref, s_scr, intra_scr, cross_scr, into_scr, carry_scr):
    j = pl.program_id(1)
    c = RET_CHUNK
    row = lax.broadcasted_iota(jnp.int32, (c, c), 0)
    lane = lax.broadcasted_iota(jnp.int32, (c, c), 1)
    rowf = row.astype(F32)
    lanef = lane.astype(F32)

    @pl.when(j == 0)
    def _init():
        s_scr[...] = s0_ref[0]
        for d in range(2):
            for h in range(RET_H):
                lg = lg_ref[d, h]
                if d == 0:
                    keep = row >= lane
                    dist = rowf - lanef
                else:
                    keep = lane >= row
                    dist = lanef - rowf
                intra_scr[d, h] = jnp.where(keep, jnp.exp(jnp.where(keep, dist, 0.0) * lg), 0.0)
            for p in range(2):
                lgl = jnp.where(lane < RET_DK, lg_ref[d, 2 * p], lg_ref[d, 2 * p + 1])
                if d == 0:
                    cross_scr[d, p] = jnp.exp((rowf + 1.0) * lgl)
                    into_scr[d, p] = jnp.exp((c - 1.0 - rowf) * lgl)
                else:
                    cross_scr[d, p] = jnp.exp((c - rowf) * lgl)
                    into_scr[d, p] = jnp.exp(rowf * lgl)
                carry_scr[d, p] = jnp.exp(float(c) * lgl)

    same_head = (row < RET_DK) == (lane < RET_DK)

    chains = [(d, p, refs) for d, refs in ((0, (qf_ref, kf_ref, vf_ref, of_ref)), (1, (qb_ref, kb_ref, vb_ref, ob_ref)))
              for p in range(2)]
    staged = []
    for d, p, (q_ref, k_ref, v_ref, _) in chains:
        sl = slice(p * LANES, (p + 1) * LANES)
        q2b = q_ref[:, sl].astype(BF16)
        k2 = k_ref[:, sl]
        v2 = v_ref[:, sl]
        scores, vals = [], []
        for e in range(2):
            sel = (lane >= RET_DK) if e else (lane < RET_DK)
            ke = jnp.where(sel, k2, 0.0).astype(BF16)
            vals.append(jnp.where(sel, v2, 0.0).astype(BF16))
            scores.append(lax.dot_general(q2b, ke, (((1,), (1,)), ((), ())), preferred_element_type=F32))
        st = s_scr[d, p]
        from_state = jnp.dot(q2b, st.astype(BF16), preferred_element_type=F32)
        kw = (k2 * into_scr[d, p]).astype(BF16)
        upd = lax.dot_general(kw, v2.astype(BF16), (((0,), (0,)), ((), ())), preferred_element_type=F32)
        staged.append((scores, vals, st, from_state, upd))
    for (d, p, (_, _, _, o_ref)), (scores, vals, st, from_state, upd) in zip(chains, staged):
        o = from_state * cross_scr[d, p]
        for e in range(2):
            a = (scores[e] * intra_scr[d, 2 * p + e]).astype(BF16)
            o = o + jnp.dot(a, vals[e], preferred_element_type=F32)
        s_scr[d, p] = st * carry_scr[d, p] + jnp.where(same_head, upd, 0.0)
        o_ref[:, p * LANES:(p + 1) * LANES] = o

    @pl.when(j == pl.num_programs(1) - 1)
    def _fin():
        sfin_ref[0] = s_scr[...]


def _retention(rq, rk, rv, lg, s0, nb, t, row0):
    n = nb * t
    nc = t // RET_CHUNK
    base = row0 // RET_CHUNK
    fwd = lambda off: pl.BlockSpec((RET_CHUNK, RET_W), lambda b, j: (off + b * nc + j, 0))
    bwd = lambda off: pl.BlockSpec((RET_CHUNK, RET_W), lambda b, j: (off + b * nc + nc - 1 - j, 0))
    st = pl.BlockSpec((1, 2, 2, LANES, LANES), lambda b, j: (b, 0, 0, 0, 0))
    return pl.pallas_call(
        _ret_kernel,
        grid=(nb, nc),
        in_specs=[pl.BlockSpec(memory_space=pltpu.SMEM), fwd(base), fwd(base), fwd(base), bwd(base), bwd(base),
                  bwd(base), st],
        out_specs=[fwd(0), bwd(0), st],
        out_shape=[jax.ShapeDtypeStruct((n, RET_W), F32), jax.ShapeDtypeStruct((n, RET_W), F32),
                   jax.ShapeDtypeStruct((nb, 2, 2, LANES, LANES), F32)],
        scratch_shapes=[pltpu.VMEM((2, 2, LANES, LANES), F32), pltpu.VMEM((2, RET_H, RET_CHUNK, RET_CHUNK), F32),
                        pltpu.VMEM((2, 2, RET_CHUNK, LANES), F32), pltpu.VMEM((2, 2, RET_CHUNK, LANES), F32),
                        pltpu.VMEM((2, 2, RET_CHUNK, LANES), F32)],
        compiler_params=_cparams(("arbitrary", "arbitrary")),
        name="retention",
    )(lg, rq, rk, rv, rq, rk, rv, s0)


def _state_to_pairs(s):
    b = s.shape[0]
    s = s.reshape(b, 2, 2, 2, RET_DK, RET_DK)
    z = jnp.zeros_like(s[:, :, :, 0])
    top = jnp.concatenate([s[:, :, :, 0], z], axis=-1)
    bot = jnp.concatenate([z, s[:, :, :, 1]], axis=-1)
    return jnp.concatenate([top, bot], axis=-2)


def _pairs_to_state(s):
    b = s.shape[0]
    a = s[..., :RET_DK, :RET_DK]
    c = s[..., RET_DK:, RET_DK:]
    return jnp.stack([a, c], axis=3).reshape(b, 2, RET_H, RET_DK, RET_DK)


def _scan_rows(a, b, reverse):
    r = a.shape[0]
    rows = lax.broadcasted_iota(jnp.int32, a.shape, 0)
    s = 1
    while s < r:
        if reverse:
            a_s = pltpu.roll(a, r - s, 0)
            b_s = pltpu.roll(b, r - s, 0)
            m = rows < r - s
        else:
            a_s = pltpu.roll(a, s, 0)
            b_s = pltpu.roll(b, s, 0)
            m = rows >= s
        b = jnp.where(m, a * b_s + b, b)
        a = jnp.where(m, a * a_s, a)
        s *= 2
    return a, b


def _lru_kernel(c8_ref, cw_ref, cb_ref, wg_ref, bg_ref, lx_ref, ly_ref, h0_ref, o_ref, hfin_ref, xc_scr, hf_scr,
                *, t, r):
    nch = t // r
    cw = cw_ref[0]
    cb = cb_ref[0]
    wg = wg_ref[0]
    bg = bg_ref[0]
    c8 = c8_ref[0]

    def conv_body(c, carry):
        r0 = pl.multiple_of(c * r, r)
        cur = lx_ref[pl.ds(r0, r), :]
        prev = lx_ref[pl.ds(pl.multiple_of(jnp.maximum(r0 - 8, 0), 8), 8), :]
        nxt = lx_ref[pl.ds(pl.multiple_of(jnp.minimum(r0 + r, t - 8), 8), 8), :]
        prev = jnp.where(c > 0, prev, 0.0)
        nxt = jnp.where(c < nch - 1, nxt, 0.0)
        ext = jnp.concatenate([prev, cur, nxt], axis=0)
        acc = jnp.broadcast_to(cb, (r, LANES))
        for tap in range(4):
            sh = (2 - tap) % (r + 16)
            xs = ext if sh == 0 else pltpu.roll(ext, sh, 0)
            acc = acc + xs[8:8 + r] * cw[tap:tap + 1]
        xc_scr[pl.ds(r0, r), :] = acc
        return carry

    lax.fori_loop(0, nch, conv_body, 0)

    def gates(xc, d):
        g = jnp.dot(xc.astype(BF16), wg[:, d * 256:(d + 1) * 256], preferred_element_type=F32) + bg[:, d * 256:(d + 1) * 256]
        rg = jax.nn.sigmoid(g[:, :LANES])
        ig = jax.nn.sigmoid(g[:, LANES:])
        log_a = c8[d:d + 1] * rg
        a = jnp.exp(log_a)
        b = jnp.sqrt(jnp.tanh(-log_a) * (a * a + 1.0)) * (ig * xc)
        return a, b

    def fwd_body(c, h):
        r0 = pl.multiple_of(c * r, r)
        a, b = gates(xc_scr[pl.ds(r0, r), :], 0)
        a, b = _scan_rows(a, b, False)
        hc = a * h + b
        hf_scr[pl.ds(r0, r), :] = hc
        return hc[r - 1:r]

    h_f = lax.fori_loop(0, nch, fwd_body, h0_ref[0, 0:1, :])

    def bwd_body(i, h):
        r0 = pl.multiple_of((nch - 1 - i) * r, r)
        a, b = gates(xc_scr[pl.ds(r0, r), :], 1)
        a, b = _scan_rows(a, b, True)
        hc = a * h + b
        o_ref[pl.ds(r0, r), :] = ((hf_scr[pl.ds(r0, r), :] + hc) * jax.nn.gelu(ly_ref[pl.ds(r0, r), :])).astype(BF16)
        return hc[0:1]

    h_b = lax.fori_loop(0, nch, bwd_body, h0_ref[0, 1:2, :])
    hfin_ref[0, 0:1, :] = h_f
    hfin_ref[0, 1:2, :] = h_b


def _lru(lx, ly, h0, c8, cw, cb, wg, bg, nb, t, row0):
    base = row0 // t
    r = min(LRU_ROWS, t)
    seq = lambda off: pl.BlockSpec((t, LANES), lambda b, hh: (off + b, hh))
    par = lambda a: pl.BlockSpec((1,) + a.shape[1:], lambda b, hh: (hh,) + (0,) * (a.ndim - 1))
    st = pl.BlockSpec((1, 2, LANES), lambda b, hh: (b, 0, hh))
    return pl.pallas_call(
        functools.partial(_lru_kernel, t=t, r=r),
        grid=(nb, 2),
        in_specs=[par(c8), par(cw), par(cb), par(wg), par(bg), seq(base), seq(base), st],
        out_specs=[seq(0), st],
        out_shape=[jax.ShapeDtypeStruct((nb * t, LRU_W), BF16), jax.ShapeDtypeStruct((nb, 2, LRU_W), F32)],
        scratch_shapes=[pltpu.VMEM((t, LANES), F32), pltpu.VMEM((t, LANES), F32)],
        compiler_params=_cparams(("arbitrary", "arbitrary")),
        name="rg_lru",
    )(c8, cw, cb, wg, bg, lx, ly, h0)


def _mla_kernel(*refs, n_ctx, t, tq):
    if n_ctx:
        q_ref, ckv_ref, kr_ref, cckv_ref, ckr_ref, wk_ref, wv_ref = refs[:7]
    else:
        q_ref, ckv_ref, kr_ref, wk_ref, wv_ref = refs[:5]
    o_ref, k_scr, vt_scr, qt_scr, ot_scr = refs[-5:]
    qi = pl.program_id(1)
    cexp = (MLA_DN + MLA_DR) ** -0.5 * math.log2(math.e)
    ones_row = (lax.broadcasted_iota(jnp.int32, (MLA_H * MLA_VT_ROWS, 1), 0) % MLA_VT_ROWS == MLA_DV).astype(F32)

    def put(ckv, kr, c0):
        nrow = ckv.shape[0]
        ckv_b = ckv.astype(BF16)
        kn = jnp.dot(ckv_b, wk_ref[...], preferred_element_type=F32)
        vt = lax.dot_general(wv_ref[...], ckv_b, (((1,), (1,)), ((), ())), preferred_element_type=F32) + ones_row
        for h in range(MLA_H):
            k_scr[h, c0:c0 + nrow, :] = ((kn[:, h * LANES:(h + 1) * LANES] + kr) * cexp).astype(BF16)
            vt_scr[h, :, c0:c0 + nrow] = vt[h * MLA_VT_ROWS:(h + 1) * MLA_VT_ROWS].astype(BF16)

    @pl.when(qi == 0)
    def _build():
        if n_ctx:
            put(cckv_ref[0], ckr_ref[0], 0)
        step = min(KV_BUILD_ROWS, t)
        for c in range(t // step):
            put(ckv_ref[c * step:(c + 1) * step, :], kr_ref[c * step:(c + 1) * step, :], n_ctx + c * step)

    qt_scr[...] = q_ref[...].astype(F32).T.astype(BF16)
    s_len = n_ctx + t

    nchunk = s_len // ATT_KC
    steps = [(h, c) for h in range(MLA_H) for c in range(nchunk)]

    def score(i):
        h, c = steps[i]
        return jnp.dot(k_scr[h, c * ATT_KC:(c + 1) * ATT_KC, :], qt_scr[h * LANES:(h + 1) * LANES, :],
                       preferred_element_type=F32)

    pending = {i: score(i) for i in range(min(ATT_AHEAD, len(steps)))}
    held = None
    m = o = None
    for i in range(len(steps) + 1):
        if i + ATT_AHEAD < len(steps):
            pending[i + ATT_AHEAD] = score(i + ATT_AHEAD)
        if i < len(steps):
            h, c = steps[i]
            s = pending.pop(i)
            m_old = jnp.full((1, tq), -1e30, F32) if c == 0 else m
            m = jnp.maximum(m_old, jnp.max(s, axis=0, keepdims=True))
            p = jnp.exp2(s - m).astype(BF16)
            alpha = jnp.exp2(m_old - m)
        if held is not None:
            hh, cc, p_h, alpha_h = held
            pv = jnp.dot(vt_scr[hh, :, cc * ATT_KC:(cc + 1) * ATT_KC], p_h, preferred_element_type=F32)
            o = pv if cc == 0 else o * alpha_h + pv
            if cc == nchunk - 1:
                ot_scr[hh] = o[:MLA_DV] / o[MLA_DV:MLA_DV + 1]
        held = (h, c, p, alpha) if i < len(steps) else None
    for pp in range(MLA_H // 2):
        pair = jnp.concatenate([ot_scr[2 * pp], ot_scr[2 * pp + 1]], axis=0)
        o_ref[:, pp * LANES:(pp + 1) * LANES] = pair.T.astype(BF16)


def _mla(q, ckvn, kr, cache, wk, wv_t, nb, t, row0):
    tq = min(ATT_TQ, t)
    nq = t // tq
    n_ctx = 0 if cache is None else cache[0].shape[1]
    s_len = n_ctx + t
    qspec = pl.BlockSpec((tq, MLA_H * LANES), lambda b, i: (row0 // tq + b * nq + i, 0))
    seq = pl.BlockSpec((t, LANES), lambda b, i: (row0 // t + b, 0))
    full = lambda a: pl.BlockSpec(a.shape, lambda b, i: (0,) * a.ndim)
    ins = [q, ckvn, kr]
    specs = [qspec, seq, seq]
    if n_ctx:
        cspec = pl.BlockSpec((1, n_ctx, LANES), lambda b, i: (b, 0, 0))
        ins += [cache[0], cache[1]]
        specs += [cspec, cspec]
    ins += [wk, wv_t]
    specs += [full(wk), full(wv_t)]
    return pl.pallas_call(
        functools.partial(_mla_kernel, n_ctx=n_ctx, t=t, tq=tq),
        grid=(nb, nq),
        in_specs=specs,
        out_specs=pl.BlockSpec((tq, MLA_W), lambda b, i: (b * nq + i, 0)),
        out_shape=jax.ShapeDtypeStruct((nb * t, MLA_W), BF16),
        scratch_shapes=[pltpu.VMEM((MLA_H, s_len, LANES), BF16), pltpu.VMEM((MLA_H, MLA_VT_ROWS, s_len), BF16),
                        pltpu.VMEM((MLA_H * LANES, tq), BF16), pltpu.VMEM((MLA_H, MLA_DV, tq), F32)],
        compiler_params=_cparams(("arbitrary", "arbitrary")),
        name="mla_attention",
    )(*ins)


def _outproj_kernel(ofc_ref, obc_ref, lruc_ref, mlac_ref, xc_ref, ofl_ref, obl_ref, lrul_ref, mlal_ref, xl_ref, rg_ref,
                    mod_ref, retn_ref, g1_ref, g2_ref, wout_ref, rw_ref, rb_ref, x1_ref, h2_ref, route_ref, cnt_ref,
                    cnt_scr, *, ctx_tiles):
    is_ctx = pl.program_id(0) < ctx_tiles
    pick = lambda c_ref, l_ref: jnp.where(is_ctx, c_ref[...], l_ref[...])
    o = pick(ofc_ref, ofl_ref) + pick(obc_ref, obl_ref)
    hid = lax.broadcasted_iota(jnp.int32, o.shape, 1) // RET_DK

    def head_sum(v):
        tot = jnp.zeros_like(v)
        for hh in range(RET_H):
            msk = hid == hh
            tot = jnp.where(msk, jnp.sum(jnp.where(msk, v, 0.0), axis=1, keepdims=True), tot)
        return tot

    mu = head_sum(o) * (1.0 / RET_DK)
    dl = o - mu
    var = head_sum(dl * dl) * (1.0 / RET_DK)
    rg = rg_ref[...]
    ret = dl * lax.rsqrt(var + EPS) * retn_ref[...] * (rg * jax.nn.sigmoid(rg))
    mix = jnp.concatenate([ret.astype(BF16), pick(lruc_ref, lrul_ref), pick(mlac_ref, mlal_ref)], axis=1)
    mo = jnp.dot(mix, wout_ref[...], preferred_element_type=F32)
    mod = mod_ref[0]
    x1 = pick(xc_ref, xl_ref) + mod[2:3] * _rms(mo, g1_ref[...])
    x1_ref[...] = x1
    h2 = _rms(x1, g2_ref[...]) * (1.0 + mod[4:5]) + mod[3:4]
    _store_row_tiles(h2_ref, h2)

    tm = h2.shape[0]
    lg = jnp.dot(h2.astype(BF16), rw_ref[...], preferred_element_type=F32) + rb_ref[...]
    lgt = lg.T[:N_EXPERTS, :]
    rowf = lax.broadcasted_iota(jnp.int32, (N_EXPERTS, tm), 0).astype(F32)
    tops, idxs, hots = [], [], []
    for _ in range(TOP_K):
        m = jnp.max(lgt, axis=0, keepdims=True)
        idx = jnp.min(jnp.where(lgt == m, rowf, float(N_EXPERTS)), axis=0, keepdims=True)
        hot = rowf == idx
        lgt = jnp.where(hot, -jnp.inf, lgt)
        tops.append(m)
        idxs.append(idx)
        hots.append(hot)
    exps = [jnp.exp(t - tops[0]) for t in tops]
    den = exps[0] + exps[1] + exps[2] + exps[3]
    member = jnp.zeros((N_EXPERTS, tm), F32)
    for hot in hots:
        member = member + hot.astype(F32)

    @pl.when(pl.program_id(0) == 0)
    def _zero_counts():
        cnt_scr[...] = jnp.zeros_like(cnt_scr)

    earlier = (lax.broadcasted_iota(jnp.int32, (tm, tm), 0) < lax.broadcasted_iota(jnp.int32, (tm, tm), 1))
    counts = cnt_scr[...]
    before = jnp.dot(member.astype(BF16), earlier.astype(BF16), preferred_element_type=F32) + counts[:, 0:1]
    ranks = [jnp.sum(jnp.where(hot, before, 0.0), axis=0, keepdims=True) for hot in hots]
    gates = [e / den for e in exps]
    route_ref[0] = jnp.concatenate(idxs + gates + ranks + [jnp.zeros((TOP_K, tm), F32)], axis=0)
    counts = counts + jnp.sum(member, axis=1, keepdims=True)
    cnt_scr[...] = counts
    cnt_ref[...] = counts


def _out_projection(ctx_mix, lat_mix, rg, mod, retn, g1, g2, wout, rw, rb, geo):
    n = geo.n
    row = lambda w: pl.BlockSpec((ROW_TILE, w), lambda i: (i, 0))
    full = lambda a: pl.BlockSpec(a.shape, lambda i: (0,) * a.ndim)
    widths = (256, 256, 256, MLA_W, D_MODEL)
    return pl.pallas_call(
        functools.partial(_outproj_kernel, ctx_tiles=geo.ctx_tiles),
        grid=(n // ROW_TILE,),
        in_specs=[geo.ctx_rows(w) for w in widths] + [geo.lat_rows(w) for w in widths] + [
                  row(256),
                  pl.BlockSpec((1, 6, D_MODEL), lambda i: (geo.mod_group(i), 0, 0)),
                  full(retn), full(g1), full(g2), full(wout), full(rw), full(rb)],
        out_specs=[row(D_MODEL), pl.BlockSpec((ROW_TILE * ROW_CHUNKS, LANES), lambda i: (i, 0)),
                   pl.BlockSpec((1, 4 * TOP_K, ROW_TILE), lambda i: (i, 0, 0)),
                   pl.BlockSpec((N_EXPERTS, LANES), lambda i: (0, 0))],
        out_shape=[jax.ShapeDtypeStruct((n, D_MODEL), F32), jax.ShapeDtypeStruct((n * ROW_CHUNKS, LANES), F32),
                   jax.ShapeDtypeStruct((n // ROW_TILE, 4 * TOP_K, ROW_TILE), F32),
                   jax.ShapeDtypeStruct((N_EXPERTS, LANES), F32)],
        scratch_shapes=[pltpu.VMEM((N_EXPERTS, LANES), F32)],
        compiler_params=_cparams(("arbitrary",)),
        name="out_projection",
    )(*ctx_mix, *lat_mix, rg, mod, retn, g1, g2, wout, rw, rb)


def _dispatch_kernel(pe_ref, slot_ref, h_ref, xs_hbm, zero_scr, sem):
    i = pl.program_id(0)
    n_slots = xs_hbm.shape[0] // ROW_CHUNKS

    def slot_rows(first_slot, n):
        return xs_hbm.at[pl.ds(pl.multiple_of(first_slot * ROW_CHUNKS, ROW_CHUNKS), n * ROW_CHUNKS), :]

    @pl.when(i == 0)
    def _zero_pads():
        zero_scr[...] = jnp.zeros_like(zero_scr)

        def fill(e):
            end = pe_ref[e]
            start = 0 if e == 0 else pe_ref[e - 1]
            return end > start, pltpu.make_async_copy(zero_scr, slot_rows(jnp.maximum(end - MOE_TILE, 0), MOE_TILE), sem)

        def fill_tail(j):
            row = pe_ref[N_EXPERTS - 1] + j * MOE_TILE
            dst = slot_rows(jnp.minimum(row, n_slots - MOE_TILE), MOE_TILE)
            return row < n_slots, pltpu.make_async_copy(zero_scr, dst, sem)

        for e in range(N_EXPERTS):
            for todo, cp in (fill(e), fill_tail(e)):
                pl.when(todo)(cp.start)
        for e in range(N_EXPERTS):
            for todo, cp in (fill(e), fill_tail(e)):
                pl.when(todo)(cp.wait)

    for r in range(ROW_TILE):
        for k in range(TOP_K):
            first = pl.multiple_of(slot_ref[0, 0, k * ROW_TILE + r], ROW_CHUNKS)
            dst = xs_hbm.at[pl.ds(first, ROW_CHUNKS), :]
            pltpu.make_async_copy(h_ref.at[pl.ds(r * ROW_CHUNKS, ROW_CHUNKS), :], dst, sem).start(priority=k % 2)
    for k in range(TOP_K):
        pltpu.make_async_copy(h_ref, xs_hbm.at[pl.ds(0, ROW_TILE * ROW_CHUNKS), :], sem).wait()


def _dispatch(h2, slots3, pad_end, n_slots):
    n = h2.shape[0] // ROW_CHUNKS
    grid_spec = pltpu.PrefetchScalarGridSpec(
        num_scalar_prefetch=1,
        grid=(n // ROW_TILE,),
        in_specs=[
            pl.BlockSpec((1, 1, TOP_K * ROW_TILE), lambda i, pe: (i, 0, 0), memory_space=pltpu.SMEM),
            pl.BlockSpec((ROW_TILE * ROW_CHUNKS, LANES), lambda i, pe: (i, 0)),
        ],
        out_specs=pl.BlockSpec(memory_space=pl.ANY),
        scratch_shapes=[pltpu.VMEM((MOE_TILE * ROW_CHUNKS, LANES), F32), pltpu.SemaphoreType.DMA(())],
    )
    return pl.pallas_call(
        _dispatch_kernel,
        grid_spec=grid_spec,
        out_shape=jax.ShapeDtypeStruct((n_slots * ROW_CHUNKS, LANES), F32),
        compiler_params=_cparams(("arbitrary",)),
        name="moe_dispatch",
    )(pad_end, slots3, h2)


def _moe_kernel(te_ref, nu_ref, x_ref, wgu_ref, bgu_ref, wd_ref, bd_ref, y_ref, wgu_b, wd_b):
    t = pl.program_id(0)
    n_used = nu_ref[0]
    nch = D_EXPERT // MOE_HCHUNK
    new_expert = jnp.logical_or(t == 0, te_ref[t] != te_ref[jnp.maximum(t - 1, 0)])

    @pl.when(jnp.logical_and(new_expert, t < n_used))
    def _cast_weights():
        for j in range(nch):
            cols = slice(j * MOE_HCHUNK, (j + 1) * MOE_HCHUNK)
            ucols = slice(D_EXPERT + j * MOE_HCHUNK, D_EXPERT + (j + 1) * MOE_HCHUNK)
            wgu_b[j, :, :MOE_HCHUNK] = wgu_ref[0, 0, :, cols].astype(BF16)
            wgu_b[j, :, MOE_HCHUNK:] = wgu_ref[0, 0, :, ucols].astype(BF16)
        wd_b[...] = wd_ref[0, 0].astype(BF16)

    @pl.when(t < n_used)
    def _compute():
        x = _load_row_tiles(x_ref).astype(BF16)
        bgu = bgu_ref[0, 0]

        def gate_up(j):
            return jnp.dot(x, wgu_b[j], preferred_element_type=F32)

        pending = gate_up(0)
        y = None
        for j in range(nch):
            gu = pending
            if j + 1 < nch:
                pending = gate_up(j + 1)
            cols = slice(j * MOE_HCHUNK, (j + 1) * MOE_HCHUNK)
            ucols = slice(D_EXPERT + j * MOE_HCHUNK, D_EXPERT + (j + 1) * MOE_HCHUNK)
            g = jnp.minimum(gu[:, :MOE_HCHUNK] + bgu[:, cols], SWIGLU_LIMIT)
            u = jnp.clip(gu[:, MOE_HCHUNK:] + bgu[:, ucols], -SWIGLU_LIMIT, SWIGLU_LIMIT)
            act = (g * jax.nn.sigmoid(SWIGLU_ALPHA * g) * (u + 1.0)).astype(BF16)
            part = jnp.dot(act, wd_b[cols, :], preferred_element_type=F32)
            y = part if y is None else y + part
        _store_row_tiles(y_ref, y + bd_ref[0, 0])

    @pl.when(t >= n_used)
    def _idle():
        y_ref[...] = jnp.zeros_like(y_ref)


def _moe(xs, tile_expert, n_used, layer, wgu, bgu, wd, bd):
    n_tiles = tile_expert.shape[0]
    wspec = lambda r, c: pl.BlockSpec((1, 1, r, c), lambda t, te, nu: (layer, te[t], 0, 0))
    grid_spec = pltpu.PrefetchScalarGridSpec(
        num_scalar_prefetch=2,
        grid=(n_tiles,),
        in_specs=[
            pl.BlockSpec((MOE_TILE * ROW_CHUNKS, LANES),
                         lambda t, te, nu: (jnp.minimum(t, jnp.maximum(nu[0] - 1, 0)), 0)),
            wspec(D_MODEL, 2 * D_EXPERT), wspec(1, 2 * D_EXPERT), wspec(D_EXPERT, D_MODEL), wspec(1, D_MODEL),
        ],
        out_specs=pl.BlockSpec((MOE_TILE * ROW_CHUNKS, LANES), lambda t, te, nu: (t, 0)),
        scratch_shapes=[pltpu.VMEM((D_EXPERT // MOE_HCHUNK, D_MODEL, 2 * MOE_HCHUNK), BF16),
                        pltpu.VMEM((D_EXPERT, D_MODEL), BF16)],
    )
    return pl.pallas_call(
        _moe_kernel,
        grid_spec=grid_spec,
        out_shape=jax.ShapeDtypeStruct((n_tiles * MOE_TILE * ROW_CHUNKS, LANES), F32),
        compiler_params=_cparams(("arbitrary",)),
        name="moe_experts",
    )(tile_expert, n_used, xs, wgu, bgu.reshape(DEPTH, N_EXPERTS, 1, -1), wd, bd.reshape(DEPTH, N_EXPERTS, 1, -1))


def _combine_kernel(inv_ref, y_hbm, gate_ref, x1_ref, mod_ref, g3_ref, oc_ref, ol_ref, buf, sem, *, ctx_tiles):
    for r in range(ROW_TILE):
        for k in range(TOP_K):
            first = pl.multiple_of(inv_ref[0, 0, k * ROW_TILE + r], ROW_CHUNKS)
            pltpu.make_async_copy(y_hbm.at[pl.ds(first, ROW_CHUNKS), :],
                                  buf.at[k, pl.ds(r * ROW_CHUNKS, ROW_CHUNKS), :], sem).start(priority=k % 2)
    for k in range(TOP_K):
        pltpu.make_async_copy(y_hbm.at[pl.ds(0, ROW_TILE * ROW_CHUNKS), :], buf.at[k], sem).wait()
    gate = gate_ref[...]
    ff = gate[:, 0:1] * _load_row_tiles(buf.at[0])
    for k in range(1, TOP_K):
        ff = ff + gate[:, k:k + 1] * _load_row_tiles(buf.at[k])
    out = x1_ref[...] + mod_ref[0][5:6] * _rms(ff, g3_ref[...])
    is_ctx = pl.program_id(0) < ctx_tiles

    @pl.when(is_ctx)
    def _ctx():
        oc_ref[...] = out

    @pl.when(jnp.logical_not(is_ctx))
    def _lat():
        ol_ref[...] = out


def _combine(y_sorted, inv3, gate, x1, mod, g3, geo):
    n = x1.shape[0]
    nt = n // ROW_TILE
    return pl.pallas_call(
        functools.partial(_combine_kernel, ctx_tiles=geo.ctx_tiles),
        grid=(nt,),
        in_specs=[
            pl.BlockSpec((1, 1, TOP_K * ROW_TILE), lambda i: (i, 0, 0), memory_space=pltpu.SMEM),
            pl.BlockSpec(memory_space=pl.ANY),
            pl.BlockSpec((ROW_TILE, TOP_K), lambda i: (i, 0)),
            pl.BlockSpec((ROW_TILE, D_MODEL), lambda i: (i, 0)),
            pl.BlockSpec((1, 6, D_MODEL), lambda i: (geo.mod_group(i), 0, 0)),
            pl.BlockSpec((1, D_MODEL), lambda i: (0, 0)),
        ],
        out_specs=[geo.ctx_rows(D_MODEL), geo.lat_rows(D_MODEL)],
        out_shape=[jax.ShapeDtypeStruct((geo.n_ctx, D_MODEL), F32), jax.ShapeDtypeStruct((n - geo.n_ctx, D_MODEL), F32)],
        scratch_shapes=[pltpu.VMEM((TOP_K, ROW_TILE * ROW_CHUNKS, LANES), F32), pltpu.SemaphoreType.DMA(())],
        compiler_params=_cparams(("arbitrary",)),
        name="moe_combine",
    )(inv3, y_sorted, gate, x1, mod, g3)


def _plan(route, counts):
    nt = route.shape[0]
    n = nt * ROW_TILE
    expert = route[:, 0:TOP_K, :].astype(jnp.int32)
    gate = route[:, TOP_K:2 * TOP_K, :].transpose(0, 2, 1).reshape(n, TOP_K)
    rank = route[:, 2 * TOP_K:3 * TOP_K, :].astype(jnp.int32)
    cnt = counts[:, 0].astype(jnp.int32)
    padded = (cnt + MOE_TILE - 1) // MOE_TILE * MOE_TILE
    pad_end = jnp.cumsum(padded).astype(jnp.int32)
    pad_start = pad_end - padded
    start_of = jnp.sum(jnp.where(expert[..., None] == jnp.arange(N_EXPERTS), pad_start, 0), axis=-1)
    slots3 = ((start_of + rank) * ROW_CHUNKS).reshape(nt, 1, TOP_K * ROW_TILE)
    n_tiles = n * TOP_K // MOE_TILE + N_EXPERTS
    tile_start = jnp.arange(n_tiles, dtype=jnp.int32) * MOE_TILE
    tile_expert = jnp.minimum(jnp.sum((tile_start[:, None] >= pad_end[None, :]).astype(jnp.int32), axis=1),
                              N_EXPERTS - 1)
    n_used = (pad_end[-1] // MOE_TILE).reshape(1)
    return gate, slots3, pad_end, tile_expert, n_used, n_tiles * MOE_TILE


class _Geometry:
    def __init__(self, nb_ctx, t_ctx, nb_lat, t_lat, tile):
        self.nb_ctx, self.t_ctx, self.nb_lat, self.t_lat, self.tile = nb_ctx, t_ctx, nb_lat, t_lat, tile
        self.n_ctx = nb_ctx * t_ctx
        self.n = self.n_ctx + nb_lat * t_lat
        self.ctx_tiles = self.n_ctx // tile
        self.lat_tiles = t_lat // tile

    def mod_group(self, i):
        return jnp.where(i < self.ctx_tiles, 0, 1 + (i - self.ctx_tiles) // self.lat_tiles)

    def ctx_rows(self, w):
        return pl.BlockSpec((self.tile, w), lambda i: (jnp.minimum(i, self.ctx_tiles - 1), 0))

    def lat_rows(self, w):
        return pl.BlockSpec((self.tile, w), lambda i: (jnp.maximum(i - self.ctx_tiles, 0), 0))

    def rope_block(self, i):
        return jnp.where(i < self.ctx_tiles, self.lat_tiles, (i - self.ctx_tiles) % self.lat_tiles)


def _rope_tables(t, tile):
    pos = jnp.arange(t)
    row = (pos // GRID_W).astype(F32)
    col = (pos % GRID_W).astype(F32)

    def cs(dim):
        q = dim // 4
        inv = ROPE_BASE ** (-jnp.arange(q, dtype=F32) / q)
        ar = row[:, None] * inv
        ac = col[:, None] * inv
        c = jnp.concatenate([jnp.cos(ar), jnp.cos(ar), jnp.cos(ac), jnp.cos(ac)], axis=-1)
        s = jnp.concatenate([-jnp.sin(ar), jnp.sin(ar), -jnp.sin(ac), jnp.sin(ac)], axis=-1)
        return c, s

    def with_identity(c, s):
        return (jnp.concatenate([c, jnp.ones((tile, c.shape[1]), F32)], axis=0),
                jnp.concatenate([s, jnp.zeros((tile, s.shape[1]), F32)], axis=0))

    c64, s64 = cs(RET_DK)
    rc, rs = with_identity(jnp.tile(c64, (1, RET_H)), jnp.tile(s64, (1, RET_H)))
    c32, s32 = cs(MLA_DR)
    pad = lambda a, v: jnp.concatenate([jnp.full((t, MLA_DN), v, F32), a, jnp.full((t, LANES - MLA_DN - MLA_DR), v, F32)], axis=-1)
    mc, ms = with_identity(pad(c32, 1.0), pad(s32, 0.0))
    return rc, rs, mc, ms


def _layer_params(l, norm_g, w_in, ret_decay, ret_norm, conv_w, conv_b, lru_gate_w, lru_gate_b, lru_lambda,
                  mla_q_norm, mla_kv_norm, mla_w_uq, mla_w_ukv, w_out, router_w, router_b):
    p = {}
    p['g'] = [norm_g[l, i].reshape(1, D_MODEL) for i in range(4)]
    kr0 = 1920
    p['win'] = jnp.concatenate([w_in[l][:, :kr0], jnp.zeros((D_MODEL, MLA_DN), F32), w_in[l][:, kr0:],
                                jnp.zeros((D_MODEL, LANES - MLA_DN - MLA_DR), F32)], axis=1).astype(BF16)
    p['lg'] = jax.nn.log_sigmoid(ret_decay[l].astype(F32))
    p['retn'] = ret_norm[l].reshape(1, RET_W)
    p['qn'] = mla_q_norm[l].reshape(1, Q_RANK)
    p['kvn'] = mla_kv_norm[l].reshape(1, KV_RANK)
    wuq = mla_w_uq[l].reshape(Q_RANK, MLA_H, MLA_DN + MLA_DR)
    p['wuq'] = jnp.pad(wuq, ((0, 0), (0, 0), (0, LANES - MLA_DN - MLA_DR))).reshape(Q_RANK, MLA_H * LANES).astype(BF16)
    wukv = mla_w_ukv[l].reshape(KV_RANK, MLA_H, MLA_DN + MLA_DV)
    p['wk'] = jnp.pad(wukv[:, :, :MLA_DN], ((0, 0), (0, 0), (0, LANES - MLA_DN))).reshape(KV_RANK, MLA_H * LANES).astype(BF16)
    wv_t = jnp.pad(wukv[:, :, MLA_DN:].transpose(1, 2, 0), ((0, 0), (0, MLA_VT_ROWS - MLA_DV), (0, 0)))
    p['wv_t'] = wv_t.reshape(MLA_H * MLA_VT_ROWS, KV_RANK).astype(BF16)
    p['wout'] = w_out[l].astype(BF16)
    p['rw'] = jnp.pad(router_w[l], ((0, 0), (0, LANES - N_EXPERTS))).astype(BF16)
    p['rb'] = jnp.pad(router_b[l], (0, LANES - N_EXPERTS)).reshape(1, LANES)
    p['cw'] = conv_w[l].reshape(4, 2, LANES).transpose(1, 0, 2)
    p['cb'] = conv_b[l].reshape(2, 1, LANES)
    gw = lru_gate_w[l]
    halves = []
    for hh in range(2):
        cols = []
        for d in range(2):
            for g in range(2):
                blk = jnp.zeros((LANES, LANES), F32)
                for j in range(2):
                    blk = blk.at[j * LRU_BW:(j + 1) * LRU_BW, j * LRU_BW:(j + 1) * LRU_BW].set(gw[d, g, 2 * hh + j])
                cols.append(blk)
        halves.append(jnp.concatenate(cols, axis=1))
    p['wg'] = jnp.stack(halves).astype(BF16)
    gb = lru_gate_b[l].reshape(2, 2, 2, LANES)
    p['bg'] = gb.transpose(2, 0, 1, 3).reshape(2, 1, 4 * LANES)
    p['c8'] = (8.0 * jax.nn.log_sigmoid(lru_lambda[l].astype(F32))).reshape(2, 2, LANES).transpose(1, 0, 2)
    return p


def _forward(x_prompt, x_sample, c, state_ret, state_lru, cache_mla_ckv, cache_mla_krope, c_ctx, ada_w, ada_b,
             *weights):
    nb_ctx, t_ctx, _ = x_prompt.shape
    nb_lat, t_lat, _ = x_sample.shape
    geo = _Geometry(nb_ctx, t_ctx, nb_lat, t_lat, ROW_TILE)
    geo_in = _Geometry(nb_ctx, t_ctx, nb_lat, t_lat, IN_TILE)
    n_c = geo.n_ctx
    x_ctx = x_prompt.reshape(n_c, D_MODEL)
    x_lat = x_sample.reshape(-1, D_MODEL)
    cond = jnp.concatenate([c_ctx[None, :], c, jnp.zeros((16 - 1 - nb_lat, D_MODEL), F32)], axis=0)
    mod_all = _modulation(cond, ada_w, ada_b)[:, :1 + nb_lat].reshape(DEPTH, 1 + nb_lat, 6, D_MODEL)
    tabs = _rope_tables(t_lat, IN_TILE)
    krope_pad = jnp.pad(cache_mla_krope, ((0, 0), (0, 0), (0, 0), (MLA_DN, LANES - MLA_DN - MLA_DR)))
    ret_out, lru_out, ckv_out, kr_out = [], [], [], []
    for l in range(DEPTH):
        p = _layer_params(l, *weights[:-4])
        mod = mod_all[l]
        rq, rk, rv, rg, lx, ly, q, ckvn, kr = _in_projection(x_ctx, x_lat, mod, p['g'][0], p['win'], tabs, p['qn'],
                                                             p['kvn'], p['wuq'], geo_in)
        zero_s = jnp.zeros((nb_ctx, 2, 2, LANES, LANES), F32)
        ofc, obc, s_ctx = _retention(rq, rk, rv, p['lg'], zero_s, nb_ctx, t_ctx, 0)
        ofl, obl, _ = _retention(rq, rk, rv, p['lg'], _state_to_pairs(state_ret[:, l]), nb_lat, t_lat, n_c)
        lru_c, h_ctx = _lru(lx, ly, jnp.zeros((nb_ctx, 2, LRU_W), F32), p['c8'], p['cw'], p['cb'], p['wg'], p['bg'],
                            nb_ctx, t_ctx, 0)
        lru_l, _ = _lru(lx, ly, state_lru[:, l], p['c8'], p['cw'], p['cb'], p['wg'], p['bg'], nb_lat, t_lat, n_c)
        mla_c = _mla(q, ckvn, kr, None, p['wk'], p['wv_t'], nb_ctx, t_ctx, 0)
        mla_l = _mla(q, ckvn, kr, (cache_mla_ckv[:, l], krope_pad[:, l]), p['wk'], p['wv_t'], nb_lat, t_lat, n_c)
        x1, h2, route, counts = _out_projection((ofc, obc, lru_c, mla_c, x_ctx), (ofl, obl, lru_l, mla_l, x_lat), rg,
                                                mod, p['retn'], p['g'][1], p['g'][2], p['wout'], p['rw'], p['rb'], geo)
        gate, slots3, pad_end, tile_expert, n_used, n_slots = _plan(route, counts)
        xs = _dispatch(h2, slots3, pad_end, n_slots)
        y_sorted = _moe(xs, tile_expert, n_used, l, *weights[-4:])
        x_ctx, x_lat = _combine(y_sorted, slots3, gate, x1, mod, p['g'][3], geo)
        ret_out.append(_pairs_to_state(s_ctx))
        lru_out.append(h_ctx)
        ckv_out.append(ckvn[:n_c].reshape(nb_ctx, t_ctx, KV_RANK))
        kr_out.append(kr[:n_c, MLA_DN:MLA_DN + MLA_DR].reshape(nb_ctx, t_ctx, MLA_DR))
    y_prompt = x_ctx.reshape(nb_ctx, t_ctx, D_MODEL)
    y_sample = x_lat.reshape(nb_lat, t_lat, D_MODEL)
    return (y_prompt, y_sample, jnp.stack(ret_out, axis=1), jnp.stack(lru_out, axis=1),
            jnp.stack(ckv_out, axis=1), jnp.stack(kr_out, axis=1))


def kernel(x_prompt, x_sample, c, state_ret, state_lru, cache_mla_ckv, cache_mla_krope, c_ctx, ada_w, ada_b, norm_g, w_in, ret_decay, ret_norm, conv_w, conv_b, lru_gate_w, lru_gate_b, lru_lambda, mla_q_norm, mla_kv_norm, mla_w_uq, mla_w_ukv, w_out, router_w, router_b, moe_w_gu, moe_b_gu, moe_w_down, moe_b_down):
    return _forward(x_prompt, x_sample, c, state_ret, state_lru, cache_mla_ckv, cache_mla_krope, c_ctx, ada_w, ada_b,
                    norm_g, w_in, ret_decay, ret_norm, conv_w, conv_b, lru_gate_w, lru_gate_b, lru_lambda,
                    mla_q_norm, mla_kv_norm, mla_w_uq, mla_w_ukv, w_out, router_w, router_b,
                    moe_w_gu, moe_b_gu, moe_w_down, moe_b_down)
```

```python
import functools
import math

import jax
import jax.numpy as jnp
from jax import lax
from jax.experimental import pallas as pl
from jax.experimental.pallas import tpu as pltpu

F32 = jnp.float32
BF16 = jnp.bfloat16

D_MODEL = 1024
DEPTH = 2
GRID_W = 64
RET_H = 4
RET_DK = 64
RET_W = 256
RET_CHUNK = 128
LRU_W = 256
LRU_BLOCKS = 4
LRU_BW = 64
MLA_H = 8
MLA_DN = 64
MLA_DR = 32
MLA_DV = 64
MLA_W = MLA_H * MLA_DV
MLA_VT_ROWS = 80
Q_RANK = 256
KV_RANK = 128
ROPE_BASE = 10000.0
N_EXPERTS = 32
TOP_K = 4
D_EXPERT = 1024
SWIGLU_ALPHA = 1.702
SWIGLU_LIMIT = 7.0
EPS = 1e-6

LANES = 128
ROW_TILE = 256
IN_TILE = 512
MOE_TILE = 512
MOE_HCHUNK = 256
LRU_ROWS = 256
ATT_TQ = 256
ATT_KC = 256
ATT_AHEAD = 4
KV_BUILD_ROWS = 512
PROJ_PAD = 2048
VMEM_LIMIT = 56 * 1024 * 1024


def _cparams(sem):
    return pltpu.CompilerParams(dimension_semantics=sem, vmem_limit_bytes=VMEM_LIMIT)


def _rms(x, g):
    return x * lax.rsqrt(jnp.mean(x * x, axis=-1, keepdims=True) + EPS) * g


ROW_CHUNKS = D_MODEL // LANES


def _store_row_tiles(ref, x):
    for j in range(ROW_CHUNKS):
        ref[pl.ds(j, x.shape[0], stride=ROW_CHUNKS), :] = x[:, j * LANES:(j + 1) * LANES]


def _load_row_tiles(ref):
    rows = ref.shape[0] // ROW_CHUNKS
    return jnp.concatenate([ref[pl.ds(j, rows, stride=ROW_CHUNKS), :] for j in range(ROW_CHUNKS)], axis=1)


def _rope(x, c, s, quarter):
    w = x.shape[-1]
    lane = lax.broadcasted_iota(jnp.int32, x.shape, 1)
    first = (lane % (2 * quarter)) < quarter
    partner = jnp.where(first, pltpu.roll(x, w - quarter, 1), pltpu.roll(x, quarter, 1))
    return x * c + partner * s


def _mod_kernel(cond_ref, w_ref, b_ref, o_ref):
    c = cond_ref[...]
    s = (c * jax.nn.sigmoid(c)).astype(BF16)
    o_ref[0] = jnp.dot(s, w_ref[0].astype(BF16), preferred_element_type=F32) + b_ref[0]


def _modulation(cond, ada_w, ada_b):
    nblk = 6
    return pl.pallas_call(
        _mod_kernel,
        grid=(DEPTH, nblk),
        in_specs=[
            pl.BlockSpec((16, D_MODEL), lambda l, j: (0, 0)),
            pl.BlockSpec((1, D_MODEL, D_MODEL), lambda l, j: (l, 0, j)),
            pl.BlockSpec((1, 1, D_MODEL), lambda l, j: (l, 0, j)),
        ],
        out_specs=pl.BlockSpec((1, 16, D_MODEL), lambda l, j: (l, 0, j)),
        out_shape=jax.ShapeDtypeStruct((DEPTH, 16, 6 * D_MODEL), F32),
        compiler_params=_cparams(("arbitrary", "arbitrary")),
        name="adaln_mod",
    )(cond, ada_w, ada_b.reshape(DEPTH, 1, 6 * D_MODEL))


def _inproj_kernel(xc_ref, xl_ref, mod_ref, g0_ref, win_ref, rc_ref, rs_ref, mc_ref, ms_ref, qn_ref, kvn_ref, wuq_ref,
                   rq_ref, rk_ref, rv_ref, rg_ref, lx_ref, ly_ref, q_ref, ckv_ref, kr_ref, *, ctx_tiles):
    x = jnp.where(pl.program_id(0) < ctx_tiles, xc_ref[...], xl_ref[...])
    mod = mod_ref[0]
    h = _rms(x, g0_ref[...]) * (1.0 + mod[1:2]) + mod[0:1]
    hb = h.astype(BF16)
    z_mla = jnp.dot(hb, win_ref[:, 1536:2048], preferred_element_type=F32)
    z_qk = jnp.dot(hb, win_ref[:, 0:512], preferred_element_type=F32)
    cqn = _rms(z_mla[:, 0:256], qn_ref[...])
    q = jnp.dot(cqn.astype(BF16), wuq_ref[...], preferred_element_type=F32)
    z_rest = jnp.dot(hb, win_ref[:, 512:1536], preferred_element_type=F32)
    rc = rc_ref[...]
    rs = rs_ref[...]
    rq_ref[...] = _rope(z_qk[:, 0:256], rc, rs, 16)
    rk_ref[...] = _rope(z_qk[:, 256:512] * (RET_DK ** -0.5), rc, rs, 16)
    mc = mc_ref[...]
    ms = ms_ref[...]
    ckv_ref[...] = _rms(z_mla[:, 256:384], kvn_ref[...])
    kr_ref[...] = _rope(z_mla[:, 384:512], mc, ms, 8)
    for h_i in range(MLA_H):
        sl = slice(h_i * LANES, (h_i + 1) * LANES)
        q_ref[:, sl] = _rope(q[:, sl], mc, ms, 8).astype(BF16)
    rv_ref[...] = z_rest[:, 0:256]
    rg_ref[...] = z_rest[:, 256:512]
    lx_ref[...] = z_rest[:, 512:768]
    ly_ref[...] = z_rest[:, 768:1024]


def _in_projection(x_ctx, x_lat, mod, g0, win_p, tabs, qn, kvn, wuq_p, geo):
    n = geo.n
    nt = n // geo.tile
    rc, rs, mc, ms = tabs
    row = lambda w: pl.BlockSpec((geo.tile, w), lambda i: (i, 0))
    full = lambda a: pl.BlockSpec(a.shape, lambda i: (0,) * a.ndim)
    tab = lambda w: pl.BlockSpec((geo.tile, w), lambda i: (geo.rope_block(i), 0))
    outs = [(256, F32)] * 6 + [(MLA_H * LANES, BF16), (LANES, F32), (LANES, F32)]
    return pl.pallas_call(
        functools.partial(_inproj_kernel, ctx_tiles=geo.ctx_tiles),
        grid=(nt,),
        in_specs=[geo.ctx_rows(D_MODEL), geo.lat_rows(D_MODEL),
                  pl.BlockSpec((1, 6, D_MODEL), lambda i: (geo.mod_group(i), 0, 0)),
                  full(g0), full(win_p), tab(256), tab(256), tab(LANES), tab(LANES), full(qn), full(kvn), full(wuq_p)],
        out_specs=[row(w) for w, _ in outs],
        out_shape=[jax.ShapeDtypeStruct((n, w), dt) for w, dt in outs],
        compiler_params=_cparams(("arbitrary",)),
        name="in_projection",
    )(x_ctx, x_lat, mod, g0, win_p, rc, rs, mc, ms, qn, kvn, wuq_p)


def _ret_kernel(lg_ref, qf_ref, kf_ref, vf_ref, qb_ref, kb_ref, vb_ref, s0_ref,
                of_ref, ob_ref, sfin_ref, s_scr, intra_scr, cross_scr, into_scr, carry_scr):
    j = pl.program_id(1)
    c = RET_CHUNK
    row = lax.broadcasted_iota(jnp.int32, (c, c), 0)
    lane = lax.broadcasted_iota(jnp.int32, (c, c), 1)
    rowf = row.astype(F32)
    lanef = lane.astype(F32)

    @pl.when(j == 0)
    def _init():
        s_scr[...] = s0_ref[0]
        for d in range(2):
            for h in range(RET_H):
                lg = lg_ref[d, h]
                if d == 0:
                    keep = row >= lane
                    dist = rowf - lanef
                else:
                    keep = lane >= row
                    dist = lanef - rowf
                intra_scr[d, h] = jnp.where(keep, jnp.exp(jnp.where(keep, dist, 0.0) * lg), 0.0)
            for p in range(2):
                lgl = jnp.where(lane < RET_DK, lg_ref[d, 2 * p], lg_ref[d, 2 * p + 1])
                if d == 0:
                    cross_scr[d, p] = jnp.exp((rowf + 1.0) * lgl)
                    into_scr[d, p] = jnp.exp((c - 1.0 - rowf) * lgl)
                else:
                    cross_scr[d, p] = jnp.exp((c - rowf) * lgl)
                    into_scr[d, p] = jnp.exp(rowf * lgl)
                carry_scr[d, p] = jnp.exp(float(c) * lgl)

    same_head = (row < RET_DK) == (lane < RET_DK)

    chains = [(d, p, refs) for d, refs in ((0, (qf_ref, kf_ref, vf_ref, of_ref)), (1, (qb_ref, kb_ref, vb_ref, ob_ref)))
              for p in range(2)]
    staged = []
    for d, p, (q_ref, k_ref, v_ref, _) in chains:
        sl = slice(p * LANES, (p + 1) * LANES)
        q2b = q_ref[:, sl].astype(BF16)
        k2 = k_ref[:, sl]
        v2 = v_ref[:, sl]
        scores, vals = [], []
        for e in range(2):
            sel = (lane >= RET_DK) if e else (lane < RET_DK)
            ke = jnp.where(sel, k2, 0.0).astype(BF16)
            vals.append(jnp.where(sel, v2, 0.0).astype(BF16))
            scores.append(lax.dot_general(q2b, ke, (((1,), (1,)), ((), ())), preferred_element_type=F32))
        st = s_scr[d, p]
        from_state = jnp.dot(q2b, st.astype(BF16), preferred_element_type=F32)
        kw = (k2 * into_scr[d, p]).astype(BF16)
        upd = lax.dot_general(kw, v2.astype(BF16), (((0,), (0,)), ((), ())), preferred_element_type=F32)
        staged.append((scores, vals, st, from_state, upd))
    for (d, p, (_, _, _, o_ref)), (scores, vals, st, from_state, upd) in zip(chains, staged):
        o = from_state * cross_scr[d, p]
        for e in range(2):
            a = (scores[e] * intra_scr[d, 2 * p + e]).astype(BF16)
            o = o + jnp.dot(a, vals[e], preferred_element_type=F32)
        s_scr[d, p] = st * carry_scr[d, p] + jnp.where(same_head, upd, 0.0)
        o_ref[:, p * LANES:(p + 1) * LANES] = o

    @pl.when(j == pl.num_programs(1) - 1)
    def _fin():
        sfin_ref[0] = s_scr[...]


def _retention(rq, rk, rv, lg, s0, nb, t, row0):
    n = nb * t
    nc = t // RET_CHUNK
    base = row0 // RET_CHUNK
    fwd = lambda off: pl.BlockSpec((RET_CHUNK, RET_W), lambda b, j: (off + b * nc + j, 0))
    bwd = lambda off: pl.BlockSpec((RET_CHUNK, RET_W), lambda b, j: (off + b * nc + nc - 1 - j, 0))
    st = pl.BlockSpec((1, 2, 2, LANES, LANES), lambda b, j: (b, 0, 0, 0, 0))
    return pl.pallas_call(
        _ret_kernel,
        grid=(nb, nc),
        in_specs=[pl.BlockSpec(memory_space=pltpu.SMEM), fwd(base), fwd(base), fwd(base), bwd(base), bwd(base),
                  bwd(base), st],
        out_specs=[fwd(0), bwd(0), st],
        out_shape=[jax.ShapeDtypeStruct((n, RET_W), F32), jax.ShapeDtypeStruct((n, RET_W), F32),
                   jax.ShapeDtypeStruct((nb, 2, 2, LANES, LANES), F32)],
        scratch_shapes=[pltpu.VMEM((2, 2, LANES, LANES), F32), pltpu.VMEM((2, RET_H, RET_CHUNK, RET_CHUNK), F32),
                        pltpu.VMEM((2, 2, RET_CHUNK, LANES), F32), pltpu.VMEM((2, 2, RET_CHUNK, LANES), F32),
                        pltpu.VMEM((2, 2, RET_CHUNK, LANES), F32)],
        compiler_params=_cparams(("arbitrary", "arbitrary")),
        name="retention",
    )(lg, rq, rk, rv, rq, rk, rv, s0)


def _state_to_pairs(s):
    b = s.shape[0]
    s = s.reshape(b, 2, 2, 2, RET_DK, RET_DK)
    z = jnp.zeros_like(s[:, :, :, 0])
    top = jnp.concatenate([s[:, :, :, 0], z], axis=-1)
    bot = jnp.concatenate([z, s[:, :, :, 1]], axis=-1)
    return jnp.concatenate([top, bot], axis=-2)


def _pairs_to_state(s):
    b = s.shape[0]
    a = s[..., :RET_DK, :RET_DK]
    c = s[..., RET_DK:, RET_DK:]
    return jnp.stack([a, c], axis=3).reshape(b, 2, RET_H, RET_DK, RET_DK)


def _scan_rows(a, b, reverse):
    r = a.shape[0]
    rows = lax.broadcasted_iota(jnp.int32, a.shape, 0)
    s = 1
    while s < r:
        if reverse:
            a_s = pltpu.roll(a, r - s, 0)
            b_s = pltpu.roll(b, r - s, 0)
            m = rows < r - s
        else:
            a_s = pltpu.roll(a, s, 0)
            b_s = pltpu.roll(b, s, 0)
            m = rows >= s
        b = jnp.where(m, a * b_s + b, b)
        a = jnp.where(m, a * a_s, a)
        s *= 2
    return a, b


def _lru_kernel(c8_ref, cw_ref, cb_ref, wg_ref, bg_ref, lx_ref, ly_ref, h0_ref, o_ref, hfin_ref, xc_scr, hf_scr,
                *, t, r):
    nch = t // r
    cw = cw_ref[0]
    cb = cb_ref[0]
    wg = wg_ref[0]
    bg = bg_ref[0]
    c8 = c8_ref[0]

    def conv_body(c, carry):
        r0 = pl.multiple_of(c * r, r)
        cur = lx_ref[pl.ds(r0, r), :]
        prev = lx_ref[pl.ds(pl.multiple_of(jnp.maximum(r0 - 8, 0), 8), 8), :]
        nxt = lx_ref[pl.ds(pl.multiple_of(jnp.minimum(r0 + r, t - 8), 8), 8), :]
        prev = jnp.where(c > 0, prev, 0.0)
        nxt = jnp.where(c < nch - 1, nxt, 0.0)
        ext = jnp.concatenate([prev, cur, nxt], axis=0)
        acc = jnp.broadcast_to(cb, (r, LANES))
        for tap in range(4):
            sh = (2 - tap) % (r + 16)
            xs = ext if sh == 0 else pltpu.roll(ext, sh, 0)
            acc = acc + xs[8:8 + r] * cw[tap:tap + 1]
        xc_scr[pl.ds(r0, r), :] = acc
        return carry

    lax.fori_loop(0, nch, conv_body, 0)

    def gates(xc, d):
        g = jnp.dot(xc.astype(BF16), wg[:, d * 256:(d + 1) * 256], preferred_element_type=F32) + bg[:, d * 256:(d + 1) * 256]
        rg = jax.nn.sigmoid(g[:, :LANES])
        ig = jax.nn.sigmoid(g[:, LANES:])
        log_a = c8[d:d + 1] * rg
        a = jnp.exp(log_a)
        b = jnp.sqrt(jnp.tanh(-log_a) * (a * a + 1.0)) * (ig * xc)
        return a, b

    def fwd_body(c, h):
        r0 = pl.multiple_of(c * r, r)
        a, b = gates(xc_scr[pl.ds(r0, r), :], 0)
        a, b = _scan_rows(a, b, False)
        hc = a * h + b
        hf_scr[pl.ds(r0, r), :] = hc
        return hc[r - 1:r]

    h_f = lax.fori_loop(0, nch, fwd_body, h0_ref[0, 0:1, :])

    def bwd_body(i, h):
        r0 = pl.multiple_of((nch - 1 - i) * r, r)
        a, b = gates(xc_scr[pl.ds(r0, r), :], 1)
        a, b = _scan_rows(a, b, True)
        hc = a * h + b
        o_ref[pl.ds(r0, r), :] = ((hf_scr[pl.ds(r0, r), :] + hc) * jax.nn.gelu(ly_ref[pl.ds(r0, r), :])).astype(BF16)
        return hc[0:1]

    h_b = lax.fori_loop(0, nch, bwd_body, h0_ref[0, 1:2, :])
    hfin_ref[0, 0:1, :] = h_f
    hfin_ref[0, 1:2, :] = h_b


def _lru(lx, ly, h0, c8, cw, cb, wg, bg, nb, t, row0):
    base = row0 // t
    r = min(LRU_ROWS, t)
    seq = lambda off: pl.BlockSpec((t, LANES), lambda b, hh: (off + b, hh))
    par = lambda a: pl.BlockSpec((1,) + a.shape[1:], lambda b, hh: (hh,) + (0,) * (a.ndim - 1))
    st = pl.BlockSpec((1, 2, LANES), lambda b, hh: (b, 0, hh))
    return pl.pallas_call(
        functools.partial(_lru_kernel, t=t, r=r),
        grid=(nb, 2),
        in_specs=[par(c8), par(cw), par(cb), par(wg), par(bg), seq(base), seq(base), st],
        out_specs=[seq(0), st],
        out_shape=[jax.ShapeDtypeStruct((nb * t, LRU_W), BF16), jax.ShapeDtypeStruct((nb, 2, LRU_W), F32)],
        scratch_shapes=[pltpu.VMEM((t, LANES), F32), pltpu.VMEM((t, LANES), F32)],
        compiler_params=_cparams(("arbitrary", "arbitrary")),
        name="rg_lru",
    )(c8, cw, cb, wg, bg, lx, ly, h0)


def _mla_kernel(*refs, n_ctx, t, tq):
    if n_ctx:
        q_ref, ckv_ref, kr_ref, cckv_ref, ckr_ref, wk_ref, wv_ref = refs[:7]
    else:
        q_ref, ckv_ref, kr_ref, wk_ref, wv_ref = refs[:5]
    o_ref, k_scr, vt_scr, qt_scr, ot_scr = refs[-5:]
    qi = pl.program_id(1)
    cexp = (MLA_DN + MLA_DR) ** -0.5 * math.log2(math.e)
    ones_row = (lax.broadcasted_iota(jnp.int32, (MLA_H * MLA_VT_ROWS, 1), 0) % MLA_VT_ROWS == MLA_DV).astype(F32)

    def put(ckv, kr, c0):
        nrow = ckv.shape[0]
        ckv_b = ckv.astype(BF16)
        kn = jnp.dot(ckv_b, wk_ref[...], preferred_element_type=F32)
        vt = lax.dot_general(wv_ref[...], ckv_b, (((1,), (1,)), ((), ())), preferred_element_type=F32) + ones_row
        for h in range(MLA_H):
            k_scr[h, c0:c0 + nrow, :] = ((kn[:, h * LANES:(h + 1) * LANES] + kr) * cexp).astype(BF16)
            vt_scr[h, :, c0:c0 + nrow] = vt[h * MLA_VT_ROWS:(h + 1) * MLA_VT_ROWS].astype(BF16)

    @pl.when(qi == 0)
    def _build():
        if n_ctx:
            put(cckv_ref[0], ckr_ref[0], 0)
        step = min(KV_BUILD_ROWS, t)
        for c in range(t // step):
            put(ckv_ref[c * step:(c + 1) * step, :], kr_ref[c * step:(c + 1) * step, :], n_ctx + c * step)

    qt_scr[...] = q_ref[...].astype(F32).T.astype(BF16)
    s_len = n_ctx + t

    chunks = [slice(c0, min(c0 + ATT_KC, s_len)) for c0 in range(0, s_len, ATT_KC)]
    nchunk = len(chunks)
    steps = [(h, c) for h in range(MLA_H) for c in range(nchunk)]

    def score(i):
        h, c = steps[i]
        return jnp.dot(k_scr[h, chunks[c], :], qt_scr[h * LANES:(h + 1) * LANES, :], preferred_element_type=F32)

    pending = {i: score(i) for i in range(min(ATT_AHEAD, len(steps)))}
    held = None
    m = o = None
    for i in range(len(steps) + 1):
        if i + ATT_AHEAD < len(steps):
            pending[i + ATT_AHEAD] = score(i + ATT_AHEAD)
        if i < len(steps):
            h, c = steps[i]
            s = pending.pop(i)
            m_old = jnp.full((1, tq), -1e30, F32) if c == 0 else m
            m = jnp.maximum(m_old, jnp.max(s, axis=0, keepdims=True))
            p = jnp.exp2(s - m).astype(BF16)
            alpha = jnp.exp2(m_old - m)
        if held is not None:
            hh, cc, p_h, alpha_h = held
            pv = jnp.dot(vt_scr[hh, :, chunks[cc]], p_h, preferred_element_type=F32)
            o = pv if cc == 0 else o * alpha_h + pv
            if cc == nchunk - 1:
                ot_scr[hh] = o[:MLA_DV] / o[MLA_DV:MLA_DV + 1]
        held = (h, c, p, alpha) if i < len(steps) else None
    for pp in range(MLA_H // 2):
        pair = jnp.concatenate([ot_scr[2 * pp], ot_scr[2 * pp + 1]], axis=0)
        o_ref[:, pp * LANES:(pp + 1) * LANES] = pair.T.astype(BF16)


def _mla(q, ckvn, kr, cache, wk, wv_t, nb, t, row0):
    tq = min(ATT_TQ, t)
    nq = t // tq
    n_ctx = 0 if cache is None else cache[0].shape[1]
    s_len = n_ctx + t
    qspec = pl.BlockSpec((tq, MLA_H * LANES), lambda b, i: (row0 // tq + b * nq + i, 0))
    seq = pl.BlockSpec((t, LANES), lambda b, i: (row0 // t + b, 0))
    full = lambda a: pl.BlockSpec(a.shape, lambda b, i: (0,) * a.ndim)
    ins = [q, ckvn, kr]
    specs = [qspec, seq, seq]
    if n_ctx:
        cspec = pl.BlockSpec((1, n_ctx, LANES), lambda b, i: (b, 0, 0))
        ins += [cache[0], cache[1]]
        specs += [cspec, cspec]
    ins += [wk, wv_t]
    specs += [full(wk), full(wv_t)]
    return pl.pallas_call(
        functools.partial(_mla_kernel, n_ctx=n_ctx, t=t, tq=tq),
        grid=(nb, nq),
        in_specs=specs,
        out_specs=pl.BlockSpec((tq, MLA_W), lambda b, i: (b * nq + i, 0)),
        out_shape=jax.ShapeDtypeStruct((nb * t, MLA_W), BF16),
        scratch_shapes=[pltpu.VMEM((MLA_H, s_len, LANES), BF16), pltpu.VMEM((MLA_H, MLA_VT_ROWS, s_len), BF16),
                        pltpu.VMEM((MLA_H * LANES, tq), BF16), pltpu.VMEM((MLA_H, MLA_DV, tq), F32)],
        compiler_params=_cparams(("arbitrary", "arbitrary")),
        name="mla_attention",
    )(*ins)


def _outproj_kernel(ofc_ref, obc_ref, lruc_ref, mlac_ref, xc_ref, ofl_ref, obl_ref, lrul_ref, mlal_ref, xl_ref, rg_ref,
                    mod_ref, retn_ref, g1_ref, g2_ref, wout_ref, rw_ref, rb_ref, x1_ref, h2_ref, route_ref, cnt_ref,
                    cnt_scr, *, ctx_tiles):
    is_ctx = pl.program_id(0) < ctx_tiles
    pick = lambda c_ref, l_ref: jnp.where(is_ctx, c_ref[...], l_ref[...])
    o = pick(ofc_ref, ofl_ref) + pick(obc_ref, obl_ref)
    hid = lax.broadcasted_iota(jnp.int32, o.shape, 1) // RET_DK

    def head_sum(v):
        tot = jnp.zeros_like(v)
        for hh in range(RET_H):
            msk = hid == hh
            tot = jnp.where(msk, jnp.sum(jnp.where(msk, v, 0.0), axis=1, keepdims=True), tot)
        return tot

    mu = head_sum(o) * (1.0 / RET_DK)
    dl = o - mu
    var = head_sum(dl * dl) * (1.0 / RET_DK)
    rg = rg_ref[...]
    ret = dl * lax.rsqrt(var + EPS) * retn_ref[...] * (rg * jax.nn.sigmoid(rg))
    mix = jnp.concatenate([ret.astype(BF16), pick(lruc_ref, lrul_ref), pick(mlac_ref, mlal_ref)], axis=1)
    mo = jnp.dot(mix, wout_ref[...], preferred_element_type=F32)
    mod = mod_ref[0]
    x1 = pick(xc_ref, xl_ref) + mod[2:3] * _rms(mo, g1_ref[...])
    x1_ref[...] = x1
    h2 = _rms(x1, g2_ref[...]) * (1.0 + mod[4:5]) + mod[3:4]
    _store_row_tiles(h2_ref, h2)

    tm = h2.shape[0]
    lg = jnp.dot(h2.astype(BF16), rw_ref[...], preferred_element_type=F32) + rb_ref[...]
    lgt = lg.T[:N_EXPERTS, :]
    rowf = lax.broadcasted_iota(jnp.int32, (N_EXPERTS, tm), 0).astype(F32)
    tops, idxs, hots = [], [], []
    for _ in range(TOP_K):
        m = jnp.max(lgt, axis=0, keepdims=True)
        idx = jnp.min(jnp.where(lgt == m, rowf, float(N_EXPERTS)), axis=0, keepdims=True)
        hot = rowf == idx
        lgt = jnp.where(hot, -jnp.inf, lgt)
        tops.append(m)
        idxs.append(idx)
        hots.append(hot)
    exps = [jnp.exp(t - tops[0]) for t in tops]
    den = exps[0] + exps[1] + exps[2] + exps[3]
    member = jnp.zeros((N_EXPERTS, tm), F32)
    for hot in hots:
        member = member + hot.astype(F32)

    @pl.when(pl.program_id(0) == 0)
    def _zero_counts():
        cnt_scr[...] = jnp.zeros_like(cnt_scr)

    earlier = (lax.broadcasted_iota(jnp.int32, (tm, tm), 0) < lax.broadcasted_iota(jnp.int32, (tm, tm), 1))
    counts = cnt_scr[...]
    before = jnp.dot(member.astype(BF16), earlier.astype(BF16), preferred_element_type=F32) + counts[:, 0:1]
    ranks = [jnp.sum(jnp.where(hot, before, 0.0), axis=0, keepdims=True) for hot in hots]
    gates = [e / den for e in exps]
    route_ref[0] = jnp.concatenate(idxs + gates + ranks + [jnp.zeros((TOP_K, tm), F32)], axis=0)
    counts = counts + jnp.sum(member, axis=1, keepdims=True)
    cnt_scr[...] = counts
    cnt_ref[...] = counts


def _out_projection(ctx_mix, lat_mix, rg, mod, retn, g1, g2, wout, rw, rb, geo):
    n = geo.n
    row = lambda w: pl.BlockSpec((ROW_TILE, w), lambda i: (i, 0))
    full = lambda a: pl.BlockSpec(a.shape, lambda i: (0,) * a.ndim)
    widths = (256, 256, 256, MLA_W, D_MODEL)
    return pl.pallas_call(
        functools.partial(_outproj_kernel, ctx_tiles=geo.ctx_tiles),
        grid=(n // ROW_TILE,),
        in_specs=[geo.ctx_rows(w) for w in widths] + [geo.lat_rows(w) for w in widths] + [
                  row(256),
                  pl.BlockSpec((1, 6, D_MODEL), lambda i: (geo.mod_group(i), 0, 0)),
                  full(retn), full(g1), full(g2), full(wout), full(rw), full(rb)],
        out_specs=[row(D_MODEL), pl.BlockSpec((ROW_TILE * ROW_CHUNKS, LANES), lambda i: (i, 0)),
                   pl.BlockSpec((1, 4 * TOP_K, ROW_TILE), lambda i: (i, 0, 0)),
                   pl.BlockSpec((N_EXPERTS, LANES), lambda i: (0, 0))],
        out_shape=[jax.ShapeDtypeStruct((n, D_MODEL), F32), jax.ShapeDtypeStruct((n * ROW_CHUNKS, LANES), F32),
                   jax.ShapeDtypeStruct((n // ROW_TILE, 4 * TOP_K, ROW_TILE), F32),
                   jax.ShapeDtypeStruct((N_EXPERTS, LANES), F32)],
        scratch_shapes=[pltpu.VMEM((N_EXPERTS, LANES), F32)],
        compiler_params=_cparams(("arbitrary",)),
        name="out_projection",
    )(*ctx_mix, *lat_mix, rg, mod, retn, g1, g2, wout, rw, rb)


def _dispatch_kernel(pe_ref, slot_ref, h_ref, xs_hbm, zero_scr, sem):
    i = pl.program_id(0)
    n_slots = xs_hbm.shape[0] // ROW_CHUNKS

    def slot_rows(first_slot, n):
        return xs_hbm.at[pl.ds(pl.multiple_of(first_slot * ROW_CHUNKS, ROW_CHUNKS), n * ROW_CHUNKS), :]

    @pl.when(i == 0)
    def _zero_pads():
        zero_scr[...] = jnp.zeros_like(zero_scr)

        def fill(e):
            end = pe_ref[e]
            start = 0 if e == 0 else pe_ref[e - 1]
            return end > start, pltpu.make_async_copy(zero_scr, slot_rows(jnp.maximum(end - MOE_TILE, 0), MOE_TILE), sem)

        def fill_tail(j):
            row = pe_ref[N_EXPERTS - 1] + j * MOE_TILE
            dst = slot_rows(jnp.minimum(row, n_slots - MOE_TILE), MOE_TILE)
            return row < n_slots, pltpu.make_async_copy(zero_scr, dst, sem)

        for e in range(N_EXPERTS):
            for todo, cp in (fill(e), fill_tail(e)):
                pl.when(todo)(cp.start)
        for e in range(N_EXPERTS):
            for todo, cp in (fill(e), fill_tail(e)):
                pl.when(todo)(cp.wait)

    for r in range(ROW_TILE):
        for k in range(TOP_K):
            first = pl.multiple_of(slot_ref[0, 0, k * ROW_TILE + r], ROW_CHUNKS)
            dst = xs_hbm.at[pl.ds(first, ROW_CHUNKS), :]
            pltpu.make_async_copy(h_ref.at[pl.ds(r * ROW_CHUNKS, ROW_CHUNKS), :], dst, sem).start(priority=k % 2)
    for k in range(TOP_K):
        pltpu.make_async_copy(h_ref, xs_hbm.at[pl.ds(0, ROW_TILE * ROW_CHUNKS), :], sem).wait()


def _dispatch(h2, slots3, pad_end, n_slots):
    n = h2.shape[0] // ROW_CHUNKS
    grid_spec = pltpu.PrefetchScalarGridSpec(
        num_scalar_prefetch=1,
        grid=(n // ROW_TILE,),
        in_specs=[
            pl.BlockSpec((1, 1, TOP_K * ROW_TILE), lambda i, pe: (i, 0, 0), memory_space=pltpu.SMEM),
            pl.BlockSpec((ROW_TILE * ROW_CHUNKS, LANES), lambda i, pe: (i, 0)),
        ],
        out_specs=pl.BlockSpec(memory_space=pl.ANY),
        scratch_shapes=[pltpu.VMEM((MOE_TILE * ROW_CHUNKS, LANES), F32), pltpu.SemaphoreType.DMA(())],
    )
    return pl.pallas_call(
        _dispatch_kernel,
        grid_spec=grid_spec,
        out_shape=jax.ShapeDtypeStruct((n_slots * ROW_CHUNKS, LANES), F32),
        compiler_params=_cparams(("arbitrary",)),
        name="moe_dispatch",
    )(pad_end, slots3, h2)


def _moe_kernel(te_ref, nu_ref, x_ref, wgu_ref, bgu_ref, wd_ref, bd_ref, y_ref, wgu_b, wd_b):
    t = pl.program_id(0)
    n_used = nu_ref[0]
    nch = D_EXPERT // MOE_HCHUNK
    new_expert = jnp.logical_or(t == 0, te_ref[t] != te_ref[jnp.maximum(t - 1, 0)])

    @pl.when(jnp.logical_and(new_expert, t < n_used))
    def _cast_weights():
        for j in range(nch):
            cols = slice(j * MOE_HCHUNK, (j + 1) * MOE_HCHUNK)
            ucols = slice(D_EXPERT + j * MOE_HCHUNK, D_EXPERT + (j + 1) * MOE_HCHUNK)
            wgu_b[j, :, :MOE_HCHUNK] = wgu_ref[0, 0, :, cols].astype(BF16)
            wgu_b[j, :, MOE_HCHUNK:] = wgu_ref[0, 0, :, ucols].astype(BF16)
        wd_b[...] = wd_ref[0, 0].astype(BF16)

    @pl.when(t < n_used)
    def _compute():
        x = _load_row_tiles(x_ref).astype(BF16)
        bgu = bgu_ref[pl.ds(te_ref[t], 1), :]

        def gate_up(j):
            return jnp.dot(x, wgu_b[j], preferred_element_type=F32)

        pending = gate_up(0)
        y = None
        for j in range(nch):
            gu = pending
            if j + 1 < nch:
                pending = gate_up(j + 1)
            cols = slice(j * MOE_HCHUNK, (j + 1) * MOE_HCHUNK)
            ucols = slice(D_EXPERT + j * MOE_HCHUNK, D_EXPERT + (j + 1) * MOE_HCHUNK)
            g = jnp.minimum(gu[:, :MOE_HCHUNK] + bgu[:, cols], SWIGLU_LIMIT)
            u = jnp.clip(gu[:, MOE_HCHUNK:] + bgu[:, ucols], -SWIGLU_LIMIT, SWIGLU_LIMIT)
            act = (g * jax.nn.sigmoid(SWIGLU_ALPHA * g) * (u + 1.0)).astype(BF16)
            part = jnp.dot(act, wd_b[cols, :], preferred_element_type=F32)
            y = part if y is None else y + part
        _store_row_tiles(y_ref, y + bd_ref[pl.ds(te_ref[t], 1), :])

    @pl.when(t >= n_used)
    def _idle():
        y_ref[...] = jnp.zeros_like(y_ref)


def _moe(xs, tile_expert, n_used, layer, wgu, bgu, wd, bd):
    n_tiles = tile_expert.shape[0]
    wspec = lambda r, c: pl.BlockSpec((1, 1, r, c), lambda t, te, nu: (layer, te[t], 0, 0))
    bspec = lambda c: pl.BlockSpec((N_EXPERTS, c), lambda t, te, nu: (0, 0))
    grid_spec = pltpu.PrefetchScalarGridSpec(
        num_scalar_prefetch=2,
        grid=(n_tiles,),
        in_specs=[
            pl.BlockSpec((MOE_TILE * ROW_CHUNKS, LANES),
                         lambda t, te, nu: (jnp.minimum(t, jnp.maximum(nu[0] - 1, 0)), 0)),
            wspec(D_MODEL, 2 * D_EXPERT), bspec(2 * D_EXPERT), wspec(D_EXPERT, D_MODEL), bspec(D_MODEL),
        ],
        out_specs=pl.BlockSpec((MOE_TILE * ROW_CHUNKS, LANES), lambda t, te, nu: (t, 0)),
        scratch_shapes=[pltpu.VMEM((D_EXPERT // MOE_HCHUNK, D_MODEL, 2 * MOE_HCHUNK), BF16),
                        pltpu.VMEM((D_EXPERT, D_MODEL), BF16)],
    )
    return pl.pallas_call(
        _moe_kernel,
        grid_spec=grid_spec,
        out_shape=jax.ShapeDtypeStruct((n_tiles * MOE_TILE * ROW_CHUNKS, LANES), F32),
        compiler_params=_cparams(("arbitrary",)),
        name="moe_experts",
    )(tile_expert, n_used, xs, wgu, bgu[layer], wd, bd[layer])


def _combine_kernel(inv_ref, invn_ref, y_hbm, gate_ref, x1_ref, mod_ref, g3_ref, oc_ref, ol_ref, buf, sem, *,
                    ctx_tiles):
    i = pl.program_id(0)
    cur = lax.rem(i, 2)

    def gather(idx_ref, half):
        for r in range(ROW_TILE):
            for k in range(TOP_K):
                first = pl.multiple_of(idx_ref[0, 0, k * ROW_TILE + r], ROW_CHUNKS)
                pltpu.make_async_copy(y_hbm.at[pl.ds(first, ROW_CHUNKS), :],
                                      buf.at[half, k, pl.ds(r * ROW_CHUNKS, ROW_CHUNKS), :],
                                      sem.at[half]).start(priority=k % 2)

    @pl.when(i == 0)
    def _first():
        gather(inv_ref, 0)

    has_next = i + 1 < pl.num_programs(0)
    for half in range(2):
        pl.when(jnp.logical_and(has_next, cur != half))(functools.partial(gather, invn_ref, half))

    for k in range(TOP_K):
        pltpu.make_async_copy(y_hbm.at[pl.ds(0, ROW_TILE * ROW_CHUNKS), :], buf.at[cur, k], sem.at[cur]).wait()
    gate = gate_ref[...]
    ff = gate[:, 0:1] * _load_row_tiles(buf.at[cur, 0])
    for k in range(1, TOP_K):
        ff = ff + gate[:, k:k + 1] * _load_row_tiles(buf.at[cur, k])
    out = x1_ref[...] + mod_ref[0][5:6] * _rms(ff, g3_ref[...])
    is_ctx = pl.program_id(0) < ctx_tiles

    @pl.when(is_ctx)
    def _ctx():
        oc_ref[...] = out

    @pl.when(jnp.logical_not(is_ctx))
    def _lat():
        ol_ref[...] = out


def _combine(y_sorted, inv3, gate, x1, mod, g3, geo):
    n = x1.shape[0]
    nt = n // ROW_TILE
    return pl.pallas_call(
        functools.partial(_combine_kernel, ctx_tiles=geo.ctx_tiles),
        grid=(nt,),
        in_specs=[
            pl.BlockSpec((1, 1, TOP_K * ROW_TILE), lambda i: (i, 0, 0), memory_space=pltpu.SMEM),
            pl.BlockSpec((1, 1, TOP_K * ROW_TILE), lambda i: (jnp.minimum(i + 1, nt - 1), 0, 0),
                         memory_space=pltpu.SMEM),
            pl.BlockSpec(memory_space=pl.ANY),
            pl.BlockSpec((ROW_TILE, TOP_K), lambda i: (i, 0)),
            pl.BlockSpec((ROW_TILE, D_MODEL), lambda i: (i, 0)),
            pl.BlockSpec((1, 6, D_MODEL), lambda i: (geo.mod_group(i), 0, 0)),
            pl.BlockSpec((1, D_MODEL), lambda i: (0, 0)),
        ],
        out_specs=[geo.ctx_rows(D_MODEL), geo.lat_rows(D_MODEL)],
        out_shape=[jax.ShapeDtypeStruct((geo.n_ctx, D_MODEL), F32), jax.ShapeDtypeStruct((n - geo.n_ctx, D_MODEL), F32)],
        scratch_shapes=[pltpu.VMEM((2, TOP_K, ROW_TILE * ROW_CHUNKS, LANES), F32), pltpu.SemaphoreType.DMA((2,))],
        compiler_params=_cparams(("arbitrary",)),
        name="moe_combine",
    )(inv3, inv3, y_sorted, gate, x1, mod, g3)


def _plan(route, counts):
    nt = route.shape[0]
    n = nt * ROW_TILE
    expert = route[:, 0:TOP_K, :].astype(jnp.int32)
    gate = route[:, TOP_K:2 * TOP_K, :].transpose(0, 2, 1).reshape(n, TOP_K)
    rank = route[:, 2 * TOP_K:3 * TOP_K, :].astype(jnp.int32)
    cnt = counts[:, 0].astype(jnp.int32)
    padded = (cnt + MOE_TILE - 1) // MOE_TILE * MOE_TILE
    pad_end = jnp.cumsum(padded).astype(jnp.int32)
    pad_start = pad_end - padded
    start_of = jnp.sum(jnp.where(expert[..., None] == jnp.arange(N_EXPERTS), pad_start, 0), axis=-1)
    slots3 = ((start_of + rank) * ROW_CHUNKS).reshape(nt, 1, TOP_K * ROW_TILE)
    n_tiles = n * TOP_K // MOE_TILE + N_EXPERTS
    tile_start = jnp.arange(n_tiles, dtype=jnp.int32) * MOE_TILE
    tile_expert = jnp.minimum(jnp.sum((tile_start[:, None] >= pad_end[None, :]).astype(jnp.int32), axis=1),
                              N_EXPERTS - 1)
    n_used = (pad_end[-1] // MOE_TILE).reshape(1)
    return gate, slots3, pad_end, tile_expert, n_used, n_tiles * MOE_TILE


class _Geometry:
    def __init__(self, nb_ctx, t_ctx, nb_lat, t_lat, tile):
        self.nb_ctx, self.t_ctx, self.nb_lat, self.t_lat, self.tile = nb_ctx, t_ctx, nb_lat, t_lat, tile
        self.n_ctx = nb_ctx * t_ctx
        self.n = self.n_ctx + nb_lat * t_lat
        self.ctx_tiles = self.n_ctx // tile
        self.lat_tiles = t_lat // tile

    def mod_group(self, i):
        return jnp.where(i < self.ctx_tiles, 0, 1 + (i - self.ctx_tiles) // self.lat_tiles)

    def ctx_rows(self, w):
        return pl.BlockSpec((self.tile, w), lambda i: (jnp.minimum(i, self.ctx_tiles - 1), 0))

    def lat_rows(self, w):
        return pl.BlockSpec((self.tile, w), lambda i: (jnp.maximum(i - self.ctx_tiles, 0), 0))

    def rope_block(self, i):
        return jnp.where(i < self.ctx_tiles, self.lat_tiles, (i - self.ctx_tiles) % self.lat_tiles)


def _rope_tables(t, tile):
    pos = jnp.arange(t)
    row = (pos // GRID_W).astype(F32)
    col = (pos % GRID_W).astype(F32)

    def cs(dim):
        q = dim // 4
        inv = ROPE_BASE ** (-jnp.arange(q, dtype=F32) / q)
        ar = row[:, None] * inv
        ac = col[:, None] * inv
        c = jnp.concatenate([jnp.cos(ar), jnp.cos(ar), jnp.cos(ac), jnp.cos(ac)], axis=-1)
        s = jnp.concatenate([-jnp.sin(ar), jnp.sin(ar), -jnp.sin(ac), jnp.sin(ac)], axis=-1)
        return c, s

    def with_identity(c, s):
        return (jnp.concatenate([c, jnp.ones((tile, c.shape[1]), F32)], axis=0),
                jnp.concatenate([s, jnp.zeros((tile, s.shape[1]), F32)], axis=0))

    c64, s64 = cs(RET_DK)
    rc, rs = with_identity(jnp.tile(c64, (1, RET_H)), jnp.tile(s64, (1, RET_H)))
    c32, s32 = cs(MLA_DR)
    pad = lambda a, v: jnp.concatenate([jnp.full((t, MLA_DN), v, F32), a, jnp.full((t, LANES - MLA_DN - MLA_DR), v, F32)], axis=-1)
    mc, ms = with_identity(pad(c32, 1.0), pad(s32, 0.0))
    return rc, rs, mc, ms


def _layer_params(l, norm_g, w_in, ret_decay, ret_norm, conv_w, conv_b, lru_gate_w, lru_gate_b, lru_lambda,
                  mla_q_norm, mla_kv_norm, mla_w_uq, mla_w_ukv, w_out, router_w, router_b):
    p = {}
    p['g'] = [norm_g[l, i].reshape(1, D_MODEL) for i in range(4)]
    kr0 = 1920
    p['win'] = jnp.concatenate([w_in[l][:, :kr0], jnp.zeros((D_MODEL, MLA_DN), F32), w_in[l][:, kr0:],
                                jnp.zeros((D_MODEL, LANES - MLA_DN - MLA_DR), F32)], axis=1).astype(BF16)
    p['lg'] = jax.nn.log_sigmoid(ret_decay[l].astype(F32))
    p['retn'] = ret_norm[l].reshape(1, RET_W)
    p['qn'] = mla_q_norm[l].reshape(1, Q_RANK)
    p['kvn'] = mla_kv_norm[l].reshape(1, KV_RANK)
    wuq = mla_w_uq[l].reshape(Q_RANK, MLA_H, MLA_DN + MLA_DR)
    p['wuq'] = jnp.pad(wuq, ((0, 0), (0, 0), (0, LANES - MLA_DN - MLA_DR))).reshape(Q_RANK, MLA_H * LANES).astype(BF16)
    wukv = mla_w_ukv[l].reshape(KV_RANK, MLA_H, MLA_DN + MLA_DV)
    p['wk'] = jnp.pad(wukv[:, :, :MLA_DN], ((0, 0), (0, 0), (0, LANES - MLA_DN))).reshape(KV_RANK, MLA_H * LANES).astype(BF16)
    wv_t = jnp.pad(wukv[:, :, MLA_DN:].transpose(1, 2, 0), ((0, 0), (0, MLA_VT_ROWS - MLA_DV), (0, 0)))
    p['wv_t'] = wv_t.reshape(MLA_H * MLA_VT_ROWS, KV_RANK).astype(BF16)
    p['wout'] = w_out[l].astype(BF16)
    p['rw'] = jnp.pad(router_w[l], ((0, 0), (0, LANES - N_EXPERTS))).astype(BF16)
    p['rb'] = jnp.pad(router_b[l], (0, LANES - N_EXPERTS)).reshape(1, LANES)
    p['cw'] = conv_w[l].reshape(4, 2, LANES).transpose(1, 0, 2)
    p['cb'] = conv_b[l].reshape(2, 1, LANES)
    gw = lru_gate_w[l]
    halves = []
    for hh in range(2):
        cols = []
        for d in range(2):
            for g in range(2):
                blk = jnp.zeros((LANES, LANES), F32)
                for j in range(2):
                    blk = blk.at[j * LRU_BW:(j + 1) * LRU_BW, j * LRU_BW:(j + 1) * LRU_BW].set(gw[d, g, 2 * hh + j])
                cols.append(blk)
        halves.append(jnp.concatenate(cols, axis=1))
    p['wg'] = jnp.stack(halves).astype(BF16)
    gb = lru_gate_b[l].reshape(2, 2, 2, LANES)
    p['bg'] = gb.transpose(2, 0, 1, 3).reshape(2, 1, 4 * LANES)
    p['c8'] = (8.0 * jax.nn.log_sigmoid(lru_lambda[l].astype(F32))).reshape(2, 2, LANES).transpose(1, 0, 2)
    return p


def _forward(x_prompt, x_sample, c, state_ret, state_lru, cache_mla_ckv, cache_mla_krope, c_ctx, ada_w, ada_b,
             *weights):
    nb_ctx, t_ctx, _ = x_prompt.shape
    nb_lat, t_lat, _ = x_sample.shape
    geo = _Geometry(nb_ctx, t_ctx, nb_lat, t_lat, ROW_TILE)
    geo_in = _Geometry(nb_ctx, t_ctx, nb_lat, t_lat, IN_TILE)
    n_c = geo.n_ctx
    x_ctx = x_prompt.reshape(n_c, D_MODEL)
    x_lat = x_sample.reshape(-1, D_MODEL)
    cond = jnp.concatenate([c_ctx[None, :], c, jnp.zeros((16 - 1 - nb_lat, D_MODEL), F32)], axis=0)
    mod_all = _modulation(cond, ada_w, ada_b)[:, :1 + nb_lat].reshape(DEPTH, 1 + nb_lat, 6, D_MODEL)
    tabs = _rope_tables(t_lat, IN_TILE)
    krope_pad = jnp.pad(cache_mla_krope, ((0, 0), (0, 0), (0, 0), (MLA_DN, LANES - MLA_DN - MLA_DR)))
    ret_out, lru_out, ckv_out, kr_out = [], [], [], []
    for l in range(DEPTH):
        p = _layer_params(l, *weights[:-4])
        mod = mod_all[l]
        rq, rk, rv, rg, lx, ly, q, ckvn, kr = _in_projection(x_ctx, x_lat, mod, p['g'][0], p['win'], tabs, p['qn'],
                                                             p['kvn'], p['wuq'], geo_in)
        zero_s = jnp.zeros((nb_ctx, 2, 2, LANES, LANES), F32)
        ofc, obc, s_ctx = _retention(rq, rk, rv, p['lg'], zero_s, nb_ctx, t_ctx, 0)
        ofl, obl, _ = _retention(rq, rk, rv, p['lg'], _state_to_pairs(state_ret[:, l]), nb_lat, t_lat, n_c)
        lru_c, h_ctx = _lru(lx, ly, jnp.zeros((nb_ctx, 2, LRU_W), F32), p['c8'], p['cw'], p['cb'], p['wg'], p['bg'],
                            nb_ctx, t_ctx, 0)
        lru_l, _ = _lru(lx, ly, state_lru[:, l], p['c8'], p['cw'], p['cb'], p['wg'], p['bg'], nb_lat, t_lat, n_c)
        mla_c = _mla(q, ckvn, kr, None, p['wk'], p['wv_t'], nb_ctx, t_ctx, 0)
        mla_l = _mla(q, ckvn, kr, (cache_mla_ckv[:, l], krope_pad[:, l]), p['wk'], p['wv_t'], nb_lat, t_lat, n_c)
        x1, h2, route, counts = _out_projection((ofc, obc, lru_c, mla_c, x_ctx), (ofl, obl, lru_l, mla_l, x_lat), rg,
                                                mod, p['retn'], p['g'][1], p['g'][2], p['wout'], p['rw'], p['rb'], geo)
        gate, slots3, pad_end, tile_expert, n_used, n_slots = _plan(route, counts)
        xs = _dispatch(h2, slots3, pad_end, n_slots)
        y_sorted = _moe(xs, tile_expert, n_used, l, *weights[-4:])
        x_ctx, x_lat = _combine(y_sorted, slots3, gate, x1, mod, p['g'][3], geo)
        ret_out.append(_pairs_to_state(s_ctx))
        lru_out.append(h_ctx)
        ckv_out.append(ckvn[:n_c].reshape(nb_ctx, t_ctx, KV_RANK))
        kr_out.append(kr[:n_c, MLA_DN:MLA_DN + MLA_DR].reshape(nb_ctx, t_ctx, MLA_DR))
    y_prompt = x_ctx.reshape(nb_ctx, t_ctx, D_MODEL)
    y_sample = x_lat.reshape(nb_lat, t_lat, D_MODEL)
    return (y_prompt, y_sample, jnp.stack(ret_out, axis=1), jnp.stack(lru_out, axis=1),
            jnp.stack(ckv_out, axis=1), jnp.stack(kr_out, axis=1))


def kernel(x_prompt, x_sample, c, state_ret, state_lru, cache_mla_ckv, cache_mla_krope, c_ctx, ada_w, ada_b, norm_g, w_in, ret_decay, ret_norm, conv_w, conv_b, lru_gate_w, lru_gate_b, lru_lambda, mla_q_norm, mla_kv_norm, mla_w_uq, mla_w_ukv, w_out, router_w, router_b, moe_w_gu, moe_b_gu, moe_w_down, moe_b_down):
    return _forward(x_prompt, x_sample, c, state_ret, state_lru, cache_mla_ckv, cache_mla_krope, c_ctx, ada_w, ada_b,
                    norm_g, w_in, ret_decay, ret_norm, conv_w, conv_b, lru_gate_w, lru_gate_b, lru_lambda,
                    mla_q_norm, mla_kv_norm, mla_w_uq, mla_w_ukv, w_out, router_w, router_b,
                    moe_w_gu, moe_b_gu, moe_w_down, moe_b_down)
```

```python
import functools
import math

import jax
import jax.numpy as jnp
from jax import lax
from jax.experimental import pallas as pl
from jax.experimental.pallas import tpu as pltpu

F32 = jnp.float32
BF16 = jnp.bfloat16

D_MODEL = 1024
DEPTH = 2
GRID_W = 64
RET_H = 4
RET_DK = 64
RET_W = 256
RET_CHUNK = 128
RET_STEP_CHUNKS = 2
LRU_W = 256
LRU_BLOCKS = 4
LRU_BW = 64
MLA_H = 8
MLA_DN = 64
MLA_DR = 32
MLA_DV = 64
MLA_W = MLA_H * MLA_DV
MLA_VT_ROWS = 80
Q_RANK = 256
KV_RANK = 128
ROPE_BASE = 10000.0
N_EXPERTS = 32
TOP_K = 4
D_EXPERT = 1024
SWIGLU_ALPHA = 1.702
SWIGLU_LIMIT = 7.0
EPS = 1e-6

LANES = 128
ROW_TILE = 256
IN_TILE = 512
MOE_TILE = 512
MOE_HCHUNK = 256
LRU_ROWS = 256
ATT_TQ = 256
ATT_KC = 256
ATT_AHEAD = 4
KV_BUILD_ROWS = 512
PROJ_PAD = 2048
VMEM_LIMIT = 56 * 1024 * 1024


def _cparams(sem):
    return pltpu.CompilerParams(dimension_semantics=sem, vmem_limit_bytes=VMEM_LIMIT)


def _rms(x, g):
    return x * lax.rsqrt(jnp.mean(x * x, axis=-1, keepdims=True) + EPS) * g


ROW_CHUNKS = D_MODEL // LANES


def _store_row_tiles(ref, x):
    for j in range(ROW_CHUNKS):
        ref[pl.ds(j, x.shape[0], stride=ROW_CHUNKS), :] = x[:, j * LANES:(j + 1) * LANES]


def _load_row_tiles(ref):
    rows = ref.shape[0] // ROW_CHUNKS
    return jnp.concatenate([ref[pl.ds(j, rows, stride=ROW_CHUNKS), :] for j in range(ROW_CHUNKS)], axis=1)


def _rope(x, c, s, quarter):
    w = x.shape[-1]
    lane = lax.broadcasted_iota(jnp.int32, x.shape, 1)
    first = (lane % (2 * quarter)) < quarter
    partner = jnp.where(first, pltpu.roll(x, w - quarter, 1), pltpu.roll(x, quarter, 1))
    return x * c + partner * s


def _mod_kernel(cond_ref, w_ref, b_ref, o_ref):
    c = cond_ref[...]
    s = (c * jax.nn.sigmoid(c)).astype(BF16)
    o_ref[0] = jnp.dot(s, w_ref[0].astype(BF16), preferred_element_type=F32) + b_ref[0]


def _modulation(cond, ada_w, ada_b):
    nblk = 6
    return pl.pallas_call(
        _mod_kernel,
        grid=(DEPTH, nblk),
        in_specs=[
            pl.BlockSpec((16, D_MODEL), lambda l, j: (0, 0)),
            pl.BlockSpec((1, D_MODEL, D_MODEL), lambda l, j: (l, 0, j)),
            pl.BlockSpec((1, 1, D_MODEL), lambda l, j: (l, 0, j)),
        ],
        out_specs=pl.BlockSpec((1, 16, D_MODEL), lambda l, j: (l, 0, j)),
        out_shape=jax.ShapeDtypeStruct((DEPTH, 16, 6 * D_MODEL), F32),
        compiler_params=_cparams(("arbitrary", "arbitrary")),
        name="adaln_mod",
    )(cond, ada_w, ada_b.reshape(DEPTH, 1, 6 * D_MODEL))


def _inproj_kernel(xc_ref, xl_ref, mod_ref, g0_ref, win_ref, rc_ref, rs_ref, mc_ref, ms_ref, qn_ref, kvn_ref, wuq_ref,
                   rq_ref, rk_ref, rv_ref, rg_ref, lx_ref, ly_ref, q_ref, ckv_ref, kr_ref, *, ctx_tiles):
    x = jnp.where(pl.program_id(0) < ctx_tiles, xc_ref[...], xl_ref[...])
    mod = mod_ref[0]
    h = _rms(x, g0_ref[...]) * (1.0 + mod[1:2]) + mod[0:1]
    hb = h.astype(BF16)
    z_mla = jnp.dot(hb, win_ref[:, 1536:2048], preferred_element_type=F32)
    z_qk = jnp.dot(hb, win_ref[:, 0:512], preferred_element_type=F32)
    cqn = _rms(z_mla[:, 0:256], qn_ref[...])
    q = jnp.dot(cqn.astype(BF16), wuq_ref[...], preferred_element_type=F32)
    z_rest = jnp.dot(hb, win_ref[:, 512:1536], preferred_element_type=F32)
    rc = rc_ref[...]
    rs = rs_ref[...]
    rq_ref[...] = _rope(z_qk[:, 0:256], rc, rs, 16)
    rk_ref[...] = _rope(z_qk[:, 256:512] * (RET_DK ** -0.5), rc, rs, 16)
    mc = mc_ref[...]
    ms = ms_ref[...]
    ckv_ref[...] = _rms(z_mla[:, 256:384], kvn_ref[...])
    kr_ref[...] = _rope(z_mla[:, 384:512], mc, ms, 8)
    for h_i in range(MLA_H):
        sl = slice(h_i * LANES, (h_i + 1) * LANES)
        q_ref[:, sl] = _rope(q[:, sl], mc, ms, 8).astype(BF16)
    rv_ref[...] = z_rest[:, 0:256]
    rg_ref[...] = z_rest[:, 256:512]
    lx_ref[...] = z_rest[:, 512:768]
    ly_ref[...] = z_rest[:, 768:1024]


def _in_projection(x_ctx, x_lat, mod, g0, win_p, tabs, qn, kvn, wuq_p, geo):
    n = geo.n
    nt = n // geo.tile
    rc, rs, mc, ms = tabs
    row = lambda w: pl.BlockSpec((geo.tile, w), lambda i: (i, 0))
    full = lambda a: pl.BlockSpec(a.shape, lambda i: (0,) * a.ndim)
    tab = lambda w: pl.BlockSpec((geo.tile, w), lambda i: (geo.rope_block(i), 0))
    outs = [(256, F32)] * 6 + [(MLA_H * LANES, BF16), (LANES, F32), (LANES, F32)]
    return pl.pallas_call(
        functools.partial(_inproj_kernel, ctx_tiles=geo.ctx_tiles),
        grid=(nt,),
        in_specs=[geo.ctx_rows(D_MODEL), geo.lat_rows(D_MODEL),
                  pl.BlockSpec((1, 6, D_MODEL), lambda i: (geo.mod_group(i), 0, 0)),
                  full(g0), full(win_p), tab(256), tab(256), tab(LANES), tab(LANES), full(qn), full(kvn), full(wuq_p)],
        out_specs=[row(w) for w, _ in outs],
        out_shape=[jax.ShapeDtypeStruct((n, w), dt) for w, dt in outs],
        compiler_params=_cparams(("arbitrary",)),
        name="in_projection",
    )(x_ctx, x_lat, mod, g0, win_p, rc, rs, mc, ms, qn, kvn, wuq_p)


def _ret_kernel(lg_ref, qf_ref, kf_ref, vf_ref, qb_ref, kb_ref, vb_ref, s0_ref,
                of_ref, ob_ref, sfin_ref, s_scr, intra_scr, cross_scr, into_scr, carry_scr):
    j = pl.program_id(1)
    c = RET_CHUNK
    row = lax.broadcasted_iota(jnp.int32, (c, c), 0)
    lane = lax.broadcasted_iota(jnp.int32, (c, c), 1)
    rowf = row.astype(F32)
    lanef = lane.astype(F32)

    @pl.when(j == 0)
    def _init():
        s_scr[...] = s0_ref[0]
        for d in range(2):
            for h in range(RET_H):
                lg = lg_ref[d, h]
                if d == 0:
                    keep = row >= lane
                    dist = rowf - lanef
                else:
                    keep = lane >= row
                    dist = lanef - rowf
                intra_scr[d, h] = jnp.where(keep, jnp.exp(jnp.where(keep, dist, 0.0) * lg), 0.0)
            for p in range(2):
                lgl = jnp.where(lane < RET_DK, lg_ref[d, 2 * p], lg_ref[d, 2 * p + 1])
                if d == 0:
                    cross_scr[d, p] = jnp.exp((rowf + 1.0) * lgl)
                    into_scr[d, p] = jnp.exp((c - 1.0 - rowf) * lgl)
                else:
                    cross_scr[d, p] = jnp.exp((c - rowf) * lgl)
                    into_scr[d, p] = jnp.exp(rowf * lgl)
                carry_scr[d, p] = jnp.exp(float(c) * lgl)

    same_head = (row < RET_DK) == (lane < RET_DK)

    chains = [(d, p, refs) for d, refs in ((0, (qf_ref, kf_ref, vf_ref, of_ref)), (1, (qb_ref, kb_ref, vb_ref, ob_ref)))
              for p in range(2)]
    n_sub = qf_ref.shape[0] // c
    for sub in range(n_sub):
        staged = []
        for d, p, (q_ref, k_ref, v_ref, _) in chains:
            rows = slice(sub * c, (sub + 1) * c) if d == 0 else slice((n_sub - 1 - sub) * c, (n_sub - sub) * c)
            sl = slice(p * LANES, (p + 1) * LANES)
            q2b = q_ref[rows, sl].astype(BF16)
            k2 = k_ref[rows, sl]
            v2 = v_ref[rows, sl]
            scores, vals = [], []
            for e in range(2):
                sel = (lane >= RET_DK) if e else (lane < RET_DK)
                ke = jnp.where(sel, k2, 0.0).astype(BF16)
                vals.append(jnp.where(sel, v2, 0.0).astype(BF16))
                scores.append(lax.dot_general(q2b, ke, (((1,), (1,)), ((), ())), preferred_element_type=F32))
            st = s_scr[d, p]
            from_state = jnp.dot(q2b, st.astype(BF16), preferred_element_type=F32)
            kw = (k2 * into_scr[d, p]).astype(BF16)
            upd = lax.dot_general(kw, v2.astype(BF16), (((0,), (0,)), ((), ())), preferred_element_type=F32)
            staged.append((rows, scores, vals, st, from_state, upd))
        for (d, p, (_, _, _, o_ref)), (rows, scores, vals, st, from_state, upd) in zip(chains, staged):
            o = from_state * cross_scr[d, p]
            for e in range(2):
                a = (scores[e] * intra_scr[d, 2 * p + e]).astype(BF16)
                o = o + jnp.dot(a, vals[e], preferred_element_type=F32)
            s_scr[d, p] = st * carry_scr[d, p] + jnp.where(same_head, upd, 0.0)
            o_ref[rows, p * LANES:(p + 1) * LANES] = o

    @pl.when(j == pl.num_programs(1) - 1)
    def _fin():
        sfin_ref[0] = s_scr[...]


def _retention(rq, rk, rv, lg, s0, nb, t, row0):
    n = nb * t
    step_rows = RET_STEP_CHUNKS * RET_CHUNK
    nc = t // step_rows
    base = row0 // step_rows
    fwd = lambda off: pl.BlockSpec((step_rows, RET_W), lambda b, j: (off + b * nc + j, 0))
    bwd = lambda off: pl.BlockSpec((step_rows, RET_W), lambda b, j: (off + b * nc + nc - 1 - j, 0))
    st = pl.BlockSpec((1, 2, 2, LANES, LANES), lambda b, j: (b, 0, 0, 0, 0))
    return pl.pallas_call(
        _ret_kernel,
        grid=(nb, nc),
        in_specs=[pl.BlockSpec(memory_space=pltpu.SMEM), fwd(base), fwd(base), fwd(base), bwd(base), bwd(base),
                  bwd(base), st],
        out_specs=[fwd(0), bwd(0), st],
        out_shape=[jax.ShapeDtypeStruct((n, RET_W), F32), jax.ShapeDtypeStruct((n, RET_W), F32),
                   jax.ShapeDtypeStruct((nb, 2, 2, LANES, LANES), F32)],
        scratch_shapes=[pltpu.VMEM((2, 2, LANES, LANES), F32), pltpu.VMEM((2, RET_H, RET_CHUNK, RET_CHUNK), F32),
                        pltpu.VMEM((2, 2, RET_CHUNK, LANES), F32), pltpu.VMEM((2, 2, RET_CHUNK, LANES), F32),
                        pltpu.VMEM((2, 2, RET_CHUNK, LANES), F32)],
        compiler_params=_cparams(("arbitrary", "arbitrary")),
        name="retention",
    )(lg, rq, rk, rv, rq, rk, rv, s0)


def _state_to_pairs(s):
    b = s.shape[0]
    s = s.reshape(b, 2, 2, 2, RET_DK, RET_DK)
    z = jnp.zeros_like(s[:, :, :, 0])
    top = jnp.concatenate([s[:, :, :, 0], z], axis=-1)
    bot = jnp.concatenate([z, s[:, :, :, 1]], axis=-1)
    return jnp.concatenate([top, bot], axis=-2)


def _pairs_to_state(s):
    b = s.shape[0]
    a = s[..., :RET_DK, :RET_DK]
    c = s[..., RET_DK:, RET_DK:]
    return jnp.stack([a, c], axis=3).reshape(b, 2, RET_H, RET_DK, RET_DK)


def _scan_rows(a, b, reverse):
    r = a.shape[0]
    rows = lax.broadcasted_iota(jnp.int32, a.shape, 0)
    s = 1
    while s < r:
        if reverse:
            a_s = pltpu.roll(a, r - s, 0)
            b_s = pltpu.roll(b, r - s, 0)
            m = rows < r - s
        else:
            a_s = pltpu.roll(a, s, 0)
            b_s = pltpu.roll(b, s, 0)
            m = rows >= s
        b = jnp.where(m, a * b_s + b, b)
        a = jnp.where(m, a * a_s, a)
        s *= 2
    return a, b


def _lru_kernel(c8_ref, cw_ref, cb_ref, wg_ref, bg_ref, lx_ref, ly_ref, h0_ref, o_ref, hfin_ref, xc_scr, hf_scr, hb_scr,
                *, t, r):
    nch = t // r
    cw = cw_ref[0]
    cb = cb_ref[0]
    wg = wg_ref[0]
    bg = bg_ref[0]
    c8 = c8_ref[0]

    def conv_body(c, carry):
        r0 = pl.multiple_of(c * r, r)
        cur = lx_ref[pl.ds(r0, r), :]
        prev = lx_ref[pl.ds(pl.multiple_of(jnp.maximum(r0 - 8, 0), 8), 8), :]
        nxt = lx_ref[pl.ds(pl.multiple_of(jnp.minimum(r0 + r, t - 8), 8), 8), :]
        prev = jnp.where(c > 0, prev, 0.0)
        nxt = jnp.where(c < nch - 1, nxt, 0.0)
        ext = jnp.concatenate([prev, cur, nxt], axis=0)
        acc = jnp.broadcast_to(cb, (r, LANES))
        for tap in range(4):
            sh = (2 - tap) % (r + 16)
            xs = ext if sh == 0 else pltpu.roll(ext, sh, 0)
            acc = acc + xs[8:8 + r] * cw[tap:tap + 1]
        xc_scr[pl.ds(r0, r), :] = acc
        return carry

    lax.fori_loop(0, nch, conv_body, 0)

    def gates(xc, d):
        g = jnp.dot(xc.astype(BF16), wg[:, d * 256:(d + 1) * 256], preferred_element_type=F32) + bg[:, d * 256:(d + 1) * 256]
        rg = jax.nn.sigmoid(g[:, :LANES])
        ig = jax.nn.sigmoid(g[:, LANES:])
        log_a = c8[d:d + 1] * rg
        a = jnp.exp(log_a)
        b = jnp.sqrt(jnp.tanh(-log_a) * (a * a + 1.0)) * (ig * xc)
        return a, b

    def scan_body(c, carry):
        h_f, h_b = carry
        rf = pl.multiple_of(c * r, r)
        rb = pl.multiple_of((nch - 1 - c) * r, r)
        a_f, b_f = gates(xc_scr[pl.ds(rf, r), :], 0)
        a_b, b_b = gates(xc_scr[pl.ds(rb, r), :], 1)
        a_f, b_f = _scan_rows(a_f, b_f, False)
        a_b, b_b = _scan_rows(a_b, b_b, True)
        hc_f = a_f * h_f + b_f
        hc_b = a_b * h_b + b_b
        hf_scr[pl.ds(rf, r), :] = hc_f
        hb_scr[pl.ds(rb, r), :] = hc_b
        return hc_f[r - 1:r], hc_b[0:1]

    h_f, h_b = lax.fori_loop(0, nch, scan_body, (h0_ref[0, 0:1, :], h0_ref[0, 1:2, :]))

    def out_body(c, carry):
        r0 = pl.multiple_of(c * r, r)
        rows = pl.ds(r0, r)
        o_ref[rows, :] = ((hf_scr[rows, :] + hb_scr[rows, :]) * jax.nn.gelu(ly_ref[rows, :])).astype(BF16)
        return carry

    lax.fori_loop(0, nch, out_body, 0)
    hfin_ref[0, 0:1, :] = h_f
    hfin_ref[0, 1:2, :] = h_b


def _lru(lx, ly, h0, c8, cw, cb, wg, bg, nb, t, row0):
    base = row0 // t
    r = min(LRU_ROWS, t)
    seq = lambda off: pl.BlockSpec((t, LANES), lambda b, hh: (off + b, hh))
    par = lambda a: pl.BlockSpec((1,) + a.shape[1:], lambda b, hh: (hh,) + (0,) * (a.ndim - 1))
    st = pl.BlockSpec((1, 2, LANES), lambda b, hh: (b, 0, hh))
    return pl.pallas_call(
        functools.partial(_lru_kernel, t=t, r=r),
        grid=(nb, 2),
        in_specs=[par(c8), par(cw), par(cb), par(wg), par(bg), seq(base), seq(base), st],
        out_specs=[seq(0), st],
        out_shape=[jax.ShapeDtypeStruct((nb * t, LRU_W), BF16), jax.ShapeDtypeStruct((nb, 2, LRU_W), F32)],
        scratch_shapes=[pltpu.VMEM((t, LANES), F32)] * 3,
        compiler_params=_cparams(("arbitrary", "arbitrary")),
        name="rg_lru",
    )(c8, cw, cb, wg, bg, lx, ly, h0)


def _mla_kernel(*refs, n_ctx, t, tq):
    if n_ctx:
        q_ref, ckv_ref, kr_ref, cckv_ref, ckr_ref, wk_ref, wv_ref = refs[:7]
    else:
        q_ref, ckv_ref, kr_ref, wk_ref, wv_ref = refs[:5]
    o_ref, k_scr, vt_scr, qt_scr, ot_scr = refs[-5:]
    qi = pl.program_id(1)
    cexp = (MLA_DN + MLA_DR) ** -0.5 * math.log2(math.e)
    ones_row = (lax.broadcasted_iota(jnp.int32, (MLA_H * MLA_VT_ROWS, 1), 0) % MLA_VT_ROWS == MLA_DV).astype(F32)

    def put(ckv, kr, c0):
        nrow = ckv.shape[0]
        ckv_b = ckv.astype(BF16)
        kn = jnp.dot(ckv_b, wk_ref[...], preferred_element_type=F32)
        vt = lax.dot_general(wv_ref[...], ckv_b, (((1,), (1,)), ((), ())), preferred_element_type=F32) + ones_row
        for h in range(MLA_H):
            k_scr[h, c0:c0 + nrow, :] = ((kn[:, h * LANES:(h + 1) * LANES] + kr) * cexp).astype(BF16)
            vt_scr[h, :, c0:c0 + nrow] = vt[h * MLA_VT_ROWS:(h + 1) * MLA_VT_ROWS].astype(BF16)

    @pl.when(qi == 0)
    def _build():
        if n_ctx:
            put(cckv_ref[0], ckr_ref[0], 0)
        step = min(KV_BUILD_ROWS, t)
        for c in range(t // step):
            put(ckv_ref[c * step:(c + 1) * step, :], kr_ref[c * step:(c + 1) * step, :], n_ctx + c * step)

    qt_scr[...] = q_ref[...].astype(F32).T.astype(BF16)
    s_len = n_ctx + t

    chunks = [slice(c0, min(c0 + ATT_KC, s_len)) for c0 in range(0, s_len, ATT_KC)]
    nchunk = len(chunks)
    steps = [(h, c) for h in range(MLA_H) for c in range(nchunk)]

    def score(i):
        h, c = steps[i]
        return jnp.dot(k_scr[h, chunks[c], :], qt_scr[h * LANES:(h + 1) * LANES, :], preferred_element_type=F32)

    pending = {i: score(i) for i in range(min(ATT_AHEAD, len(steps)))}
    held = None
    m = o = None
    for i in range(len(steps) + 1):
        if i + ATT_AHEAD < len(steps):
            pending[i + ATT_AHEAD] = score(i + ATT_AHEAD)
        if i < len(steps):
            h, c = steps[i]
            s = pending.pop(i)
            m_old = jnp.full((1, tq), -1e30, F32) if c == 0 else m
            m = jnp.maximum(m_old, jnp.max(s, axis=0, keepdims=True))
            p = jnp.exp2(s - m).astype(BF16)
            alpha = jnp.exp2(m_old - m)
        if held is not None:
            hh, cc, p_h, alpha_h = held
            pv = jnp.dot(vt_scr[hh, :, chunks[cc]], p_h, preferred_element_type=F32)
            o = pv if cc == 0 else o * alpha_h + pv
            if cc == nchunk - 1:
                ot_scr[hh] = o[:MLA_DV] / o[MLA_DV:MLA_DV + 1]
        held = (h, c, p, alpha) if i < len(steps) else None
    for pp in range(MLA_H // 2):
        pair = jnp.concatenate([ot_scr[2 * pp], ot_scr[2 * pp + 1]], axis=0)
        o_ref[:, pp * LANES:(pp + 1) * LANES] = pair.T.astype(BF16)


def _mla(q, ckvn, kr, cache, wk, wv_t, nb, t, row0):
    tq = min(ATT_TQ, t)
    nq = t // tq
    n_ctx = 0 if cache is None else cache[0].shape[1]
    s_len = n_ctx + t
    qspec = pl.BlockSpec((tq, MLA_H * LANES), lambda b, i: (row0 // tq + b * nq + i, 0))
    seq = pl.BlockSpec((t, LANES), lambda b, i: (row0 // t + b, 0))
    full = lambda a: pl.BlockSpec(a.shape, lambda b, i: (0,) * a.ndim)
    ins = [q, ckvn, kr]
    specs = [qspec, seq, seq]
    if n_ctx:
        cspec = pl.BlockSpec((1, n_ctx, LANES), lambda b, i: (b, 0, 0))
        ins += [cache[0], cache[1]]
        specs += [cspec, cspec]
    ins += [wk, wv_t]
    specs += [full(wk), full(wv_t)]
    return pl.pallas_call(
        functools.partial(_mla_kernel, n_ctx=n_ctx, t=t, tq=tq),
        grid=(nb, nq),
        in_specs=specs,
        out_specs=pl.BlockSpec((tq, MLA_W), lambda b, i: (b * nq + i, 0)),
        out_shape=jax.ShapeDtypeStruct((nb * t, MLA_W), BF16),
        scratch_shapes=[pltpu.VMEM((MLA_H, s_len, LANES), BF16), pltpu.VMEM((MLA_H, MLA_VT_ROWS, s_len), BF16),
                        pltpu.VMEM((MLA_H * LANES, tq), BF16), pltpu.VMEM((MLA_H, MLA_DV, tq), F32)],
        compiler_params=_cparams(("arbitrary", "arbitrary")),
        name="mla_attention",
    )(*ins)


def _outproj_kernel(ofc_ref, obc_ref, lruc_ref, mlac_ref, xc_ref, ofl_ref, obl_ref, lrul_ref, mlal_ref, xl_ref, rg_ref,
                    mod_ref, retn_ref, g1_ref, g2_ref, wout_ref, rw_ref, rb_ref, x1_ref, h2_ref, route_ref, cnt_ref,
                    cnt_scr, *, ctx_tiles):
    is_ctx = pl.program_id(0) < ctx_tiles
    pick = lambda c_ref, l_ref: jnp.where(is_ctx, c_ref[...], l_ref[...])
    o = pick(ofc_ref, ofl_ref) + pick(obc_ref, obl_ref)
    hid = lax.broadcasted_iota(jnp.int32, o.shape, 1) // RET_DK

    def head_sum(v):
        tot = jnp.zeros_like(v)
        for hh in range(RET_H):
            msk = hid == hh
            tot = jnp.where(msk, jnp.sum(jnp.where(msk, v, 0.0), axis=1, keepdims=True), tot)
        return tot

    mu = head_sum(o) * (1.0 / RET_DK)
    dl = o - mu
    var = head_sum(dl * dl) * (1.0 / RET_DK)
    rg = rg_ref[...]
    ret = dl * lax.rsqrt(var + EPS) * retn_ref[...] * (rg * jax.nn.sigmoid(rg))
    mix = jnp.concatenate([ret.astype(BF16), pick(lruc_ref, lrul_ref), pick(mlac_ref, mlal_ref)], axis=1)
    mo = jnp.dot(mix, wout_ref[...], preferred_element_type=F32)
    mod = mod_ref[0]
    x1 = pick(xc_ref, xl_ref) + mod[2:3] * _rms(mo, g1_ref[...])
    x1_ref[...] = x1
    h2 = _rms(x1, g2_ref[...]) * (1.0 + mod[4:5]) + mod[3:4]
    _store_row_tiles(h2_ref, h2)

    tm = h2.shape[0]
    lg = jnp.dot(h2.astype(BF16), rw_ref[...], preferred_element_type=F32) + rb_ref[...]
    lgt = lg.T[:N_EXPERTS, :]
    rowf = lax.broadcasted_iota(jnp.int32, (N_EXPERTS, tm), 0).astype(F32)
    tops, idxs, hots = [], [], []
    for _ in range(TOP_K):
        m = jnp.max(lgt, axis=0, keepdims=True)
        idx = jnp.min(jnp.where(lgt == m, rowf, float(N_EXPERTS)), axis=0, keepdims=True)
        hot = rowf == idx
        lgt = jnp.where(hot, -jnp.inf, lgt)
        tops.append(m)
        idxs.append(idx)
        hots.append(hot)
    exps = [jnp.exp(t - tops[0]) for t in tops]
    den = exps[0] + exps[1] + exps[2] + exps[3]
    member = jnp.zeros((N_EXPERTS, tm), F32)
    for hot in hots:
        member = member + hot.astype(F32)

    @pl.when(pl.program_id(0) == 0)
    def _zero_counts():
        cnt_scr[...] = jnp.zeros_like(cnt_scr)

    earlier = (lax.broadcasted_iota(jnp.int32, (tm, tm), 0) < lax.broadcasted_iota(jnp.int32, (tm, tm), 1))
    counts = cnt_scr[...]
    before = jnp.dot(member.astype(BF16), earlier.astype(BF16), preferred_element_type=F32) + counts[:, 0:1]
    ranks = [jnp.sum(jnp.where(hot, before, 0.0), axis=0, keepdims=True) for hot in hots]
    gates = [e / den for e in exps]
    route_ref[0] = jnp.concatenate(idxs + gates + ranks + [jnp.zeros((TOP_K, tm), F32)], axis=0)
    counts = counts + jnp.sum(member, axis=1, keepdims=True)
    cnt_scr[...] = counts
    cnt_ref[...] = counts


def _out_projection(ctx_mix, lat_mix, rg, mod, retn, g1, g2, wout, rw, rb, geo):
    n = geo.n
    row = lambda w: pl.BlockSpec((ROW_TILE, w), lambda i: (i, 0))
    full = lambda a: pl.BlockSpec(a.shape, lambda i: (0,) * a.ndim)
    widths = (256, 256, 256, MLA_W, D_MODEL)
    return pl.pallas_call(
        functools.partial(_outproj_kernel, ctx_tiles=geo.ctx_tiles),
        grid=(n // ROW_TILE,),
        in_specs=[geo.ctx_rows(w) for w in widths] + [geo.lat_rows(w) for w in widths] + [
                  row(256),
                  pl.BlockSpec((1, 6, D_MODEL), lambda i: (geo.mod_group(i), 0, 0)),
                  full(retn), full(g1), full(g2), full(wout), full(rw), full(rb)],
        out_specs=[row(D_MODEL), pl.BlockSpec((ROW_TILE * ROW_CHUNKS, LANES), lambda i: (i, 0)),
                   pl.BlockSpec((1, 4 * TOP_K, ROW_TILE), lambda i: (i, 0, 0)),
                   pl.BlockSpec((N_EXPERTS, LANES), lambda i: (0, 0))],
        out_shape=[jax.ShapeDtypeStruct((n, D_MODEL), F32), jax.ShapeDtypeStruct((n * ROW_CHUNKS, LANES), F32),
                   jax.ShapeDtypeStruct((n // ROW_TILE, 4 * TOP_K, ROW_TILE), F32),
                   jax.ShapeDtypeStruct((N_EXPERTS, LANES), F32)],
        scratch_shapes=[pltpu.VMEM((N_EXPERTS, LANES), F32)],
        compiler_params=_cparams(("arbitrary",)),
        name="out_projection",
    )(*ctx_mix, *lat_mix, rg, mod, retn, g1, g2, wout, rw, rb)


def _dispatch_kernel(pe_ref, slot_ref, h_ref, xs_hbm, zero_scr, sem):
    i = pl.program_id(0)
    n_slots = xs_hbm.shape[0] // ROW_CHUNKS

    def slot_rows(first_slot, n):
        return xs_hbm.at[pl.ds(pl.multiple_of(first_slot * ROW_CHUNKS, ROW_CHUNKS), n * ROW_CHUNKS), :]

    @pl.when(i == 0)
    def _zero_pads():
        zero_scr[...] = jnp.zeros_like(zero_scr)

        def fill(e):
            end = pe_ref[e]
            start = 0 if e == 0 else pe_ref[e - 1]
            return end > start, pltpu.make_async_copy(zero_scr, slot_rows(jnp.maximum(end - MOE_TILE, 0), MOE_TILE), sem)

        def fill_tail(j):
            row = pe_ref[N_EXPERTS - 1] + j * MOE_TILE
            dst = slot_rows(jnp.minimum(row, n_slots - MOE_TILE), MOE_TILE)
            return row < n_slots, pltpu.make_async_copy(zero_scr, dst, sem)

        for e in range(N_EXPERTS):
            for todo, cp in (fill(e), fill_tail(e)):
                pl.when(todo)(cp.start)
        for e in range(N_EXPERTS):
            for todo, cp in (fill(e), fill_tail(e)):
                pl.when(todo)(cp.wait)

    for r in range(ROW_TILE):
        for k in range(TOP_K):
            first = pl.multiple_of(slot_ref[0, 0, k * ROW_TILE + r], ROW_CHUNKS)
            dst = xs_hbm.at[pl.ds(first, ROW_CHUNKS), :]
            pltpu.make_async_copy(h_ref.at[pl.ds(r * ROW_CHUNKS, ROW_CHUNKS), :], dst, sem).start(priority=k % 2)
    for k in range(TOP_K):
        pltpu.make_async_copy(h_ref, xs_hbm.at[pl.ds(0, ROW_TILE * ROW_CHUNKS), :], sem).wait()


def _dispatch(h2, slots3, pad_end, n_slots):
    n = h2.shape[0] // ROW_CHUNKS
    grid_spec = pltpu.PrefetchScalarGridSpec(
        num_scalar_prefetch=1,
        grid=(n // ROW_TILE,),
        in_specs=[
            pl.BlockSpec((1, 1, TOP_K * ROW_TILE), lambda i, pe: (i, 0, 0), memory_space=pltpu.SMEM),
            pl.BlockSpec((ROW_TILE * ROW_CHUNKS, LANES), lambda i, pe: (i, 0)),
        ],
        out_specs=pl.BlockSpec(memory_space=pl.ANY),
        scratch_shapes=[pltpu.VMEM((MOE_TILE * ROW_CHUNKS, LANES), F32), pltpu.SemaphoreType.DMA(())],
    )
    return pl.pallas_call(
        _dispatch_kernel,
        grid_spec=grid_spec,
        out_shape=jax.ShapeDtypeStruct((n_slots * ROW_CHUNKS, LANES), F32),
        compiler_params=_cparams(("arbitrary",)),
        name="moe_dispatch",
    )(pad_end, slots3, h2)


def _moe_kernel(te_ref, nu_ref, x_ref, wgu_ref, bgu_ref, wd_ref, bd_ref, y_ref, wgu_b, wd_b):
    t = pl.program_id(0)
    n_used = nu_ref[0]
    nch = D_EXPERT // MOE_HCHUNK
    new_expert = jnp.logical_or(t == 0, te_ref[t] != te_ref[jnp.maximum(t - 1, 0)])

    @pl.when(jnp.logical_and(new_expert, t < n_used))
    def _cast_weights():
        for j in range(nch):
            cols = slice(j * MOE_HCHUNK, (j + 1) * MOE_HCHUNK)
            ucols = slice(D_EXPERT + j * MOE_HCHUNK, D_EXPERT + (j + 1) * MOE_HCHUNK)
            wgu_b[j, :, :MOE_HCHUNK] = wgu_ref[0, 0, :, cols].astype(BF16)
            wgu_b[j, :, MOE_HCHUNK:] = wgu_ref[0, 0, :, ucols].astype(BF16)
        wd_b[...] = wd_ref[0, 0].astype(BF16)

    @pl.when(t < n_used)
    def _compute():
        x = _load_row_tiles(x_ref).astype(BF16)
        bgu = bgu_ref[pl.ds(te_ref[t], 1), :]

        def gate_up(j):
            return jnp.dot(x, wgu_b[j], preferred_element_type=F32)

        pending = gate_up(0)
        y = None
        for j in range(nch):
            gu = pending
            if j + 1 < nch:
                pending = gate_up(j + 1)
            cols = slice(j * MOE_HCHUNK, (j + 1) * MOE_HCHUNK)
            ucols = slice(D_EXPERT + j * MOE_HCHUNK, D_EXPERT + (j + 1) * MOE_HCHUNK)
            g = jnp.minimum(gu[:, :MOE_HCHUNK] + bgu[:, cols], SWIGLU_LIMIT)
            u = jnp.clip(gu[:, MOE_HCHUNK:] + bgu[:, ucols], -SWIGLU_LIMIT, SWIGLU_LIMIT)
            act = (g * jax.nn.sigmoid(SWIGLU_ALPHA * g) * (u + 1.0)).astype(BF16)
            part = jnp.dot(act, wd_b[cols, :], preferred_element_type=F32)
            y = part if y is None else y + part
        _store_row_tiles(y_ref, y + bd_ref[pl.ds(te_ref[t], 1), :])

    @pl.when(t >= n_used)
    def _idle():
        y_ref[...] = jnp.zeros_like(y_ref)


def _moe(xs, tile_expert, n_used, layer, wgu, bgu, wd, bd):
    n_tiles = tile_expert.shape[0]
    wspec = lambda r, c: pl.BlockSpec((1, 1, r, c), lambda t, te, nu: (layer, te[t], 0, 0))
    bspec = lambda c: pl.BlockSpec((N_EXPERTS, c), lambda t, te, nu: (0, 0))
    grid_spec = pltpu.PrefetchScalarGridSpec(
        num_scalar_prefetch=2,
        grid=(n_tiles,),
        in_specs=[
            pl.BlockSpec((MOE_TILE * ROW_CHUNKS, LANES),
                         lambda t, te, nu: (jnp.minimum(t, jnp.maximum(nu[0] - 1, 0)), 0)),
            wspec(D_MODEL, 2 * D_EXPERT), bspec(2 * D_EXPERT), wspec(D_EXPERT, D_MODEL), bspec(D_MODEL),
        ],
        out_specs=pl.BlockSpec((MOE_TILE * ROW_CHUNKS, LANES), lambda t, te, nu: (t, 0)),
        scratch_shapes=[pltpu.VMEM((D_EXPERT // MOE_HCHUNK, D_MODEL, 2 * MOE_HCHUNK), BF16),
                        pltpu.VMEM((D_EXPERT, D_MODEL), BF16)],
    )
    return pl.pallas_call(
        _moe_kernel,
        grid_spec=grid_spec,
        out_shape=jax.ShapeDtypeStruct((n_tiles * MOE_TILE * ROW_CHUNKS, LANES), F32),
        compiler_params=_cparams(("arbitrary",)),
        name="moe_experts",
    )(tile_expert, n_used, xs, wgu, bgu[layer], wd, bd[layer])


def _combine_kernel(inv_ref, invn_ref, y_hbm, gate_ref, x1_ref, mod_ref, g3_ref, oc_ref, ol_ref, buf, sem, *,
                    ctx_tiles):
    i = pl.program_id(0)
    cur = lax.rem(i, 2)

    def gather(idx_ref, half):
        for r in range(ROW_TILE):
            for k in range(TOP_K):
                first = pl.multiple_of(idx_ref[0, 0, k * ROW_TILE + r], ROW_CHUNKS)
                pltpu.make_async_copy(y_hbm.at[pl.ds(first, ROW_CHUNKS), :],
                                      buf.at[half, k, pl.ds(r * ROW_CHUNKS, ROW_CHUNKS), :],
                                      sem.at[half]).start(priority=k % 2)

    @pl.when(i == 0)
    def _first():
        gather(inv_ref, 0)

    has_next = i + 1 < pl.num_programs(0)
    for half in range(2):
        pl.when(jnp.logical_and(has_next, cur != half))(functools.partial(gather, invn_ref, half))

    for k in range(TOP_K):
        pltpu.make_async_copy(y_hbm.at[pl.ds(0, ROW_TILE * ROW_CHUNKS), :], buf.at[cur, k], sem.at[cur]).wait()
    gate = gate_ref[...]
    ff = gate[:, 0:1] * _load_row_tiles(buf.at[cur, 0])
    for k in range(1, TOP_K):
        ff = ff + gate[:, k:k + 1] * _load_row_tiles(buf.at[cur, k])
    out = x1_ref[...] + mod_ref[0][5:6] * _rms(ff, g3_ref[...])
    is_ctx = pl.program_id(0) < ctx_tiles

    @pl.when(is_ctx)
    def _ctx():
        oc_ref[...] = out

    @pl.when(jnp.logical_not(is_ctx))
    def _lat():
        ol_ref[...] = out


def _combine(y_sorted, inv3, gate, x1, mod, g3, geo):
    n = x1.shape[0]
    nt = n // ROW_TILE
    return pl.pallas_call(
        functools.partial(_combine_kernel, ctx_tiles=geo.ctx_tiles),
        grid=(nt,),
        in_specs=[
            pl.BlockSpec((1, 1, TOP_K * ROW_TILE), lambda i: (i, 0, 0), memory_space=pltpu.SMEM),
            pl.BlockSpec((1, 1, TOP_K * ROW_TILE), lambda i: (jnp.minimum(i + 1, nt - 1), 0, 0),
                         memory_space=pltpu.SMEM),
            pl.BlockSpec(memory_space=pl.ANY),
            pl.BlockSpec((ROW_TILE, TOP_K), lambda i: (i, 0)),
            pl.BlockSpec((ROW_TILE, D_MODEL), lambda i: (i, 0)),
            pl.BlockSpec((1, 6, D_MODEL), lambda i: (geo.mod_group(i), 0, 0)),
            pl.BlockSpec((1, D_MODEL), lambda i: (0, 0)),
        ],
        out_specs=[geo.ctx_rows(D_MODEL), geo.lat_rows(D_MODEL)],
        out_shape=[jax.ShapeDtypeStruct((geo.n_ctx, D_MODEL), F32), jax.ShapeDtypeStruct((n - geo.n_ctx, D_MODEL), F32)],
        scratch_shapes=[pltpu.VMEM((2, TOP_K, ROW_TILE * ROW_CHUNKS, LANES), F32), pltpu.SemaphoreType.DMA((2,))],
        compiler_params=_cparams(("arbitrary",)),
        name="moe_combine",
    )(inv3, inv3, y_sorted, gate, x1, mod, g3)


def _plan(route, counts):
    nt = route.shape[0]
    n = nt * ROW_TILE
    expert = route[:, 0:TOP_K, :].astype(jnp.int32)
    gate = route[:, TOP_K:2 * TOP_K, :].transpose(0, 2, 1).reshape(n, TOP_K)
    rank = route[:, 2 * TOP_K:3 * TOP_K, :].astype(jnp.int32)
    cnt = counts[:, 0].astype(jnp.int32)
    padded = (cnt + MOE_TILE - 1) // MOE_TILE * MOE_TILE
    pad_end = jnp.cumsum(padded).astype(jnp.int32)
    pad_start = pad_end - padded
    start_of = jnp.sum(jnp.where(expert[..., None] == jnp.arange(N_EXPERTS), pad_start, 0), axis=-1)
    slots3 = ((start_of + rank) * ROW_CHUNKS).reshape(nt, 1, TOP_K * ROW_TILE)
    n_tiles = n * TOP_K // MOE_TILE + N_EXPERTS
    tile_start = jnp.arange(n_tiles, dtype=jnp.int32) * MOE_TILE
    tile_expert = jnp.minimum(jnp.sum((tile_start[:, None] >= pad_end[None, :]).astype(jnp.int32), axis=1),
                              N_EXPERTS - 1)
    n_used = (pad_end[-1] // MOE_TILE).reshape(1)
    return gate, slots3, pad_end, tile_expert, n_used, n_tiles * MOE_TILE


class _Geometry:
    def __init__(self, nb_ctx, t_ctx, nb_lat, t_lat, tile):
        self.nb_ctx, self.t_ctx, self.nb_lat, self.t_lat, self.tile = nb_ctx, t_ctx, nb_lat, t_lat, tile
        self.n_ctx = nb_ctx * t_ctx
        self.n = self.n_ctx + nb_lat * t_lat
        self.ctx_tiles = self.n_ctx // tile
        self.lat_tiles = t_lat // tile

    def mod_group(self, i):
        return jnp.where(i < self.ctx_tiles, 0, 1 + (i - self.ctx_tiles) // self.lat_tiles)

    def ctx_rows(self, w):
        return pl.BlockSpec((self.tile, w), lambda i: (jnp.minimum(i, self.ctx_tiles - 1), 0))

    def lat_rows(self, w):
        return pl.BlockSpec((self.tile, w), lambda i: (jnp.maximum(i - self.ctx_tiles, 0), 0))

    def rope_block(self, i):
        return jnp.where(i < self.ctx_tiles, self.lat_tiles, (i - self.ctx_tiles) % self.lat_tiles)


def _rope_tables(t, tile):
    pos = jnp.arange(t)
    row = (pos // GRID_W).astype(F32)
    col = (pos % GRID_W).astype(F32)

    def cs(dim):
        q = dim // 4
        inv = ROPE_BASE ** (-jnp.arange(q, dtype=F32) / q)
        ar = row[:, None] * inv
        ac = col[:, None] * inv
        c = jnp.concatenate([jnp.cos(ar), jnp.cos(ar), jnp.cos(ac), jnp.cos(ac)], axis=-1)
        s = jnp.concatenate([-jnp.sin(ar), jnp.sin(ar), -jnp.sin(ac), jnp.sin(ac)], axis=-1)
        return c, s

    def with_identity(c, s):
        return (jnp.concatenate([c, jnp.ones((tile, c.shape[1]), F32)], axis=0),
                jnp.concatenate([s, jnp.zeros((tile, s.shape[1]), F32)], axis=0))

    c64, s64 = cs(RET_DK)
    rc, rs = with_identity(jnp.tile(c64, (1, RET_H)), jnp.tile(s64, (1, RET_H)))
    c32, s32 = cs(MLA_DR)
    pad = lambda a, v: jnp.concatenate([jnp.full((t, MLA_DN), v, F32), a, jnp.full((t, LANES - MLA_DN - MLA_DR), v, F32)], axis=-1)
    mc, ms = with_identity(pad(c32, 1.0), pad(s32, 0.0))
    return rc, rs, mc, ms


def _layer_params(l, norm_g, w_in, ret_decay, ret_norm, conv_w, conv_b, lru_gate_w, lru_gate_b, lru_lambda,
                  mla_q_norm, mla_kv_norm, mla_w_uq, mla_w_ukv, w_out, router_w, router_b):
    p = {}
    p['g'] = [norm_g[l, i].reshape(1, D_MODEL) for i in range(4)]
    kr0 = 1920
    p['win'] = jnp.concatenate([w_in[l][:, :kr0], jnp.zeros((D_MODEL, MLA_DN), F32), w_in[l][:, kr0:],
                                jnp.zeros((D_MODEL, LANES - MLA_DN - MLA_DR), F32)], axis=1).astype(BF16)
    p['lg'] = jax.nn.log_sigmoid(ret_decay[l].astype(F32))
    p['retn'] = ret_norm[l].reshape(1, RET_W)
    p['qn'] = mla_q_norm[l].reshape(1, Q_RANK)
    p['kvn'] = mla_kv_norm[l].reshape(1, KV_RANK)
    wuq = mla_w_uq[l].reshape(Q_RANK, MLA_H, MLA_DN + MLA_DR)
    p['wuq'] = jnp.pad(wuq, ((0, 0), (0, 0), (0, LANES - MLA_DN - MLA_DR))).reshape(Q_RANK, MLA_H * LANES).astype(BF16)
    wukv = mla_w_ukv[l].reshape(KV_RANK, MLA_H, MLA_DN + MLA_DV)
    p['wk'] = jnp.pad(wukv[:, :, :MLA_DN], ((0, 0), (0, 0), (0, LANES - MLA_DN))).reshape(KV_RANK, MLA_H * LANES).astype(BF16)
    wv_t = jnp.pad(wukv[:, :, MLA_DN:].transpose(1, 2, 0), ((0, 0), (0, MLA_VT_ROWS - MLA_DV), (0, 0)))
    p['wv_t'] = wv_t.reshape(MLA_H * MLA_VT_ROWS, KV_RANK).astype(BF16)
    p['wout'] = w_out[l].astype(BF16)
    p['rw'] = jnp.pad(router_w[l], ((0, 0), (0, LANES - N_EXPERTS))).astype(BF16)
    p['rb'] = jnp.pad(router_b[l], (0, LANES - N_EXPERTS)).reshape(1, LANES)
    p['cw'] = conv_w[l].reshape(4, 2, LANES).transpose(1, 0, 2)
    p['cb'] = conv_b[l].reshape(2, 1, LANES)
    gw = lru_gate_w[l]
    halves = []
    for hh in range(2):
        cols = []
        for d in range(2):
            for g in range(2):
                blk = jnp.zeros((LANES, LANES), F32)
                for j in range(2):
                    blk = blk.at[j * LRU_BW:(j + 1) * LRU_BW, j * LRU_BW:(j + 1) * LRU_BW].set(gw[d, g, 2 * hh + j])
                cols.append(blk)
        halves.append(jnp.concatenate(cols, axis=1))
    p['wg'] = jnp.stack(halves).astype(BF16)
    gb = lru_gate_b[l].reshape(2, 2, 2, LANES)
    p['bg'] = gb.transpose(2, 0, 1, 3).reshape(2, 1, 4 * LANES)
    p['c8'] = (8.0 * jax.nn.log_sigmoid(lru_lambda[l].astype(F32))).reshape(2, 2, LANES).transpose(1, 0, 2)
    return p


def _forward(x_prompt, x_sample, c, state_ret, state_lru, cache_mla_ckv, cache_mla_krope, c_ctx, ada_w, ada_b,
             *weights):
    nb_ctx, t_ctx, _ = x_prompt.shape
    nb_lat, t_lat, _ = x_sample.shape
    geo = _Geometry(nb_ctx, t_ctx, nb_lat, t_lat, ROW_TILE)
    geo_in = _Geometry(nb_ctx, t_ctx, nb_lat, t_lat, IN_TILE)
    n_c = geo.n_ctx
    x_ctx = x_prompt.reshape(n_c, D_MODEL)
    x_lat = x_sample.reshape(-1, D_MODEL)
    cond = jnp.concatenate([c_ctx[None, :], c, jnp.zeros((16 - 1 - nb_lat, D_MODEL), F32)], axis=0)
    mod_all = _modulation(cond, ada_w, ada_b)[:, :1 + nb_lat].reshape(DEPTH, 1 + nb_lat, 6, D_MODEL)
    tabs = _rope_tables(t_lat, IN_TILE)
    krope_pad = jnp.pad(cache_mla_krope, ((0, 0), (0, 0), (0, 0), (MLA_DN, LANES - MLA_DN - MLA_DR)))
    ret_out, lru_out, ckv_out, kr_out = [], [], [], []
    for l in range(DEPTH):
        p = _layer_params(l, *weights[:-4])
        mod = mod_all[l]
        rq, rk, rv, rg, lx, ly, q, ckvn, kr = _in_projection(x_ctx, x_lat, mod, p['g'][0], p['win'], tabs, p['qn'],
                                                             p['kvn'], p['wuq'], geo_in)
        zero_s = jnp.zeros((nb_ctx, 2, 2, LANES, LANES), F32)
        ofc, obc, s_ctx = _retention(rq, rk, rv, p['lg'], zero_s, nb_ctx, t_ctx, 0)
        ofl, obl, _ = _retention(rq, rk, rv, p['lg'], _state_to_pairs(state_ret[:, l]), nb_lat, t_lat, n_c)
        lru_c, h_ctx = _lru(lx, ly, jnp.zeros((nb_ctx, 2, LRU_W), F32), p['c8'], p['cw'], p['cb'], p['wg'], p['bg'],
                            nb_ctx, t_ctx, 0)
        lru_l, _ = _lru(lx, ly, state_lru[:, l], p['c8'], p['cw'], p['cb'], p['wg'], p['bg'], nb_lat, t_lat, n_c)
        mla_c = _mla(q, ckvn, kr, None, p['wk'], p['wv_t'], nb_ctx, t_ctx, 0)
        mla_l = _mla(q, ckvn, kr, (cache_mla_ckv[:, l], krope_pad[:, l]), p['wk'], p['wv_t'], nb_lat, t_lat, n_c)
        x1, h2, route, counts = _out_projection((ofc, obc, lru_c, mla_c, x_ctx), (ofl, obl, lru_l, mla_l, x_lat), rg,
                                                mod, p['retn'], p['g'][1], p['g'][2], p['wout'], p['rw'], p['rb'], geo)
        gate, slots3, pad_end, tile_expert, n_used, n_slots = _plan(route, counts)
        xs = _dispatch(h2, slots3, pad_end, n_slots)
        y_sorted = _moe(xs, tile_expert, n_used, l, *weights[-4:])
        x_ctx, x_lat = _combine(y_sorted, slots3, gate, x1, mod, p['g'][3], geo)
        ret_out.append(_pairs_to_state(s_ctx))
        lru_out.append(h_ctx)
        ckv_out.append(ckvn[:n_c].reshape(nb_ctx, t_ctx, KV_RANK))
        kr_out.append(kr[:n_c, MLA_DN:MLA_DN + MLA_DR].reshape(nb_ctx, t_ctx, MLA_DR))
    y_prompt = x_ctx.reshape(nb_ctx, t_ctx, D_MODEL)
    y_sample = x_lat.reshape(nb_lat, t_lat, D_MODEL)
    return (y_prompt, y_sample, jnp.stack(ret_out, axis=1), jnp.stack(lru_out, axis=1),
            jnp.stack(ckv_out, axis=1), jnp.stack(kr_out, axis=1))


def kernel(x_prompt, x_sample, c, state_ret, state_lru, cache_mla_ckv, cache_mla_krope, c_ctx, ada_w, ada_b, norm_g, w_in, ret_decay, ret_norm, conv_w, conv_b, lru_gate_w, lru_gate_b, lru_lambda, mla_q_norm, mla_kv_norm, mla_w_uq, mla_w_ukv, w_out, router_w, router_b, moe_w_gu, moe_b_gu, moe_w_down, moe_b_down):
    return _forward(x_prompt, x_sample, c, state_ret, state_lru, cache_mla_ckv, cache_mla_krope, c_ctx, ada_w, ada_b,
                    norm_g, w_in, ret_decay, ret_norm, conv_w, conv_b, lru_gate_w, lru_gate_b, lru_lambda,
                    mla_q_norm, mla_kv_norm, mla_w_uq, mla_w_ukv, w_out, router_w, router_b,
                    moe_w_gu, moe_b_gu, moe_w_down, moe_b_down)
```

```python
import functools
import math

import jax
import jax.numpy as jnp
from jax import lax
from jax.experimental import pallas as pl
from jax.experimental.pallas import tpu as pltpu

F32 = jnp.float32
BF16 = jnp.bfloat16

D_MODEL = 1024
DEPTH = 2
GRID_W = 64
RET_H = 4
RET_DK = 64
RET_W = 256
RET_CHUNK = 128
RET_STEP_CHUNKS = 8
LRU_W = 256
LRU_BLOCKS = 4
LRU_BW = 64
MLA_H = 8
MLA_DN = 64
MLA_DR = 32
MLA_DV = 64
MLA_W = MLA_H * MLA_DV
MLA_VT_ROWS = 80
Q_RANK = 256
KV_RANK = 128
ROPE_BASE = 10000.0
N_EXPERTS = 32
TOP_K = 4
D_EXPERT = 1024
SWIGLU_ALPHA = 1.702
SWIGLU_LIMIT = 7.0
EPS = 1e-6

LANES = 128
ROW_TILE = 256
IN_TILE = 512
MOE_TILE = 512
MOE_HCHUNK = 256
LRU_ROWS = 256
ATT_TQ = 256
ATT_KC = 256
ATT_AHEAD = 4
KV_BUILD_ROWS = 512
PROJ_PAD = 2048
VMEM_LIMIT = 56 * 1024 * 1024


def _cparams(sem):
    return pltpu.CompilerParams(dimension_semantics=sem, vmem_limit_bytes=VMEM_LIMIT)


def _rms(x, g):
    return x * lax.rsqrt(jnp.mean(x * x, axis=-1, keepdims=True) + EPS) * g


ROW_CHUNKS = D_MODEL // LANES


def _store_row_tiles(ref, x):
    for j in range(ROW_CHUNKS):
        ref[pl.ds(j, x.shape[0], stride=ROW_CHUNKS), :] = x[:, j * LANES:(j + 1) * LANES]


def _load_row_tiles(ref):
    rows = ref.shape[0] // ROW_CHUNKS
    return jnp.concatenate([ref[pl.ds(j, rows, stride=ROW_CHUNKS), :] for j in range(ROW_CHUNKS)], axis=1)


def _rope(x, c, s, quarter):
    w = x.shape[-1]
    lane = lax.broadcasted_iota(jnp.int32, x.shape, 1)
    first = (lane % (2 * quarter)) < quarter
    partner = jnp.where(first, pltpu.roll(x, w - quarter, 1), pltpu.roll(x, quarter, 1))
    return x * c + partner * s


def _mod_kernel(cond_ref, w_ref, b_ref, o_ref):
    c = cond_ref[...]
    s = (c * jax.nn.sigmoid(c)).astype(BF16)
    o_ref[0] = jnp.dot(s, w_ref[0].astype(BF16), preferred_element_type=F32) + b_ref[0]


def _modulation(cond, ada_w, ada_b):
    nblk = 6
    return pl.pallas_call(
        _mod_kernel,
        grid=(DEPTH, nblk),
        in_specs=[
            pl.BlockSpec((16, D_MODEL), lambda l, j: (0, 0)),
            pl.BlockSpec((1, D_MODEL, D_MODEL), lambda l, j: (l, 0, j)),
            pl.BlockSpec((1, 1, D_MODEL), lambda l, j: (l, 0, j)),
        ],
        out_specs=pl.BlockSpec((1, 16, D_MODEL), lambda l, j: (l, 0, j)),
        out_shape=jax.ShapeDtypeStruct((DEPTH, 16, 6 * D_MODEL), F32),
        compiler_params=_cparams(("arbitrary", "arbitrary")),
        name="adaln_mod",
    )(cond, ada_w, ada_b.reshape(DEPTH, 1, 6 * D_MODEL))


def _inproj_kernel(xc_ref, xl_ref, mod_ref, g0_ref, win_ref, rc_ref, rs_ref, mc_ref, ms_ref, qn_ref, kvn_ref, wuq_ref,
                   rq_ref, rk_ref, rv_ref, rg_ref, lx_ref, ly_ref, q_ref, ckv_ref, kr_ref, *, ctx_tiles):
    x = jnp.where(pl.program_id(0) < ctx_tiles, xc_ref[...], xl_ref[...])
    mod = mod_ref[0]
    h = _rms(x, g0_ref[...]) * (1.0 + mod[1:2]) + mod[0:1]
    hb = h.astype(BF16)
    z_mla = jnp.dot(hb, win_ref[:, 1536:2048], preferred_element_type=F32)
    z_qk = jnp.dot(hb, win_ref[:, 0:512], preferred_element_type=F32)
    cqn = _rms(z_mla[:, 0:256], qn_ref[...])
    q = jnp.dot(cqn.astype(BF16), wuq_ref[...], preferred_element_type=F32)
    z_rest = jnp.dot(hb, win_ref[:, 512:1536], preferred_element_type=F32)
    rc = rc_ref[...]
    rs = rs_ref[...]
    rq_ref[...] = _rope(z_qk[:, 0:256], rc, rs, 16)
    rk_ref[...] = _rope(z_qk[:, 256:512] * (RET_DK ** -0.5), rc, rs, 16)
    mc = mc_ref[...]
    ms = ms_ref[...]
    ckv_ref[...] = _rms(z_mla[:, 256:384], kvn_ref[...])
    kr_ref[...] = _rope(z_mla[:, 384:512], mc, ms, 8)
    for h_i in range(MLA_H):
        sl = slice(h_i * LANES, (h_i + 1) * LANES)
        q_ref[:, sl] = _rope(q[:, sl], mc, ms, 8).astype(BF16)
    rv_ref[...] = z_rest[:, 0:256]
    rg_ref[...] = z_rest[:, 256:512]
    lx_ref[...] = z_rest[:, 512:768]
    ly_ref[...] = z_rest[:, 768:1024]


def _in_projection(x_ctx, x_lat, mod, g0, win_p, tabs, qn, kvn, wuq_p, geo):
    n = geo.n
    nt = n // geo.tile
    rc, rs, mc, ms = tabs
    row = lambda w: pl.BlockSpec((geo.tile, w), lambda i: (i, 0))
    full = lambda a: pl.BlockSpec(a.shape, lambda i: (0,) * a.ndim)
    tab = lambda w: pl.BlockSpec((geo.tile, w), lambda i: (geo.rope_block(i), 0))
    outs = [(256, F32)] * 6 + [(MLA_H * LANES, BF16), (LANES, F32), (LANES, F32)]
    return pl.pallas_call(
        functools.partial(_inproj_kernel, ctx_tiles=geo.ctx_tiles),
        grid=(nt,),
        in_specs=[geo.ctx_rows(D_MODEL), geo.lat_rows(D_MODEL),
                  pl.BlockSpec((1, 6, D_MODEL), lambda i: (geo.mod_group(i), 0, 0)),
                  full(g0), full(win_p), tab(256), tab(256), tab(LANES), tab(LANES), full(qn), full(kvn), full(wuq_p)],
        out_specs=[row(w) for w, _ in outs],
        out_shape=[jax.ShapeDtypeStruct((n, w), dt) for w, dt in outs],
        compiler_params=_cparams(("arbitrary",)),
        name="in_projection",
    )(x_ctx, x_lat, mod, g0, win_p, rc, rs, mc, ms, qn, kvn, wuq_p)


def _ret_kernel(lg_ref, qf_ref, kf_ref, vf_ref, qb_ref, kb_ref, vb_ref, s0_ref,
                of_ref, ob_ref, sfin_ref, s_scr, intra_scr, cross_scr, into_scr, carry_scr):
    j = pl.program_id(1)
    c = RET_CHUNK
    row = lax.broadcasted_iota(jnp.int32, (c, c), 0)
    lane = lax.broadcasted_iota(jnp.int32, (c, c), 1)
    rowf = row.astype(F32)
    lanef = lane.astype(F32)

    @pl.when(j == 0)
    def _init():
        s_scr[...] = s0_ref[0]
        for d in range(2):
            for h in range(RET_H):
                lg = lg_ref[d, h]
                if d == 0:
                    keep = row >= lane
                    dist = rowf - lanef
                else:
                    keep = lane >= row
                    dist = lanef - rowf
                intra_scr[d, h] = jnp.where(keep, jnp.exp(jnp.where(keep, dist, 0.0) * lg), 0.0)
            for p in range(2):
                lgl = jnp.where(lane < RET_DK, lg_ref[d, 2 * p], lg_ref[d, 2 * p + 1])
                if d == 0:
                    cross_scr[d, p] = jnp.exp((rowf + 1.0) * lgl)
                    into_scr[d, p] = jnp.exp((c - 1.0 - rowf) * lgl)
                else:
                    cross_scr[d, p] = jnp.exp((c - rowf) * lgl)
                    into_scr[d, p] = jnp.exp(rowf * lgl)
                carry_scr[d, p] = jnp.exp(float(c) * lgl)

    same_head = (row < RET_DK) == (lane < RET_DK)

    chains = [(d, p, refs) for d, refs in ((0, (qf_ref, kf_ref, vf_ref, of_ref)), (1, (qb_ref, kb_ref, vb_ref, ob_ref)))
              for p in range(2)]
    n_sub = qf_ref.shape[0] // c
    for sub in range(n_sub):
        staged = []
        for d, p, (q_ref, k_ref, v_ref, _) in chains:
            rows = slice(sub * c, (sub + 1) * c) if d == 0 else slice((n_sub - 1 - sub) * c, (n_sub - sub) * c)
            sl = slice(p * LANES, (p + 1) * LANES)
            q2b = q_ref[rows, sl].astype(BF16)
            k2 = k_ref[rows, sl]
            v2 = v_ref[rows, sl]
            scores, vals = [], []
            for e in range(2):
                sel = (lane >= RET_DK) if e else (lane < RET_DK)
                ke = jnp.where(sel, k2, 0.0).astype(BF16)
                vals.append(jnp.where(sel, v2, 0.0).astype(BF16))
                scores.append(lax.dot_general(q2b, ke, (((1,), (1,)), ((), ())), preferred_element_type=F32))
            st = s_scr[d, p]
            from_state = jnp.dot(q2b, st.astype(BF16), preferred_element_type=F32)
            kw = (k2 * into_scr[d, p]).astype(BF16)
            upd = lax.dot_general(kw, v2.astype(BF16), (((0,), (0,)), ((), ())), preferred_element_type=F32)
            staged.append((rows, scores, vals, st, from_state, upd))
        for (d, p, (_, _, _, o_ref)), (rows, scores, vals, st, from_state, upd) in zip(chains, staged):
            o = from_state * cross_scr[d, p]
            for e in range(2):
                a = (scores[e] * intra_scr[d, 2 * p + e]).astype(BF16)
                o = o + jnp.dot(a, vals[e], preferred_element_type=F32)
            s_scr[d, p] = st * carry_scr[d, p] + jnp.where(same_head, upd, 0.0)
            o_ref[rows, p * LANES:(p + 1) * LANES] = o

    @pl.when(j == pl.num_programs(1) - 1)
    def _fin():
        sfin_ref[0] = s_scr[...]


def _retention(rq, rk, rv, lg, s0, nb, t, row0):
    n = nb * t
    step_rows = min(RET_STEP_CHUNKS * RET_CHUNK, t)
    nc = t // step_rows
    base = row0 // step_rows
    fwd = lambda off: pl.BlockSpec((step_rows, RET_W), lambda b, j: (off + b * nc + j, 0))
    bwd = lambda off: pl.BlockSpec((step_rows, RET_W), lambda b, j: (off + b * nc + nc - 1 - j, 0))
    st = pl.BlockSpec((1, 2, 2, LANES, LANES), lambda b, j: (b, 0, 0, 0, 0))
    return pl.pallas_call(
        _ret_kernel,
        grid=(nb, nc),
        in_specs=[pl.BlockSpec(memory_space=pltpu.SMEM), fwd(base), fwd(base), fwd(base), bwd(base), bwd(base),
                  bwd(base), st],
        out_specs=[fwd(0), bwd(0), st],
        out_shape=[jax.ShapeDtypeStruct((n, RET_W), F32), jax.ShapeDtypeStruct((n, RET_W), F32),
                   jax.ShapeDtypeStruct((nb, 2, 2, LANES, LANES), F32)],
        scratch_shapes=[pltpu.VMEM((2, 2, LANES, LANES), F32), pltpu.VMEM((2, RET_H, RET_CHUNK, RET_CHUNK), F32),
                        pltpu.VMEM((2, 2, RET_CHUNK, LANES), F32), pltpu.VMEM((2, 2, RET_CHUNK, LANES), F32),
                        pltpu.VMEM((2, 2, RET_CHUNK, LANES), F32)],
        compiler_params=_cparams(("arbitrary", "arbitrary")),
        name="retention",
    )(lg, rq, rk, rv, rq, rk, rv, s0)


def _state_to_pairs(s):
    b = s.shape[0]
    s = s.reshape(b, 2, 2, 2, RET_DK, RET_DK)
    z = jnp.zeros_like(s[:, :, :, 0])
    top = jnp.concatenate([s[:, :, :, 0], z], axis=-1)
    bot = jnp.concatenate([z, s[:, :, :, 1]], axis=-1)
    return jnp.concatenate([top, bot], axis=-2)


def _pairs_to_state(s):
    b = s.shape[0]
    a = s[..., :RET_DK, :RET_DK]
    c = s[..., RET_DK:, RET_DK:]
    return jnp.stack([a, c], axis=3).reshape(b, 2, RET_H, RET_DK, RET_DK)


def _scan_rows(a, b, reverse):
    r = a.shape[0]
    rows = lax.broadcasted_iota(jnp.int32, a.shape, 0)
    s = 1
    while s < r:
        if reverse:
            a_s = pltpu.roll(a, r - s, 0)
            b_s = pltpu.roll(b, r - s, 0)
            m = rows < r - s
        else:
            a_s = pltpu.roll(a, s, 0)
            b_s = pltpu.roll(b, s, 0)
            m = rows >= s
        b = jnp.where(m, a * b_s + b, b)
        a = jnp.where(m, a * a_s, a)
        s *= 2
    return a, b


def _lru_kernel(c8_ref, cw_ref, cb_ref, wg_ref, bg_ref, lx_ref, ly_ref, h0_ref, o_ref, hfin_ref, xc_scr, hf_scr, hb_scr,
                *, t, r):
    nch = t // r
    cw = cw_ref[0]
    cb = cb_ref[0]
    wg = wg_ref[0]
    bg = bg_ref[0]
    c8 = c8_ref[0]

    def conv_body(c, carry):
        r0 = pl.multiple_of(c * r, r)
        cur = lx_ref[pl.ds(r0, r), :]
        prev = lx_ref[pl.ds(pl.multiple_of(jnp.maximum(r0 - 8, 0), 8), 8), :]
        nxt = lx_ref[pl.ds(pl.multiple_of(jnp.minimum(r0 + r, t - 8), 8), 8), :]
        prev = jnp.where(c > 0, prev, 0.0)
        nxt = jnp.where(c < nch - 1, nxt, 0.0)
        ext = jnp.concatenate([prev, cur, nxt], axis=0)
        acc = jnp.broadcast_to(cb, (r, LANES))
        for tap in range(4):
            sh = (2 - tap) % (r + 16)
            xs = ext if sh == 0 else pltpu.roll(ext, sh, 0)
            acc = acc + xs[8:8 + r] * cw[tap:tap + 1]
        xc_scr[pl.ds(r0, r), :] = acc
        return carry

    lax.fori_loop(0, nch, conv_body, 0)

    def gates(xc, d):
        g = jnp.dot(xc.astype(BF16), wg[:, d * 256:(d + 1) * 256], preferred_element_type=F32) + bg[:, d * 256:(d + 1) * 256]
        rg = jax.nn.sigmoid(g[:, :LANES])
        ig = jax.nn.sigmoid(g[:, LANES:])
        log_a = c8[d:d + 1] * rg
        a = jnp.exp(log_a)
        b = jnp.sqrt(jnp.tanh(-log_a) * (a * a + 1.0)) * (ig * xc)
        return a, b

    def scan_body(c, carry):
        h_f, h_b = carry
        rf = pl.multiple_of(c * r, r)
        rb = pl.multiple_of((nch - 1 - c) * r, r)
        a_f, b_f = gates(xc_scr[pl.ds(rf, r), :], 0)
        a_b, b_b = gates(xc_scr[pl.ds(rb, r), :], 1)
        a_f, b_f = _scan_rows(a_f, b_f, False)
        a_b, b_b = _scan_rows(a_b, b_b, True)
        hc_f = a_f * h_f + b_f
        hc_b = a_b * h_b + b_b
        hf_scr[pl.ds(rf, r), :] = hc_f
        hb_scr[pl.ds(rb, r), :] = hc_b
        return hc_f[r - 1:r], hc_b[0:1]

    h_f, h_b = lax.fori_loop(0, nch, scan_body, (h0_ref[0, 0:1, :], h0_ref[0, 1:2, :]), unroll=min(2, nch))

    def out_body(c, carry):
        r0 = pl.multiple_of(c * r, r)
        rows = pl.ds(r0, r)
        o_ref[rows, :] = ((hf_scr[rows, :] + hb_scr[rows, :]) * jax.nn.gelu(ly_ref[rows, :])).astype(BF16)
        return carry

    lax.fori_loop(0, nch, out_body, 0)
    hfin_ref[0, 0:1, :] = h_f
    hfin_ref[0, 1:2, :] = h_b


def _lru(lx, ly, h0, c8, cw, cb, wg, bg, nb, t, row0):
    base = row0 // t
    r = min(LRU_ROWS, t)
    seq = lambda off: pl.BlockSpec((t, LANES), lambda b, hh: (off + b, hh))
    par = lambda a: pl.BlockSpec((1,) + a.shape[1:], lambda b, hh: (hh,) + (0,) * (a.ndim - 1))
    st = pl.BlockSpec((1, 2, LANES), lambda b, hh: (b, 0, hh))
    return pl.pallas_call(
        functools.partial(_lru_kernel, t=t, r=r),
        grid=(nb, 2),
        in_specs=[par(c8), par(cw), par(cb), par(wg), par(bg), seq(base), seq(base), st],
        out_specs=[seq(0), st],
        out_shape=[jax.ShapeDtypeStruct((nb * t, LRU_W), BF16), jax.ShapeDtypeStruct((nb, 2, LRU_W), F32)],
        scratch_shapes=[pltpu.VMEM((t, LANES), F32)] * 3,
        compiler_params=_cparams(("arbitrary", "arbitrary")),
        name="rg_lru",
    )(c8, cw, cb, wg, bg, lx, ly, h0)


def _mla_kernel(*refs, n_ctx, t, tq):
    if n_ctx:
        q_ref, ckv_ref, kr_ref, cckv_ref, ckr_ref, wk_ref, wv_ref = refs[:7]
    else:
        q_ref, ckv_ref, kr_ref, wk_ref, wv_ref = refs[:5]
    o_ref, k_scr, vt_scr, qt_scr, ot_scr = refs[-5:]
    qi = pl.program_id(1)
    cexp = (MLA_DN + MLA_DR) ** -0.5 * math.log2(math.e)
    ones_row = (lax.broadcasted_iota(jnp.int32, (MLA_H * MLA_VT_ROWS, 1), 0) % MLA_VT_ROWS == MLA_DV).astype(F32)

    def put(ckv, kr, c0):
        nrow = ckv.shape[0]
        ckv_b = ckv.astype(BF16)
        kn = jnp.dot(ckv_b, wk_ref[...], preferred_element_type=F32)
        vt = lax.dot_general(wv_ref[...], ckv_b, (((1,), (1,)), ((), ())), preferred_element_type=F32) + ones_row
        for h in range(MLA_H):
            k_scr[h, c0:c0 + nrow, :] = ((kn[:, h * LANES:(h + 1) * LANES] + kr) * cexp).astype(BF16)
            vt_scr[h, :, c0:c0 + nrow] = vt[h * MLA_VT_ROWS:(h + 1) * MLA_VT_ROWS].astype(BF16)

    @pl.when(qi == 0)
    def _build():
        if n_ctx:
            put(cckv_ref[0], ckr_ref[0], 0)
        step = min(KV_BUILD_ROWS, t)
        for c in range(t // step):
            put(ckv_ref[c * step:(c + 1) * step, :], kr_ref[c * step:(c + 1) * step, :], n_ctx + c * step)

    qt_scr[...] = q_ref[...].astype(F32).T.astype(BF16)
    s_len = n_ctx + t

    chunks = [slice(c0, min(c0 + ATT_KC, s_len)) for c0 in range(0, s_len, ATT_KC)]
    nchunk = len(chunks)
    steps = [(h, c) for h in range(MLA_H) for c in range(nchunk)]

    def score(i):
        h, c = steps[i]
        return jnp.dot(k_scr[h, chunks[c], :], qt_scr[h * LANES:(h + 1) * LANES, :], preferred_element_type=F32)

    pending = {i: score(i) for i in range(min(ATT_AHEAD, len(steps)))}
    held = None
    m = o = None
    for i in range(len(steps) + 1):
        if i + ATT_AHEAD < len(steps):
            pending[i + ATT_AHEAD] = score(i + ATT_AHEAD)
        if i < len(steps):
            h, c = steps[i]
            s = pending.pop(i)
            m_old = jnp.full((1, tq), -1e30, F32) if c == 0 else m
            m = jnp.maximum(m_old, jnp.max(s, axis=0, keepdims=True))
            p = jnp.exp2(s - m).astype(BF16)
            alpha = jnp.exp2(m_old - m)
        if held is not None:
            hh, cc, p_h, alpha_h = held
            pv = jnp.dot(vt_scr[hh, :, chunks[cc]], p_h, preferred_element_type=F32)
            o = pv if cc == 0 else o * alpha_h + pv
            if cc == nchunk - 1:
                ot_scr[hh] = o[:MLA_DV] / o[MLA_DV:MLA_DV + 1]
        held = (h, c, p, alpha) if i < len(steps) else None
    for pp in range(MLA_H // 2):
        pair = jnp.concatenate([ot_scr[2 * pp], ot_scr[2 * pp + 1]], axis=0)
        o_ref[:, pp * LANES:(pp + 1) * LANES] = pair.T.astype(BF16)


def _mla(q, ckvn, kr, cache, wk, wv_t, nb, t, row0):
    tq = min(ATT_TQ, t)
    nq = t // tq
    n_ctx = 0 if cache is None else cache[0].shape[1]
    s_len = n_ctx + t
    qspec = pl.BlockSpec((tq, MLA_H * LANES), lambda b, i: (row0 // tq + b * nq + i, 0))
    seq = pl.BlockSpec((t, LANES), lambda b, i: (row0 // t + b, 0))
    full = lambda a: pl.BlockSpec(a.shape, lambda b, i: (0,) * a.ndim)
    ins = [q, ckvn, kr]
    specs = [qspec, seq, seq]
    if n_ctx:
        cspec = pl.BlockSpec((1, n_ctx, LANES), lambda b, i: (b, 0, 0))
        ins += [cache[0], cache[1]]
        specs += [cspec, cspec]
    ins += [wk, wv_t]
    specs += [full(wk), full(wv_t)]
    return pl.pallas_call(
        functools.partial(_mla_kernel, n_ctx=n_ctx, t=t, tq=tq),
        grid=(nb, nq),
        in_specs=specs,
        out_specs=pl.BlockSpec((tq, MLA_W), lambda b, i: (b * nq + i, 0)),
        out_shape=jax.ShapeDtypeStruct((nb * t, MLA_W), BF16),
        scratch_shapes=[pltpu.VMEM((MLA_H, s_len, LANES), BF16), pltpu.VMEM((MLA_H, MLA_VT_ROWS, s_len), BF16),
                        pltpu.VMEM((MLA_H * LANES, tq), BF16), pltpu.VMEM((MLA_H, MLA_DV, tq), F32)],
        compiler_params=_cparams(("arbitrary", "arbitrary")),
        name="mla_attention",
    )(*ins)


def _outproj_kernel(ofc_ref, obc_ref, lruc_ref, mlac_ref, xc_ref, ofl_ref, obl_ref, lrul_ref, mlal_ref, xl_ref, rg_ref,
                    mod_ref, retn_ref, g1_ref, g2_ref, wout_ref, rw_ref, rb_ref, x1_ref, h2_ref, route_ref, cnt_ref,
                    cnt_scr, *, ctx_tiles):
    is_ctx = pl.program_id(0) < ctx_tiles
    pick = lambda c_ref, l_ref: jnp.where(is_ctx, c_ref[...], l_ref[...])
    o = pick(ofc_ref, ofl_ref) + pick(obc_ref, obl_ref)
    hid = lax.broadcasted_iota(jnp.int32, o.shape, 1) // RET_DK

    def head_sum(v):
        tot = jnp.zeros_like(v)
        for hh in range(RET_H):
            msk = hid == hh
            tot = jnp.where(msk, jnp.sum(jnp.where(msk, v, 0.0), axis=1, keepdims=True), tot)
        return tot

    mu = head_sum(o) * (1.0 / RET_DK)
    dl = o - mu
    var = head_sum(dl * dl) * (1.0 / RET_DK)
    rg = rg_ref[...]
    ret = dl * lax.rsqrt(var + EPS) * retn_ref[...] * (rg * jax.nn.sigmoid(rg))
    mix = jnp.concatenate([ret.astype(BF16), pick(lruc_ref, lrul_ref), pick(mlac_ref, mlal_ref)], axis=1)
    mo = jnp.dot(mix, wout_ref[...], preferred_element_type=F32)
    mod = mod_ref[0]
    x1 = pick(xc_ref, xl_ref) + mod[2:3] * _rms(mo, g1_ref[...])
    x1_ref[...] = x1
    h2 = _rms(x1, g2_ref[...]) * (1.0 + mod[4:5]) + mod[3:4]
    _store_row_tiles(h2_ref, h2)

    tm = h2.shape[0]
    lg = jnp.dot(h2.astype(BF16), rw_ref[...], preferred_element_type=F32) + rb_ref[...]
    lgt = lg.T[:N_EXPERTS, :]
    rowf = lax.broadcasted_iota(jnp.int32, (N_EXPERTS, tm), 0).astype(F32)
    tops, idxs, hots = [], [], []
    for _ in range(TOP_K):
        m = jnp.max(lgt, axis=0, keepdims=True)
        idx = jnp.min(jnp.where(lgt == m, rowf, float(N_EXPERTS)), axis=0, keepdims=True)
        hot = rowf == idx
        lgt = jnp.where(hot, -jnp.inf, lgt)
        tops.append(m)
        idxs.append(idx)
        hots.append(hot)
    exps = [jnp.exp(t - tops[0]) for t in tops]
    den = exps[0] + exps[1] + exps[2] + exps[3]
    member = jnp.zeros((N_EXPERTS, tm), F32)
    for hot in hots:
        member = member + hot.astype(F32)

    @pl.when(pl.program_id(0) == 0)
    def _zero_counts():
        cnt_scr[...] = jnp.zeros_like(cnt_scr)

    earlier = (lax.broadcasted_iota(jnp.int32, (tm, tm), 0) < lax.broadcasted_iota(jnp.int32, (tm, tm), 1))
    counts = cnt_scr[...]
    before = jnp.dot(member.astype(BF16), earlier.astype(BF16), preferred_element_type=F32) + counts[:, 0:1]
    ranks = [jnp.sum(jnp.where(hot, before, 0.0), axis=0, keepdims=True) for hot in hots]
    gates = [e / den for e in exps]
    route_ref[0] = jnp.concatenate(idxs + gates + ranks + [jnp.zeros((TOP_K, tm), F32)], axis=0)
    counts = counts + jnp.sum(member, axis=1, keepdims=True)
    cnt_scr[...] = counts
    cnt_ref[...] = counts


def _out_projection(ctx_mix, lat_mix, rg, mod, retn, g1, g2, wout, rw, rb, geo):
    n = geo.n
    row = lambda w: pl.BlockSpec((ROW_TILE, w), lambda i: (i, 0))
    full = lambda a: pl.BlockSpec(a.shape, lambda i: (0,) * a.ndim)
    widths = (256, 256, 256, MLA_W, D_MODEL)
    return pl.pallas_call(
        functools.partial(_outproj_kernel, ctx_tiles=geo.ctx_tiles),
        grid=(n // ROW_TILE,),
        in_specs=[geo.ctx_rows(w) for w in widths] + [geo.lat_rows(w) for w in widths] + [
                  row(256),
                  pl.BlockSpec((1, 6, D_MODEL), lambda i: (geo.mod_group(i), 0, 0)),
                  full(retn), full(g1), full(g2), full(wout), full(rw), full(rb)],
        out_specs=[row(D_MODEL), pl.BlockSpec((ROW_TILE * ROW_CHUNKS, LANES), lambda i: (i, 0)),
                   pl.BlockSpec((1, 4 * TOP_K, ROW_TILE), lambda i: (i, 0, 0)),
                   pl.BlockSpec((N_EXPERTS, LANES), lambda i: (0, 0))],
        out_shape=[jax.ShapeDtypeStruct((n, D_MODEL), F32), jax.ShapeDtypeStruct((n * ROW_CHUNKS, LANES), F32),
                   jax.ShapeDtypeStruct((n // ROW_TILE, 4 * TOP_K, ROW_TILE), F32),
                   jax.ShapeDtypeStruct((N_EXPERTS, LANES), F32)],
        scratch_shapes=[pltpu.VMEM((N_EXPERTS, LANES), F32)],
        compiler_params=_cparams(("arbitrary",)),
        name="out_projection",
    )(*ctx_mix, *lat_mix, rg, mod, retn, g1, g2, wout, rw, rb)


def _dispatch_kernel(pe_ref, slot_ref, h_ref, xs_hbm, zero_scr, sem):
    i = pl.program_id(0)
    n_slots = xs_hbm.shape[0] // ROW_CHUNKS

    def slot_rows(first_slot, n):
        return xs_hbm.at[pl.ds(pl.multiple_of(first_slot * ROW_CHUNKS, ROW_CHUNKS), n * ROW_CHUNKS), :]

    @pl.when(i == 0)
    def _zero_pads():
        zero_scr[...] = jnp.zeros_like(zero_scr)

        def fill(e):
            end = pe_ref[e]
            start = 0 if e == 0 else pe_ref[e - 1]
            return end > start, pltpu.make_async_copy(zero_scr, slot_rows(jnp.maximum(end - MOE_TILE, 0), MOE_TILE), sem)

        def fill_tail(j):
            row = pe_ref[N_EXPERTS - 1] + j * MOE_TILE
            dst = slot_rows(jnp.minimum(row, n_slots - MOE_TILE), MOE_TILE)
            return row < n_slots, pltpu.make_async_copy(zero_scr, dst, sem)

        for e in range(N_EXPERTS):
            for todo, cp in (fill(e), fill_tail(e)):
                pl.when(todo)(cp.start)
        for e in range(N_EXPERTS):
            for todo, cp in (fill(e), fill_tail(e)):
                pl.when(todo)(cp.wait)

    for r in range(ROW_TILE):
        for k in range(TOP_K):
            first = pl.multiple_of(slot_ref[0, 0, k * ROW_TILE + r], ROW_CHUNKS)
            dst = xs_hbm.at[pl.ds(first, ROW_CHUNKS), :]
            pltpu.make_async_copy(h_ref.at[pl.ds(r * ROW_CHUNKS, ROW_CHUNKS), :], dst, sem).start(priority=k % 2)
    for k in range(TOP_K):
        pltpu.make_async_copy(h_ref, xs_hbm.at[pl.ds(0, ROW_TILE * ROW_CHUNKS), :], sem).wait()


def _dispatch(h2, slots3, pad_end, n_slots):
    n = h2.shape[0] // ROW_CHUNKS
    grid_spec = pltpu.PrefetchScalarGridSpec(
        num_scalar_prefetch=1,
        grid=(n // ROW_TILE,),
        in_specs=[
            pl.BlockSpec((1, 1, TOP_K * ROW_TILE), lambda i, pe: (i, 0, 0), memory_space=pltpu.SMEM),
            pl.BlockSpec((ROW_TILE * ROW_CHUNKS, LANES), lambda i, pe: (i, 0)),
        ],
        out_specs=pl.BlockSpec(memory_space=pl.ANY),
        scratch_shapes=[pltpu.VMEM((MOE_TILE * ROW_CHUNKS, LANES), F32), pltpu.SemaphoreType.DMA(())],
    )
    return pl.pallas_call(
        _dispatch_kernel,
        grid_spec=grid_spec,
        out_shape=jax.ShapeDtypeStruct((n_slots * ROW_CHUNKS, LANES), F32),
        compiler_params=_cparams(("arbitrary",)),
        name="moe_dispatch",
    )(pad_end, slots3, h2)


def _moe_kernel(te_ref, nu_ref, x_ref, wgu_ref, bgu_ref, wd_ref, bd_ref, y_ref, wgu_b, wd_b):
    t = pl.program_id(0)
    n_used = nu_ref[0]
    nch = D_EXPERT // MOE_HCHUNK
    new_expert = jnp.logical_or(t == 0, te_ref[t] != te_ref[jnp.maximum(t - 1, 0)])

    @pl.when(jnp.logical_and(new_expert, t < n_used))
    def _cast_weights():
        for j in range(nch):
            cols = slice(j * MOE_HCHUNK, (j + 1) * MOE_HCHUNK)
            ucols = slice(D_EXPERT + j * MOE_HCHUNK, D_EXPERT + (j + 1) * MOE_HCHUNK)
            wgu_b[j, :, :MOE_HCHUNK] = wgu_ref[0, 0, :, cols].astype(BF16)
            wgu_b[j, :, MOE_HCHUNK:] = wgu_ref[0, 0, :, ucols].astype(BF16)
        wd_b[...] = wd_ref[0, 0].astype(BF16)

    @pl.when(t < n_used)
    def _compute():
        x = _load_row_tiles(x_ref).astype(BF16)
        bgu = bgu_ref[pl.ds(te_ref[t], 1), :]

        def gate_up(j):
            return jnp.dot(x, wgu_b[j], preferred_element_type=F32)

        pending = gate_up(0)
        y = None
        for j in range(nch):
            gu = pending
            if j + 1 < nch:
                pending = gate_up(j + 1)
            cols = slice(j * MOE_HCHUNK, (j + 1) * MOE_HCHUNK)
            ucols = slice(D_EXPERT + j * MOE_HCHUNK, D_EXPERT + (j + 1) * MOE_HCHUNK)
            g = jnp.minimum(gu[:, :MOE_HCHUNK] + bgu[:, cols], SWIGLU_LIMIT)
            u = jnp.clip(gu[:, MOE_HCHUNK:] + bgu[:, ucols], -SWIGLU_LIMIT, SWIGLU_LIMIT)
            act = (g * jax.nn.sigmoid(SWIGLU_ALPHA * g) * (u + 1.0)).astype(BF16)
            part = jnp.dot(act, wd_b[cols, :], preferred_element_type=F32)
            y = part if y is None else y + part
        _store_row_tiles(y_ref, y + bd_ref[pl.ds(te_ref[t], 1), :])

    @pl.when(t >= n_used)
    def _idle():
        y_ref[...] = jnp.zeros_like(y_ref)


def _moe(xs, tile_expert, n_used, layer, wgu, bgu, wd, bd):
    n_tiles = tile_expert.shape[0]
    wspec = lambda r, c: pl.BlockSpec((1, 1, r, c), lambda t, te, nu: (layer, te[t], 0, 0))
    bspec = lambda c: pl.BlockSpec((N_EXPERTS, c), lambda t, te, nu: (0, 0))
    grid_spec = pltpu.PrefetchScalarGridSpec(
        num_scalar_prefetch=2,
        grid=(n_tiles,),
        in_specs=[
            pl.BlockSpec((MOE_TILE * ROW_CHUNKS, LANES),
                         lambda t, te, nu: (jnp.minimum(t, jnp.maximum(nu[0] - 1, 0)), 0)),
            wspec(D_MODEL, 2 * D_EXPERT), bspec(2 * D_EXPERT), wspec(D_EXPERT, D_MODEL), bspec(D_MODEL),
        ],
        out_specs=pl.BlockSpec((MOE_TILE * ROW_CHUNKS, LANES), lambda t, te, nu: (t, 0)),
        scratch_shapes=[pltpu.VMEM((D_EXPERT // MOE_HCHUNK, D_MODEL, 2 * MOE_HCHUNK), BF16),
                        pltpu.VMEM((D_EXPERT, D_MODEL), BF16)],
    )
    return pl.pallas_call(
        _moe_kernel,
        grid_spec=grid_spec,
        out_shape=jax.ShapeDtypeStruct((n_tiles * MOE_TILE * ROW_CHUNKS, LANES), F32),
        compiler_params=_cparams(("arbitrary",)),
        name="moe_experts",
    )(tile_expert, n_used, xs, wgu, bgu[layer], wd, bd[layer])


def _combine_kernel(inv_ref, invn_ref, y_hbm, gate_ref, x1_ref, mod_ref, g3_ref, oc_ref, ol_ref, buf, sem, *,
                    ctx_tiles):
    i = pl.program_id(0)
    cur = lax.rem(i, 2)

    def gather(idx_ref, half):
        for r in range(ROW_TILE):
            for k in range(TOP_K):
                first = pl.multiple_of(idx_ref[0, 0, k * ROW_TILE + r], ROW_CHUNKS)
                pltpu.make_async_copy(y_hbm.at[pl.ds(first, ROW_CHUNKS), :],
                                      buf.at[half, k, pl.ds(r * ROW_CHUNKS, ROW_CHUNKS), :],
                                      sem.at[half]).start(priority=k % 2)

    @pl.when(i == 0)
    def _first():
        gather(inv_ref, 0)

    has_next = i + 1 < pl.num_programs(0)
    for half in range(2):
        pl.when(jnp.logical_and(has_next, cur != half))(functools.partial(gather, invn_ref, half))

    for k in range(TOP_K):
        pltpu.make_async_copy(y_hbm.at[pl.ds(0, ROW_TILE * ROW_CHUNKS), :], buf.at[cur, k], sem.at[cur]).wait()
    gate = gate_ref[...]
    ff = gate[:, 0:1] * _load_row_tiles(buf.at[cur, 0])
    for k in range(1, TOP_K):
        ff = ff + gate[:, k:k + 1] * _load_row_tiles(buf.at[cur, k])
    out = x1_ref[...] + mod_ref[0][5:6] * _rms(ff, g3_ref[...])
    is_ctx = pl.program_id(0) < ctx_tiles

    @pl.when(is_ctx)
    def _ctx():
        oc_ref[...] = out

    @pl.when(jnp.logical_not(is_ctx))
    def _lat():
        ol_ref[...] = out


def _combine(y_sorted, inv3, gate, x1, mod, g3, geo):
    n = x1.shape[0]
    nt = n // ROW_TILE
    return pl.pallas_call(
        functools.partial(_combine_kernel, ctx_tiles=geo.ctx_tiles),
        grid=(nt,),
        in_specs=[
            pl.BlockSpec((1, 1, TOP_K * ROW_TILE), lambda i: (i, 0, 0), memory_space=pltpu.SMEM),
            pl.BlockSpec((1, 1, TOP_K * ROW_TILE), lambda i: (jnp.minimum(i + 1, nt - 1), 0, 0),
                         memory_space=pltpu.SMEM),
            pl.BlockSpec(memory_space=pl.ANY),
            pl.BlockSpec((ROW_TILE, TOP_K), lambda i: (i, 0)),
            pl.BlockSpec((ROW_TILE, D_MODEL), lambda i: (i, 0)),
            pl.BlockSpec((1, 6, D_MODEL), lambda i: (geo.mod_group(i), 0, 0)),
            pl.BlockSpec((1, D_MODEL), lambda i: (0, 0)),
        ],
        out_specs=[geo.ctx_rows(D_MODEL), geo.lat_rows(D_MODEL)],
        out_shape=[jax.ShapeDtypeStruct((geo.n_ctx, D_MODEL), F32), jax.ShapeDtypeStruct((n - geo.n_ctx, D_MODEL), F32)],
        scratch_shapes=[pltpu.VMEM((2, TOP_K, ROW_TILE * ROW_CHUNKS, LANES), F32), pltpu.SemaphoreType.DMA((2,))],
        compiler_params=_cparams(("arbitrary",)),
        name="moe_combine",
    )(inv3, inv3, y_sorted, gate, x1, mod, g3)


def _plan(route, counts):
    nt = route.shape[0]
    n = nt * ROW_TILE
    expert = route[:, 0:TOP_K, :].astype(jnp.int32)
    gate = route[:, TOP_K:2 * TOP_K, :].transpose(0, 2, 1).reshape(n, TOP_K)
    rank = route[:, 2 * TOP_K:3 * TOP_K, :].astype(jnp.int32)
    cnt = counts[:, 0].astype(jnp.int32)
    padded = (cnt + MOE_TILE - 1) // MOE_TILE * MOE_TILE
    pad_end = jnp.cumsum(padded).astype(jnp.int32)
    pad_start = pad_end - padded
    start_of = jnp.sum(jnp.where(expert[..., None] == jnp.arange(N_EXPERTS), pad_start, 0), axis=-1)
    slots3 = ((start_of + rank) * ROW_CHUNKS).reshape(nt, 1, TOP_K * ROW_TILE)
    n_tiles = n * TOP_K // MOE_TILE + N_EXPERTS
    tile_start = jnp.arange(n_tiles, dtype=jnp.int32) * MOE_TILE
    tile_expert = jnp.minimum(jnp.sum((tile_start[:, None] >= pad_end[None, :]).astype(jnp.int32), axis=1),
                              N_EXPERTS - 1)
    n_used = (pad_end[-1] // MOE_TILE).reshape(1)
    return gate, slots3, pad_end, tile_expert, n_used, n_tiles * MOE_TILE


class _Geometry:
    def __init__(self, nb_ctx, t_ctx, nb_lat, t_lat, tile):
        self.nb_ctx, self.t_ctx, self.nb_lat, self.t_lat, self.tile = nb_ctx, t_ctx, nb_lat, t_lat, tile
        self.n_ctx = nb_ctx * t_ctx
        self.n = self.n_ctx + nb_lat * t_lat
        self.ctx_tiles = self.n_ctx // tile
        self.lat_tiles = t_lat // tile

    def mod_group(self, i):
        return jnp.where(i < self.ctx_tiles, 0, 1 + (i - self.ctx_tiles) // self.lat_tiles)

    def ctx_rows(self, w):
        return pl.BlockSpec((self.tile, w), lambda i: (jnp.minimum(i, self.ctx_tiles - 1), 0))

    def lat_rows(self, w):
        return pl.BlockSpec((self.tile, w), lambda i: (jnp.maximum(i - self.ctx_tiles, 0), 0))

    def rope_block(self, i):
        return jnp.where(i < self.ctx_tiles, self.lat_tiles, (i - self.ctx_tiles) % self.lat_tiles)


def _rope_tables(t, tile):
    pos = jnp.arange(t)
    row = (pos // GRID_W).astype(F32)
    col = (pos % GRID_W).astype(F32)

    def cs(dim):
        q = dim // 4
        inv = ROPE_BASE ** (-jnp.arange(q, dtype=F32) / q)
        ar = row[:, None] * inv
        ac = col[:, None] * inv
        c = jnp.concatenate([jnp.cos(ar), jnp.cos(ar), jnp.cos(ac), jnp.cos(ac)], axis=-1)
        s = jnp.concatenate([-jnp.sin(ar), jnp.sin(ar), -jnp.sin(ac), jnp.sin(ac)], axis=-1)
        return c, s

    def with_identity(c, s):
        return (jnp.concatenate([c, jnp.ones((tile, c.shape[1]), F32)], axis=0),
                jnp.concatenate([s, jnp.zeros((tile, s.shape[1]), F32)], axis=0))

    c64, s64 = cs(RET_DK)
    rc, rs = with_identity(jnp.tile(c64, (1, RET_H)), jnp.tile(s64, (1, RET_H)))
    c32, s32 = cs(MLA_DR)
    pad = lambda a, v: jnp.concatenate([jnp.full((t, MLA_DN), v, F32), a, jnp.full((t, LANES - MLA_DN - MLA_DR), v, F32)], axis=-1)
    mc, ms = with_identity(pad(c32, 1.0), pad(s32, 0.0))
    return rc, rs, mc, ms


def _layer_params(l, norm_g, w_in, ret_decay, ret_norm, conv_w, conv_b, lru_gate_w, lru_gate_b, lru_lambda,
                  mla_q_norm, mla_kv_norm, mla_w_uq, mla_w_ukv, w_out, router_w, router_b):
    p = {}
    p['g'] = [norm_g[l, i].reshape(1, D_MODEL) for i in range(4)]
    kr0 = 1920
    p['win'] = jnp.concatenate([w_in[l][:, :kr0], jnp.zeros((D_MODEL, MLA_DN), F32), w_in[l][:, kr0:],
                                jnp.zeros((D_MODEL, LANES - MLA_DN - MLA_DR), F32)], axis=1).astype(BF16)
    p['lg'] = jax.nn.log_sigmoid(ret_decay[l].astype(F32))
    p['retn'] = ret_norm[l].reshape(1, RET_W)
    p['qn'] = mla_q_norm[l].reshape(1, Q_RANK)
    p['kvn'] = mla_kv_norm[l].reshape(1, KV_RANK)
    wuq = mla_w_uq[l].reshape(Q_RANK, MLA_H, MLA_DN + MLA_DR)
    p['wuq'] = jnp.pad(wuq, ((0, 0), (0, 0), (0, LANES - MLA_DN - MLA_DR))).reshape(Q_RANK, MLA_H * LANES).astype(BF16)
    wukv = mla_w_ukv[l].reshape(KV_RANK, MLA_H, MLA_DN + MLA_DV)
    p['wk'] = jnp.pad(wukv[:, :, :MLA_DN], ((0, 0), (0, 0), (0, LANES - MLA_DN))).reshape(KV_RANK, MLA_H * LANES).astype(BF16)
    wv_t = jnp.pad(wukv[:, :, MLA_DN:].transpose(1, 2, 0), ((0, 0), (0, MLA_VT_ROWS - MLA_DV), (0, 0)))
    p['wv_t'] = wv_t.reshape(MLA_H * MLA_VT_ROWS, KV_RANK).astype(BF16)
    p['wout'] = w_out[l].astype(BF16)
    p['rw'] = jnp.pad(router_w[l], ((0, 0), (0, LANES - N_EXPERTS))).astype(BF16)
    p['rb'] = jnp.pad(router_b[l], (0, LANES - N_EXPERTS)).reshape(1, LANES)
    p['cw'] = conv_w[l].reshape(4, 2, LANES).transpose(1, 0, 2)
    p['cb'] = conv_b[l].reshape(2, 1, LANES)
    gw = lru_gate_w[l].reshape(2, 2, 2, 2, LRU_BW, LRU_BW)
    diag = gw[:, :, :, :, :, None, :] * jnp.eye(2, dtype=F32)[None, None, None, :, None, :, None]
    p['wg'] = diag.transpose(2, 3, 4, 0, 1, 5, 6).reshape(2, LANES, 4 * LANES).astype(BF16)
    gb = lru_gate_b[l].reshape(2, 2, 2, LANES)
    p['bg'] = gb.transpose(2, 0, 1, 3).reshape(2, 1, 4 * LANES)
    p['c8'] = (8.0 * jax.nn.log_sigmoid(lru_lambda[l].astype(F32))).reshape(2, 2, LANES).transpose(1, 0, 2)
    return p


def _forward(x_prompt, x_sample, c, state_ret, state_lru, cache_mla_ckv, cache_mla_krope, c_ctx, ada_w, ada_b,
             *weights):
    nb_ctx, t_ctx, _ = x_prompt.shape
    nb_lat, t_lat, _ = x_sample.shape
    geo = _Geometry(nb_ctx, t_ctx, nb_lat, t_lat, ROW_TILE)
    geo_in = _Geometry(nb_ctx, t_ctx, nb_lat, t_lat, IN_TILE)
    n_c = geo.n_ctx
    x_ctx = x_prompt.reshape(n_c, D_MODEL)
    x_lat = x_sample.reshape(-1, D_MODEL)
    cond = jnp.concatenate([c_ctx[None, :], c, jnp.zeros((16 - 1 - nb_lat, D_MODEL), F32)], axis=0)
    mod_all = _modulation(cond, ada_w, ada_b)[:, :1 + nb_lat].reshape(DEPTH, 1 + nb_lat, 6, D_MODEL)
    tabs = _rope_tables(t_lat, IN_TILE)
    krope_pad = jnp.pad(cache_mla_krope, ((0, 0), (0, 0), (0, 0), (MLA_DN, LANES - MLA_DN - MLA_DR)))
    ret_out, lru_out, ckv_out, kr_out = [], [], [], []
    for l in range(DEPTH):
        p = _layer_params(l, *weights[:-4])
        mod = mod_all[l]
        rq, rk, rv, rg, lx, ly, q, ckvn, kr = _in_projection(x_ctx, x_lat, mod, p['g'][0], p['win'], tabs, p['qn'],
                                                             p['kvn'], p['wuq'], geo_in)
        zero_s = jnp.zeros((nb_ctx, 2, 2, LANES, LANES), F32)
        ofc, obc, s_ctx = _retention(rq, rk, rv, p['lg'], zero_s, nb_ctx, t_ctx, 0)
        ofl, obl, _ = _retention(rq, rk, rv, p['lg'], _state_to_pairs(state_ret[:, l]), nb_lat, t_lat, n_c)
        lru_c, h_ctx = _lru(lx, ly, jnp.zeros((nb_ctx, 2, LRU_W), F32), p['c8'], p['cw'], p['cb'], p['wg'], p['bg'],
                            nb_ctx, t_ctx, 0)
        lru_l, _ = _lru(lx, ly, state_lru[:, l], p['c8'], p['cw'], p['cb'], p['wg'], p['bg'], nb_lat, t_lat, n_c)
        mla_c = _mla(q, ckvn, kr, None, p['wk'], p['wv_t'], nb_ctx, t_ctx, 0)
        mla_l = _mla(q, ckvn, kr, (cache_mla_ckv[:, l], krope_pad[:, l]), p['wk'], p['wv_t'], nb_lat, t_lat, n_c)
        x1, h2, route, counts = _out_projection((ofc, obc, lru_c, mla_c, x_ctx), (ofl, obl, lru_l, mla_l, x_lat), rg,
                                                mod, p['retn'], p['g'][1], p['g'][2], p['wout'], p['rw'], p['rb'], geo)
        gate, slots3, pad_end, tile_expert, n_used, n_slots = _plan(route, counts)
        xs = _dispatch(h2, slots3, pad_end, n_slots)
        y_sorted = _moe(xs, tile_expert, n_used, l, *weights[-4:])
        x_ctx, x_lat = _combine(y_sorted, slots3, gate, x1, mod, p['g'][3], geo)
        ret_out.append(_pairs_to_state(s_ctx))
        lru_out.append(h_ctx)
        ckv_out.append(ckvn[:n_c].reshape(nb_ctx, t_ctx, KV_RANK))
        kr_out.append(kr[:n_c, MLA_DN:MLA_DN + MLA_DR].reshape(nb_ctx, t_ctx, MLA_DR))
    y_prompt = x_ctx.reshape(nb_ctx, t_ctx, D_MODEL)
    y_sample = x_lat.reshape(nb_lat, t_lat, D_MODEL)
    return (y_prompt, y_sample, jnp.stack(ret_out, axis=1), jnp.stack(lru_out, axis=1),
            jnp.stack(ckv_out, axis=1), jnp.stack(kr_out, axis=1))


def kernel(x_prompt, x_sample, c, state_ret, state_lru, cache_mla_ckv, cache_mla_krope, c_ctx, ada_w, ada_b, norm_g, w_in, ret_decay, ret_norm, conv_w, conv_b, lru_gate_w, lru_gate_b, lru_lambda, mla_q_norm, mla_kv_norm, mla_w_uq, mla_w_ukv, w_out, router_w, router_b, moe_w_gu, moe_b_gu, moe_w_down, moe_b_down):
    return _forward(x_prompt, x_sample, c, state_ret, state_lru, cache_mla_ckv, cache_mla_krope, c_ctx, ada_w, ada_b,
                    norm_g, w_in, ret_decay, ret_norm, conv_w, conv_b, lru_gate_w, lru_gate_b, lru_lambda,
                    mla_q_norm, mla_kv_norm, mla_w_uq, mla_w_ukv, w_out, router_w, router_b,
                    moe_w_gu, moe_b_gu, moe_w_down, moe_b_down)
```
